```python
import math
import jax, jax.numpy as jnp
from jax import lax
import numpy as np

D_MODEL = 1024
BATCH = 4
SEQ = 4096
DEPTH = 4

N_A = DEPTH // 2
N_B = DEPTH - N_A
POOL_WINDOWS = (2, 4, 8, 16)
POOL_GROUP = D_MODEL // len(POOL_WINDOWS)
N_HEADS = 16
HEAD_DIM = 64
N_KV_GROUPS = 4
HEADS_PER_GROUP = N_HEADS // N_KV_GROUPS
CMP_STRIDE = 16
CMP_LEN = 2 * CMP_STRIDE
CMP_HIDDEN = 128
SLC_BLOCK = 64
N_SELECT = 16
WINDOW = 512
Q_BLOCK = 64
N_BRANCH = 3
ROPE_THETA = 500000.0
ROT_DIM = HEAD_DIM // 4
D_FF = 2816
CONV_W = 3
PLE_DIM = 256

EPS = 1e-6
NEG = -1e30
FORCE_SCORE = 1e4

kernel_name = "hybrid_pool_nsa_yoco_trunk"


def rmsnorm(x, g):
    xf = x.astype(jnp.float32)
    y = xf * lax.rsqrt(jnp.mean(xf * xf, axis=-1, keepdims=True) + EPS) * g.astype(jnp.float32)
    return y.astype(x.dtype)


def rope_tables(pos):
    inv_freq = ROPE_THETA ** (-jnp.arange(0, ROT_DIM, 2, dtype=jnp.float32) / ROT_DIM)
    ang = pos.astype(jnp.float32)[..., None] * inv_freq
    return jnp.cos(ang), jnp.sin(ang)


def rope(x, cos, sin):
    xf = x.astype(jnp.float32)
    half = ROT_DIM // 2
    x1, x2, xp = xf[..., :half], xf[..., half:ROT_DIM], xf[..., ROT_DIM:]
    out = jnp.concatenate([x1 * cos - x2 * sin, x2 * cos + x1 * sin, xp], axis=-1)
    return out.astype(x.dtype)


def masked_softmax(s, mask):
    s = jnp.where(mask, s, NEG)
    m = jnp.max(s, axis=-1, keepdims=True)
    e = jnp.where(mask, jnp.exp(s - m), 0.0)
    return e / jnp.maximum(jnp.sum(e, axis=-1, keepdims=True), 1e-30)


def pool_mixer(h, w, scale):
    B, S, D = h.shape
    c = jnp.cumsum(h.astype(jnp.float32), axis=1)
    c = jnp.pad(c, ((0, 0), (1, 0), (0, 0)))
    t = jnp.arange(S)
    outs = []
    for gi, win in enumerate(POOL_WINDOWS):
        sl = slice(gi * POOL_GROUP, (gi + 1) * POOL_GROUP)
        cg = c[..., sl]
        lo = jnp.maximum(t + 1 - win, 0)
        cnt = jnp.minimum(t + 1, win).astype(jnp.float32)
        avg = (cg[:, t + 1] - cg[:, lo]) / cnt[:, None]
        u = (avg - h[..., sl].astype(jnp.float32)).astype(h.dtype)
        outs.append(jnp.einsum('bsc,cd->bsd', u, w[gi]))
    return jnp.concatenate(outs, axis=-1) * scale


def conv_ffn(h, w_up, conv_w, conv_b, w_down):
    S = h.shape[1]
    u = h @ w_up
    up = jnp.pad(u, ((0, 0), (CONV_W - 1, 0), (0, 0)))
    u = sum(conv_w[j] * up[:, j:j + S] for j in range(CONV_W)) + conv_b
    a, g = u[..., :D_FF], u[..., D_FF:]
    return (jax.nn.silu(g) * a) @ w_down


def compress(blocks, pos_emb, w1, w2):
    z = blocks + pos_emb
    z = z.reshape(z.shape[:3] + (CMP_LEN * HEAD_DIM,))
    return jax.nn.gelu(z @ w1) @ w2


def nsa_shared_kv(x, positions, norm_kv, w_kv, cmp_pos_k, cmp_w1_k, cmp_w2_k, cmp_pos_v, cmp_w1_v, cmp_w2_v):
    B, S, _ = x.shape
    h = rmsnorm(x, norm_kv)
    kv = (h @ w_kv).reshape(B, S, 2 * N_BRANCH, N_KV_GROUPS, HEAD_DIM).transpose(2, 0, 3, 1, 4)
    k_c, v_c, k_s, v_s, k_w, v_w = [kv[i] for i in range(2 * N_BRANCH)]
    cos, sin = rope_tables(positions)
    k_s = rope(k_s, cos[:, None], sin[:, None])
    k_w = rope(k_w, cos[:, None], sin[:, None])
    n_chunk = S // CMP_STRIDE
    n_cmp = n_chunk - 1
    def blocks_of(z):
        ch = z.reshape(B, N_KV_GROUPS, n_chunk, CMP_STRIDE, HEAD_DIM)
        return jnp.concatenate([ch[:, :, :-1], ch[:, :, 1:]], axis=3)
    kc = compress(blocks_of(k_c), cmp_pos_k, cmp_w1_k, cmp_w2_k)
    vc = compress(blocks_of(v_c), cmp_pos_v, cmp_w1_v, cmp_w2_v)
    end_idx = CMP_STRIDE * jnp.arange(n_cmp) + CMP_LEN - 1
    cos_c, sin_c = rope_tables(positions[:, end_idx])
    kc = rope(kc, cos_c[:, None], sin_c[:, None])
    n_slc = S // SLC_BLOCK
    ks_blk = k_s.reshape(B, N_KV_GROUPS, n_slc, SLC_BLOCK, HEAD_DIM)
    vs_blk = v_s.reshape(B, N_KV_GROUPS, n_slc, SLC_BLOCK, HEAD_DIM)
    pad = ((0, 0), (0, 0), (WINDOW, 0), (0, 0))
    kw_pad, vw_pad = jnp.pad(k_w, pad), jnp.pad(v_w, pad)
    return (kc, vc, ks_blk, vs_blk, kw_pad, vw_pad)


def nsa_attention(h, w_in, w_out, cos, sin, kv):
    kc, vc, ks_blk, vs_blk, kw_pad, vw_pad = kv
    B, S, _ = h.shape
    n_cmp = kc.shape[2]
    n_slc = ks_blk.shape[2]
    n_top = min(N_SELECT, n_slc)
    scale = HEAD_DIM ** -0.5
    proj = h @ w_in
    q = proj[..., :N_HEADS * HEAD_DIM].reshape(B, S, N_HEADS, HEAD_DIM)
    q = rope(q, cos[:, :, None], sin[:, :, None])
    q = q.reshape(B, S, N_KV_GROUPS, HEADS_PER_GROUP, HEAD_DIM).transpose(0, 2, 3, 1, 4)
    gates = jax.nn.sigmoid(proj[..., N_HEADS * HEAD_DIM:].astype(jnp.float32))
    gates = gates.reshape(B, S, N_KV_GROUPS, HEADS_PER_GROUP, N_BRANCH).transpose(0, 2, 3, 1, 4)

    c_idx = jnp.arange(n_cmp)
    c_end = CMP_STRIDE * c_idx + CMP_LEN - 1
    blk = jnp.arange(n_slc)
    c_start = CMP_STRIDE * c_idx
    overlap = ((c_start[:, None] < (blk[None] + 1) * SLC_BLOCK)
               & (c_start[:, None] + CMP_LEN > blk[None] * SLC_BLOCK)).astype(jnp.float32)
    gather = jax.vmap(jax.vmap(lambda kb, ix: kb[ix]))

    def block_fn(s):
        t = s + jnp.arange(Q_BLOCK)
        qb = lax.dynamic_slice_in_dim(q, s, Q_BLOCK, axis=3)
        gb = lax.dynamic_slice_in_dim(gates, s, Q_BLOCK, axis=3)
        sc = jnp.einsum('bghqd,bgcd->bghqc', qb, kc).astype(jnp.float32) * scale
        p_c = masked_softmax(sc, c_end[None, :] <= t[:, None])
        o_c = jnp.einsum('bghqc,bgcd->bghqd', p_c.astype(vc.dtype), vc)
        imp = jnp.einsum('bghqc,cn->bgqn', p_c, overlap)
        cur = t // SLC_BLOCK
        forced = (blk[None] == 0) | (blk[None] == cur[:, None]) | (blk[None] == cur[:, None] - 1)
        valid = blk[None] * SLC_BLOCK <= t[:, None]
        score = jnp.where(valid, jnp.where(forced, FORCE_SCORE, imp), NEG)
        vals, idx = lax.top_k(score, n_top)
        ok = vals > 0.5 * NEG
        ks_g = gather(ks_blk, idx).reshape(B, N_KV_GROUPS, Q_BLOCK, n_top * SLC_BLOCK, HEAD_DIM)
        vs_g = gather(vs_blk, idx).reshape(B, N_KV_GROUPS, Q_BLOCK, n_top * SLC_BLOCK, HEAD_DIM)
        kpos = idx[..., None] * SLC_BLOCK + jnp.arange(SLC_BLOCK)
        mask_s = (ok[..., None] & (kpos <= t[:, None, None])).reshape(B, N_KV_GROUPS, Q_BLOCK, n_top * SLC_BLOCK)
        ss = jnp.einsum('bghqd,bgqkd->bghqk', qb, ks_g).astype(jnp.float32) * scale
        p_s = masked_softmax(ss, mask_s[:, :, None])
        o_s = jnp.einsum('bghqk,bgqkd->bghqd', p_s.astype(vs_g.dtype), vs_g)
        kw = lax.dynamic_slice_in_dim(kw_pad, s, WINDOW + Q_BLOCK, axis=2)
        vw = lax.dynamic_slice_in_dim(vw_pad, s, WINDOW + Q_BLOCK, axis=2)
        kpos_w = s - WINDOW + jnp.arange(WINDOW + Q_BLOCK)
        diff = t[:, None] - kpos_w[None]
        mask_w = (diff >= 0) & (diff < WINDOW) & (kpos_w[None] >= 0)
        sw = jnp.einsum('bghqd,bgkd->bghqk', qb, kw).astype(jnp.float32) * scale
        p_w = masked_softmax(sw, mask_w)
        o_w = jnp.einsum('bghqk,bgkd->bghqd', p_w.astype(vw.dtype), vw)
        out = (gb[..., 0:1] * o_c.astype(jnp.float32) + gb[..., 1:2] * o_s.astype(jnp.float32)
               + gb[..., 2:3] * o_w.astype(jnp.float32))
        return out.astype(h.dtype)

    starts = jnp.arange(S // Q_BLOCK) * Q_BLOCK
    o = lax.map(block_fn, starts)
    o = o.transpose(1, 0, 4, 2, 3, 5).reshape(B, S, N_HEADS * HEAD_DIM)
    return o @ w_out


def setup_inputs(seed: int = 0) -> dict:
    key = jax.random.key(seed)
    ks = iter(jax.random.split(key, 40))
    f32 = jnp.float32
    def w(shape, fan):
        return jax.random.normal(next(ks), shape, f32) * (fan ** -0.5)
    def gain(shape):
        return 1.0 + 0.05 * jax.random.normal(next(ks), shape, f32)
    def small(shape, s=0.02):
        return s * jax.random.normal(next(ks), shape, f32)
    q_cols = N_HEADS * HEAD_DIM + N_HEADS * N_BRANCH
    return {
        "x": jax.random.normal(next(ks), (BATCH, SEQ, D_MODEL), f32),
        "p": jax.random.normal(next(ks), (DEPTH, BATCH, SEQ, PLE_DIM), f32),
        "positions": jnp.broadcast_to(jnp.arange(SEQ, dtype=jnp.int32), (BATCH, SEQ)),
        "norm_mix": gain((DEPTH, D_MODEL)),
        "pool_w": w((N_A, len(POOL_WINDOWS), POOL_GROUP, POOL_GROUP), POOL_GROUP),
        "pool_scale": gain((N_A, D_MODEL)),
        "norm_kv": gain((D_MODEL,)),
        "w_kv": w((D_MODEL, 2 * N_BRANCH * N_KV_GROUPS * HEAD_DIM), D_MODEL),
        "cmp_pos_k": small((CMP_LEN, HEAD_DIM), 0.1),
        "cmp_w1_k": w((CMP_LEN * HEAD_DIM, CMP_HIDDEN), CMP_LEN * HEAD_DIM),
        "cmp_w2_k": w((CMP_HIDDEN, HEAD_DIM), CMP_HIDDEN),
        "cmp_pos_v": small((CMP_LEN, HEAD_DIM), 0.1),
        "cmp_w1_v": w((CMP_LEN * HEAD_DIM, CMP_HIDDEN), CMP_LEN * HEAD_DIM),
        "cmp_w2_v": w((CMP_HIDDEN, HEAD_DIM), CMP_HIDDEN),
        "w_in_b": w((N_B, D_MODEL, q_cols), D_MODEL),
        "w_out_b": w((N_B, N_HEADS * HEAD_DIM, D_MODEL), N_HEADS * HEAD_DIM),
        "norm_ffn": gain((DEPTH, D_MODEL)),
        "ffn_up": w((DEPTH, D_MODEL, 2 * D_FF), D_MODEL),
        "ffn_conv": w((DEPTH, CONV_W, 2 * D_FF), CONV_W),
        "ffn_conv_b": small((DEPTH, 2 * D_FF)),
        "ffn_down": w((DEPTH, D_FF, D_MODEL), D_FF),
        "norm_ple": gain((DEPTH, D_MODEL)),
        "ple_gate": w((DEPTH, D_MODEL, D_MODEL), D_MODEL),
        "ple_proj": w((DEPTH, PLE_DIM, D_MODEL), PLE_DIM),
        "norm_final": gain((D_MODEL,)),
    }


def reference(x, p, positions, norm_mix, pool_w, pool_scale, norm_kv, w_kv,
              cmp_pos_k, cmp_w1_k, cmp_w2_k, cmp_pos_v, cmp_w1_v, cmp_w2_v,
              w_in_b, w_out_b, norm_ffn, ffn_up, ffn_conv, ffn_conv_b, ffn_down,
              norm_ple, ple_gate, ple_proj, norm_final):
    cos, sin = rope_tables(positions)
    kv = None
    for i in range(DEPTH):
        if i < N_A:
            x = x + pool_mixer(rmsnorm(x, norm_mix[i]), pool_w[i], pool_scale[i])
        else:
            if i == N_A:
                kv = nsa_shared_kv(x, positions, norm_kv, w_kv, cmp_pos_k, cmp_w1_k, cmp_w2_k,
                                   cmp_pos_v, cmp_w1_v, cmp_w2_v)
            j = i - N_A
            x = x + nsa_attention(rmsnorm(x, norm_mix[i]), w_in_b[j], w_out_b[j], cos, sin, kv)
        x = x + conv_ffn(rmsnorm(x, norm_ffn[i]), ffn_up[i], ffn_conv[i], ffn_conv_b[i], ffn_down[i])
        gate = jax.nn.sigmoid(rmsnorm(x, norm_ple[i]) @ ple_gate[i])
        x = x + gate * (p[i] @ ple_proj[i])
    return rmsnorm(x, norm_final)
```

```python
import functools
import math

import jax
import jax.numpy as jnp
import numpy as np
from jax import lax
from jax.experimental import pallas as pl
from jax.experimental.pallas import tpu as pltpu

D_MODEL = 1024
DEPTH = 4
N_A = DEPTH // 2
POOL_WINDOWS = (2, 4, 8, 16)
POOL_GROUP = D_MODEL // len(POOL_WINDOWS)
POOL_HALO = 16
N_HEADS = 16
HEAD_DIM = 64
N_KV_GROUPS = 4
HEADS_PER_GROUP = N_HEADS // N_KV_GROUPS
CMP_STRIDE = 16
CMP_LEN = 2 * CMP_STRIDE
CMP_HIDDEN = 128
SLC_BLOCK = 64
N_SELECT = 16
WINDOW = 512
N_BRANCH = 3
ROPE_THETA = 500000.0
ROT_DIM = HEAD_DIM // 4
D_FF = 2816
CONV_W = 3
PLE_DIM = 256
EPS = 1e-6
NEG = -1e30
FORCE_SCORE = 1e4

LANES = 128
SUBLANES = 8
VMEM_LIMIT = 48 * 1024 * 1024

TS = 512
FC = 256
TQ = 128
KC = 256

BF16 = jnp.bfloat16
F32 = jnp.float32


def _dot(a, b):
    return jnp.dot(a, b, preferred_element_type=F32)


def _rmsnorm(x, g):
    return x * lax.rsqrt(jnp.mean(x * x, axis=-1, keepdims=True) + EPS) * g


def _sigmoid(x):
    return 1.0 / (1.0 + jnp.exp(-x))


def _rope128(x, c, a, b):
    return x * c + pltpu.roll(x, LANES - ROT_DIM // 2, 1) * a + pltpu.roll(x, ROT_DIM // 2, 1) * b


def _params(n_grid):
    return pltpu.CompilerParams(dimension_semantics=("arbitrary",) * n_grid,
                                vmem_limit_bytes=VMEM_LIMIT)


def _pool_kernel(x_ref, halo_ref, g_ref, w_ref, scale_ref, o_ref):
    s = pl.program_id(1)
    x = x_ref[0]
    g = g_ref[...]
    h = _rmsnorm(x, g)
    hh = jnp.where(s > 0, _rmsnorm(halo_ref[0], g), 0.0)
    ext = jnp.concatenate([hh, h], axis=0)
    sums = {1: ext}
    w = 1
    while w < POOL_WINDOWS[-1]:
        prev = sums[w]
        sums[2 * w] = prev + pltpu.roll(prev, w, 0)
        w *= 2
    t = s * TS + lax.broadcasted_iota(jnp.int32, (TS, 1), 0)
    outs = []
    for gi, win in enumerate(POOL_WINDOWS):
        lo, hi = gi * POOL_GROUP, (gi + 1) * POOL_GROUP
        tot = sums[win][POOL_HALO:, lo:hi]
        cnt = jnp.minimum(t + 1, win).astype(F32)
        u = tot / cnt - h[:, lo:hi]
        outs.append(_dot(u.astype(BF16), w_ref[gi]))
    o_ref[0] = x + jnp.concatenate(outs, axis=-1) * scale_ref[...]


def _pool_layer(x, g, w, scale):
    B, S, D = x.shape
    halo_blocks = TS // POOL_HALO
    return pl.pallas_call(
        _pool_kernel,
        grid=(B, S // TS),
        in_specs=[
            pl.BlockSpec((1, TS, D), lambda b, s: (b, s, 0)),
            pl.BlockSpec((1, POOL_HALO, D), lambda b, s: (b, jnp.maximum(s * halo_blocks - 1, 0), 0)),
            pl.BlockSpec((1, D), lambda b, s: (0, 0)),
            pl.BlockSpec((len(POOL_WINDOWS), POOL_GROUP, POOL_GROUP), lambda b, s: (0, 0, 0)),
            pl.BlockSpec((1, D), lambda b, s: (0, 0)),
        ],
        out_specs=pl.BlockSpec((1, TS, D), lambda b, s: (b, s, 0)),
        out_shape=jax.ShapeDtypeStruct(x.shape, F32),
        compiler_params=_params(2),
        name="pool_mixer",
    )(x, x, g.reshape(1, D), w.astype(BF16), scale.reshape(1, D))


def _ffn_kernel(*refs, has_attn, final_norm):
    it = iter(refs)
    x_ref = next(it)
    if has_attn:
        attn_ref, wout_ref = next(it), next(it)
    (gffn_ref, wa_ref, wg_ref, cwa_ref, cwg_ref, cba_ref, cbg_ref, wdown_ref,
     gple_ref, pgate_ref, p_ref, pproj_ref) = (next(it) for _ in range(12))
    if final_norm:
        gfin_ref = next(it)
    o_ref = next(it)
    xres_ref, h_ref, acc_ref, carry_ref = (next(it) for _ in range(4))

    s = pl.program_id(1)
    c = pl.program_id(2)
    n_c = pl.num_programs(2)

    @pl.when(c == 0)
    def _():
        xin = x_ref[0]
        if has_attn:
            xin = xin + _dot(attn_ref[0], wout_ref[...])
        xres_ref[...] = xin
        h_ref[...] = _rmsnorm(xin, gffn_ref[...]).astype(BF16)
        acc_ref[...] = jnp.zeros_like(acc_ref)

    @pl.when(s == 0)
    def _():
        carry_ref[c] = jnp.zeros(carry_ref.shape[1:], F32)

    h = h_ref[...]

    def conv_half(w_ref, cw_ref, cb_ref, slot):
        u = _dot(h, w_ref[...])
        prev = carry_ref[c, slot]
        carry_ref[c, slot] = u[TS - SUBLANES:, :]
        ext = jnp.concatenate([prev, u], axis=0)
        cw = cw_ref[...]
        out = cw[CONV_W - 1:CONV_W] * u + cb_ref[...]
        for back in range(1, CONV_W):
            shifted = pltpu.roll(ext, back, 0)[SUBLANES:, :]
            out = out + cw[CONV_W - 1 - back:CONV_W - back] * shifted
        return out

    a = conv_half(wa_ref, cwa_ref, cba_ref, 0)
    g = conv_half(wg_ref, cwg_ref, cbg_ref, 1)
    act = (g * _sigmoid(g) * a).astype(BF16)
    acc_ref[...] += _dot(act, wdown_ref[...])

    @pl.when(c == n_c - 1)
    def _():
        x2 = xres_ref[...] + acc_ref[...]
        hp = _rmsnorm(x2, gple_ref[...]).astype(BF16)
        gate = _sigmoid(_dot(hp, pgate_ref[...]))
        emb = _dot(p_ref[0].astype(BF16), pproj_ref[...])
        x3 = x2 + gate * emb
        if final_norm:
            x3 = _rmsnorm(x3, gfin_ref[...])
        o_ref[0] = x3


def _ffn_layer(x, attn, w_out, g_ffn, w_up, conv_w, conv_b, w_down, g_ple, ple_gate, p, ple_proj, g_final):
    B, S, D = x.shape
    n_c = D_FF // FC
    has_attn = attn is not None
    final_norm = g_final is not None
    tok = lambda b, s, c: (b, s, 0)
    const2 = lambda b, s, c: (0, 0)
    w_up = w_up.astype(BF16)
    conv_b = conv_b.reshape(1, 2 * D_FF)
    args, specs = [x], [pl.BlockSpec((1, TS, D), tok)]
    if has_attn:
        args += [attn, w_out.astype(BF16)]
        specs += [pl.BlockSpec((1, TS, D), tok), pl.BlockSpec((D, D), const2)]
    args += [g_ffn.reshape(1, D), w_up, w_up, conv_w, conv_w, conv_b, conv_b, w_down.astype(BF16),
             g_ple.reshape(1, D), ple_gate.astype(BF16), p, ple_proj.astype(BF16)]
    specs += [
        pl.BlockSpec((1, D), const2),
        pl.BlockSpec((D, FC), lambda b, s, c: (0, c)),
        pl.BlockSpec((D, FC), lambda b, s, c: (0, n_c + c)),
        pl.BlockSpec((CONV_W, FC), lambda b, s, c: (0, c)),
        pl.BlockSpec((CONV_W, FC), lambda b, s, c: (0, n_c + c)),
        pl.BlockSpec((1, FC), lambda b, s, c: (0, c)),
        pl.BlockSpec((1, FC), lambda b, s, c: (0, n_c + c)),
        pl.BlockSpec((FC, D), lambda b, s, c: (c, 0)),
        pl.BlockSpec((1, D), const2),
        pl.BlockSpec((D, D), const2),
        pl.BlockSpec((1, TS, PLE_DIM), tok),
        pl.BlockSpec((PLE_DIM, D), const2),
    ]
    if final_norm:
        args.append(g_final.reshape(1, D))
        specs.append(pl.BlockSpec((1, D), const2))
    return pl.pallas_call(
        functools.partial(_ffn_kernel, has_attn=has_attn, final_norm=final_norm),
        grid=(B, S // TS, n_c),
        in_specs=specs,
        out_specs=pl.BlockSpec((1, TS, D), tok),
        out_shape=jax.ShapeDtypeStruct(x.shape, F32),
        scratch_shapes=[
            pltpu.VMEM((TS, D), F32),
            pltpu.VMEM((TS, D), BF16),
            pltpu.VMEM((TS, D), F32),
            pltpu.VMEM((n_c, 2, SUBLANES, FC), F32),
        ],
        compiler_params=_params(3),
        name="ffn_embed",
    )(*args)


def _kv_kernel(x_ref, g_ref, w_ref, rc_ref, ra_ref, rb_ref,
               kc_ref, vc_ref, kts_ref, vs_ref, ktw_ref, vw_ref):
    s = pl.program_id(1)
    h = _rmsnorm(x_ref[0], g_ref[...]).astype(BF16)
    kv = _dot(h, w_ref[...])
    gw = N_KV_GROUPS * HEAD_DIM
    rc, ra, rb = rc_ref[0], ra_ref[0], rb_ref[0]
    lane = lax.broadcasted_iota(jnp.int32, (TS, LANES), 1)
    ones_col = jnp.where(lane == HEAD_DIM, 1.0, 0.0)
    tok = s * TS + lax.broadcasted_iota(jnp.int32, (SLC_BLOCK, TS), 1)
    blk = lax.broadcasted_iota(jnp.int32, (SLC_BLOCK, TS), 0)
    onehot = jnp.where(tok // SLC_BLOCK == blk, 1.0, 0.0).astype(BF16)

    def slabs(i):
        return [kv[:, i * gw + j * LANES:i * gw + (j + 1) * LANES] for j in range(gw // LANES)]

    for g in range(N_KV_GROUPS):
        kc_ref[0, g] = kv[:, 0 * gw + g * HEAD_DIM:0 * gw + (g + 1) * HEAD_DIM]
        vc_ref[0, g] = kv[:, 1 * gw + g * HEAD_DIM:1 * gw + (g + 1) * HEAD_DIM]

    for i, kt_ref, v_ref in ((2, kts_ref, vs_ref), (4, ktw_ref, vw_ref)):
        for j, (kslab, vslab) in enumerate(zip(slabs(i), slabs(i + 1))):
            kt = _rope128(kslab, rc, ra, rb).T
            for half in range(2):
                g = 2 * j + half
                kt_ref[0, g, 0:HEAD_DIM, :] = kt[half * HEAD_DIM:(half + 1) * HEAD_DIM, :].astype(BF16)
                vh = vslab if half == 0 else pltpu.roll(vslab, HEAD_DIM, 1)
                v_ref[0, g] = jnp.where(lane < HEAD_DIM, vh, ones_col).astype(BF16)
        if kt_ref is kts_ref:
            for g in range(N_KV_GROUPS):
                kt_ref[0, g, HEAD_DIM:, :] = onehot


def _kv_project(x, g, w_kv, rc, ra, rb):
    B, S, D = x.shape
    G = N_KV_GROUPS
    tok = lambda b, s: (b, s, 0)
    rows = lambda b, s: (b, 0, s, 0)
    cols = lambda b, s: (b, 0, 0, s)
    return pl.pallas_call(
        _kv_kernel,
        grid=(B, S // TS),
        in_specs=[
            pl.BlockSpec((1, TS, D), tok),
            pl.BlockSpec((1, D), lambda b, s: (0, 0)),
            pl.BlockSpec(w_kv.shape, lambda b, s: (0, 0)),
            pl.BlockSpec((1, TS, LANES), tok),
            pl.BlockSpec((1, TS, LANES), tok),
            pl.BlockSpec((1, TS, LANES), tok),
        ],
        out_specs=[
            pl.BlockSpec((1, G, TS, HEAD_DIM), rows),
            pl.BlockSpec((1, G, TS, HEAD_DIM), rows),
            pl.BlockSpec((1, G, 2 * HEAD_DIM, TS), cols),
            pl.BlockSpec((1, G, TS, LANES), rows),
            pl.BlockSpec((1, G, HEAD_DIM, TS), cols),
            pl.BlockSpec((1, G, TS, LANES), rows),
        ],
        out_shape=[
            jax.ShapeDtypeStruct((B, G, S, HEAD_DIM), F32),
            jax.ShapeDtypeStruct((B, G, S, HEAD_DIM), F32),
            jax.ShapeDtypeStruct((B, G, 2 * HEAD_DIM, S), BF16),
            jax.ShapeDtypeStruct((B, G, S, LANES), BF16),
            jax.ShapeDtypeStruct((B, G, HEAD_DIM, S), BF16),
            jax.ShapeDtypeStruct((B, G, S, LANES), BF16),
        ],
        compiler_params=_params(2),
        name="kv_project",
    )(x, g.reshape(1, D), w_kv.astype(BF16), rc, ra, rb)


def _gelu_tanh(x):
    return 0.5 * x * (1.0 + jnp.tanh(math.sqrt(2.0 / math.pi) * (x + 0.044715 * (x * x * x))))


def _compress_kernel(kf_ref, vf_ref, pk_ref, w1k_ref, w2k_ref, pv_ref, w1v_ref, w2v_ref,
                     rc_ref, ra_ref, rb_ref, kct_ref, vc_ref):
    half = CMP_STRIDE * HEAD_DIM

    def mlp(f_ref, pos_ref, w1_ref, w2_ref):
        flat = f_ref[0, 0]
        pos = pos_ref[...]
        first = _dot((flat + pos[0:1]).astype(BF16), w1_ref[0:half, :])
        second = _dot((flat + pos[1:2]).astype(BF16), w1_ref[half:, :])
        n = flat.shape[0]
        hid = first + pltpu.roll(second, n - 1, 0)
        return _dot(_gelu_tanh(hid).astype(BF16), w2_ref[...])

    kc = _rope128(mlp(kf_ref, pk_ref, w1k_ref, w2k_ref), rc_ref[0], ra_ref[0], rb_ref[0])
    kct_ref[0, 0] = kc.T.astype(BF16)
    vc_ref[0, 0] = mlp(vf_ref, pv_ref, w1v_ref, w2v_ref)[:, :HEAD_DIM].astype(BF16)


def _compress(kc_flat, vc_flat, pos_k, w1_k, w2_k, pos_v, w1_v, w2_v, rc, ra, rb):
    B, G, n_chunk, width = kc_flat.shape
    blk = lambda b, g: (b, g, 0, 0)
    const2 = lambda b, g: (0, 0)
    pad2 = lambda w: jnp.pad(w, ((0, 0), (0, LANES - HEAD_DIM))).astype(BF16)
    return pl.pallas_call(
        _compress_kernel,
        grid=(B, G),
        in_specs=[
            pl.BlockSpec((1, 1, n_chunk, width), blk),
            pl.BlockSpec((1, 1, n_chunk, width), blk),
            pl.BlockSpec((2, width), const2),
            pl.BlockSpec((2 * width, CMP_HIDDEN), const2),
            pl.BlockSpec((CMP_HIDDEN, LANES), const2),
            pl.BlockSpec((2, width), const2),
            pl.BlockSpec((2 * width, CMP_HIDDEN), const2),
            pl.BlockSpec((CMP_HIDDEN, LANES), const2),
            pl.BlockSpec((1, n_chunk, LANES), lambda b, g: (b, 0, 0)),
            pl.BlockSpec((1, n_chunk, LANES), lambda b, g: (b, 0, 0)),
            pl.BlockSpec((1, n_chunk, LANES), lambda b, g: (b, 0, 0)),
        ],
        out_specs=[
            pl.BlockSpec((1, 1, LANES, n_chunk), blk),
            pl.BlockSpec((1, 1, n_chunk, HEAD_DIM), blk),
        ],
        out_shape=[
            jax.ShapeDtypeStruct((B, G, LANES, n_chunk), BF16),
            jax.ShapeDtypeStruct((B, G, n_chunk, HEAD_DIM), BF16),
        ],
        compiler_params=_params(2),
        name="compress",
    )(kc_flat, vc_flat, pos_k.reshape(2, width), w1_k.astype(BF16), pad2(w2_k),
      pos_v.reshape(2, width), w1_v.astype(BF16), pad2(w2_v), rc, ra, rb)


def _qproj_kernel(x_ref, g_ref, w_ref, rc_ref, ra_ref, rb_ref, q_ref, gate_ref):
    h = _rmsnorm(x_ref[0], g_ref[...]).astype(BF16)
    proj = _dot(h, w_ref[...])
    rc, ra, rb = rc_ref[0], ra_ref[0], rb_ref[0]
    lane = lax.broadcasted_iota(jnp.int32, (TS, LANES), 1)
    scale = HEAD_DIM ** -0.5
    for j in range(N_HEADS // 2):
        slab = _rope128(proj[:, j * LANES:(j + 1) * LANES], rc, ra, rb) * scale
        q_ref[0, 2 * j] = jnp.where(lane < HEAD_DIM, slab, 0.0).astype(BF16)
        q_ref[0, 2 * j + 1] = jnp.where(lane < HEAD_DIM, pltpu.roll(slab, HEAD_DIM, 1), 0.0).astype(BF16)
    n_q = N_HEADS * HEAD_DIM
    for g in range(N_KV_GROUPS):
        gate_ref[0, g] = _sigmoid(proj[:, n_q + g * LANES:n_q + (g + 1) * LANES])


def _q_project(x, g, w_in, rc, ra, rb):
    B, S, D = x.shape
    n_q = N_HEADS * HEAD_DIM
    per_group = HEADS_PER_GROUP * N_BRANCH
    gate_w = w_in[:, n_q:].reshape(D, N_KV_GROUPS, per_group)
    gate_w = jnp.pad(gate_w, ((0, 0), (0, 0), (0, LANES - per_group))).reshape(D, N_KV_GROUPS * LANES)
    w = jnp.concatenate([w_in[:, :n_q], gate_w], axis=1).astype(BF16)
    tok = lambda b, s: (b, s, 0)
    return pl.pallas_call(
        _qproj_kernel,
        grid=(B, S // TS),
        in_specs=[
            pl.BlockSpec((1, TS, D), tok),
            pl.BlockSpec((1, D), lambda b, s: (0, 0)),
            pl.BlockSpec(w.shape, lambda b, s: (0, 0)),
            pl.BlockSpec((1, TS, LANES), tok),
            pl.BlockSpec((1, TS, LANES), tok),
            pl.BlockSpec((1, TS, LANES), tok),
        ],
        out_specs=[
            pl.BlockSpec((1, N_HEADS, TS, LANES), lambda b, s: (b, 0, s, 0)),
            pl.BlockSpec((1, N_KV_GROUPS, TS, LANES), lambda b, s: (b, 0, s, 0)),
        ],
        out_shape=[
            jax.ShapeDtypeStruct((B, N_HEADS, S, LANES), BF16),
            jax.ShapeDtypeStruct((B, N_KV_GROUPS, S, LANES), F32),
        ],
        compiler_params=_params(2),
        name="q_project",
    )(x, g.reshape(1, D), w, rc, ra, rb)


def _attn_kernel(q_ref, gate_ref, kct_ref, vc_ref, ovl_ref, kts_ref, vs_ref, ktw_ref, vw_ref,
                 o_ref, m_ref, acc_ref):
    qi = pl.program_id(2)
    s0 = qi * TQ
    hpg = HEADS_PER_GROUP
    rows = hpg * TQ
    n_slc = kts_ref.shape[3] // SLC_BLOCK
    n_cmp_pad = kct_ref.shape[3]

    q3 = q_ref[0]
    q = q3.reshape(rows, LANES)

    sc = _dot(q, kct_ref[0, 0]).reshape(hpg, TQ, n_cmp_pad)
    t_row = s0 + lax.broadcasted_iota(jnp.int32, (TQ, n_cmp_pad), 0)
    c_idx = lax.broadcasted_iota(jnp.int32, (TQ, n_cmp_pad), 1)
    mask_c = (CMP_STRIDE * c_idx + CMP_LEN - 1 <= t_row)[None]
    sc = jnp.where(mask_c, sc, NEG)
    e = jnp.where(mask_c, jnp.exp(sc - jnp.max(sc, axis=-1, keepdims=True)), 0.0)
    p_c = e / jnp.maximum(jnp.sum(e, axis=-1, keepdims=True), 1e-30)
    o_c = _dot(p_c.reshape(rows, n_cmp_pad).astype(BF16), vc_ref[0, 0])
    p_sum = p_c[0]
    for hh in range(1, hpg):
        p_sum = p_sum + p_c[hh]
    imp_t = lax.dot_general(ovl_ref[...], p_sum, (((1,), (1,)), ((), ())),
                            precision=lax.Precision.HIGHEST,
                            preferred_element_type=F32)

    blk = lax.broadcasted_iota(jnp.int32, (n_slc, TQ), 0)
    t_col = s0 + lax.broadcasted_iota(jnp.int32, (n_slc, TQ), 1)
    cur = t_col // SLC_BLOCK
    forced = (blk == 0) | (blk == cur) | (blk == cur - 1)
    valid = blk * SLC_BLOCK <= t_col
    score = jnp.where(valid, jnp.where(forced, FORCE_SCORE, imp_t), NEG)
    rank = jnp.zeros((n_slc, TQ), jnp.int32)
    for m in range(n_slc):
        other = score[m:m + 1, :]
        ahead = (other > score) | ((other == score) & (blk > m))
        rank = rank + ahead.astype(jnp.int32)
    bias_t = jnp.where(valid & (rank < min(N_SELECT, n_slc)), 0.0, NEG)
    bias = jnp.concatenate([jnp.zeros_like(bias_t), bias_t], axis=0).T
    q_sel = (q3.astype(F32) + bias[None]).astype(BF16).reshape(rows, LANES)

    m_ref[...] = jnp.full_like(m_ref, NEG)
    acc_ref[...] = jnp.zeros_like(acc_ref)

    def sel_step(j, masked):
        k0 = pl.multiple_of(j * KC, KC)
        s = _dot(q_sel, kts_ref[0, 0, :, pl.ds(k0, KC)])
        if masked:
            kpos = k0 + lax.broadcasted_iota(jnp.int32, (TQ, KC), 1)
            t = s0 + lax.broadcasted_iota(jnp.int32, (TQ, KC), 0)
            s = jnp.where((kpos <= t)[None], s.reshape(hpg, TQ, KC), NEG).reshape(rows, KC)
        m_old = m_ref[...]
        m_new = jnp.maximum(m_old, jnp.max(s, axis=-1, keepdims=True))
        p = jnp.exp(s - m_new)
        acc_ref[...] = jnp.exp(m_old - m_new) * acc_ref[...] + _dot(p.astype(BF16), vs_ref[0, 0, pl.ds(k0, KC), :])
        m_ref[...] = m_new

    n_chunks = (s0 + TQ + KC - 1) // KC
    lax.fori_loop(0, n_chunks - 1, lambda j, carry: (sel_step(j, False), carry)[1], 0)
    sel_step(n_chunks - 1, True)
    acc = acc_ref[...]
    o_s = acc[:, :HEAD_DIM] / acc[:, HEAD_DIM:HEAD_DIM + 1]

    span = WINDOW + TQ
    w0 = pl.multiple_of(s0, TQ)
    sw = _dot(q[:, :HEAD_DIM], ktw_ref[0, 0, :, pl.ds(w0, span)]).reshape(hpg, TQ, span)
    kpos = s0 - WINDOW + lax.broadcasted_iota(jnp.int32, (TQ, span), 1)
    diff = s0 + lax.broadcasted_iota(jnp.int32, (TQ, span), 0) - kpos
    mask_w = ((diff >= 0) & (diff < WINDOW) & (kpos >= 0))[None]
    sw = jnp.where(mask_w, sw, NEG)
    ew = jnp.where(mask_w, jnp.exp(sw - jnp.max(sw, axis=-1, keepdims=True)), 0.0)
    rw = _dot(ew.reshape(rows, span).astype(BF16), vw_ref[0, 0, pl.ds(w0, span), :])
    o_w = rw[:, :HEAD_DIM] / jnp.maximum(rw[:, HEAD_DIM:HEAD_DIM + 1], 1e-30)

    gates = gate_ref[0, 0]
    outs = []
    for hh in range(hpg):
        r = slice(hh * TQ, (hh + 1) * TQ)
        col = hh * N_BRANCH
        outs.append(gates[:, col:col + 1] * o_c[r] + gates[:, col + 1:col + 2] * o_s[r]
                    + gates[:, col + 2:col + 3] * o_w[r])
    o_ref[0] = jnp.concatenate(outs, axis=-1).astype(o_ref.dtype)


def _attention(q_aug, gates, kct, vc, ovl_t, kts, vs, ktw, vw):
    B, _, S, _ = q_aug.shape
    G = N_KV_GROUPS
    hpg = HEADS_PER_GROUP
    per_bg = lambda b, g, qi: (b, g, 0, 0)
    return pl.pallas_call(
        _attn_kernel,
        grid=(B, G, S // TQ),
        in_specs=[
            pl.BlockSpec((1, hpg, TQ, LANES), lambda b, g, qi: (b, g, qi, 0)),
            pl.BlockSpec((1, 1, TQ, LANES), lambda b, g, qi: (b, g, qi, 0)),
            pl.BlockSpec((1, 1) + kct.shape[2:], per_bg),
            pl.BlockSpec((1, 1) + vc.shape[2:], per_bg),
            pl.BlockSpec(ovl_t.shape, lambda b, g, qi: (0, 0)),
            pl.BlockSpec((1, 1) + kts.shape[2:], per_bg),
            pl.BlockSpec((1, 1) + vs.shape[2:], per_bg),
            pl.BlockSpec((1, 1) + ktw.shape[2:], per_bg),
            pl.BlockSpec((1, 1) + vw.shape[2:], per_bg),
        ],
        out_specs=pl.BlockSpec((1, TQ, hpg * HEAD_DIM), lambda b, g, qi: (b, qi, g)),
        out_shape=jax.ShapeDtypeStruct((B, S, N_HEADS * HEAD_DIM), BF16),
        scratch_shapes=[
            pltpu.VMEM((hpg * TQ, 1), F32),
            pltpu.VMEM((hpg * TQ, LANES), F32),
        ],
        compiler_params=_params(3),
        name="sparse_attention",
    )(q_aug, gates, kct, vc, ovl_t, kts, vs, ktw, vw)


def _rope_tables(pos):
    inv_freq = ROPE_THETA ** (-jnp.arange(0, ROT_DIM, 2, dtype=F32) / ROT_DIM)
    ang = pos.astype(F32)[..., None] * inv_freq
    cos, sin = jnp.cos(ang), jnp.sin(ang)
    one = jnp.ones(cos.shape[:-1] + (HEAD_DIM - ROT_DIM,), F32)
    zero = jnp.zeros(cos.shape[:-1] + (HEAD_DIM - ROT_DIM // 2,), F32)
    c = jnp.concatenate([cos, cos, one], axis=-1)
    a = jnp.concatenate([-sin, zero], axis=-1)
    b = jnp.concatenate([jnp.zeros_like(sin), sin, zero[..., :HEAD_DIM - ROT_DIM]], axis=-1)
    tile = lambda v: jnp.concatenate([v, v], axis=-1)
    return tile(c), tile(a), tile(b)


def _overlap_t(n_chunk, n_slc):
    c_start = CMP_STRIDE * np.arange(n_chunk)
    blk = np.arange(n_slc)
    ovl = ((c_start[None] < (blk[:, None] + 1) * SLC_BLOCK)
           & (c_start[None] + CMP_LEN > blk[:, None] * SLC_BLOCK)
           & (np.arange(n_chunk)[None] < n_chunk - 1))
    return jnp.asarray(ovl.astype(np.float32))


def kernel(x, p, positions, norm_mix, pool_w, pool_scale, norm_kv, w_kv, cmp_pos_k, cmp_w1_k, cmp_w2_k, cmp_pos_v, cmp_w1_v, cmp_w2_v, w_in_b, w_out_b, norm_ffn, ffn_up, ffn_conv, ffn_conv_b, ffn_down, norm_ple, ple_gate, ple_proj, norm_final):
    B, S, D = x.shape
    n_chunk = S // CMP_STRIDE
    rc, ra, rb = _rope_tables(positions)
    end_idx = jnp.minimum(CMP_STRIDE * jnp.arange(n_chunk) + CMP_LEN - 1, S - 1)
    rc_c, ra_c, rb_c = _rope_tables(positions[:, end_idx])
    ovl_t = _overlap_t(n_chunk, S // SLC_BLOCK)

    kv = None
    attn = w_out = None
    for i in range(DEPTH):
        if i < N_A:
            x = _pool_layer(x, norm_mix[i], pool_w[i], pool_scale[i])
            attn = w_out = None
        else:
            if i == N_A:
                kc_raw, vc_raw, kts, vs, ktw, vw = _kv_project(x, norm_kv, w_kv, rc, ra, rb)
                flat = lambda z: z.reshape(B, N_KV_GROUPS, n_chunk, CMP_STRIDE * HEAD_DIM)
                kct, vc = _compress(flat(kc_raw), flat(vc_raw), cmp_pos_k, cmp_w1_k, cmp_w2_k,
                                    cmp_pos_v, cmp_w1_v, cmp_w2_v, rc_c, ra_c, rb_c)
                ktw = jnp.pad(ktw, ((0, 0), (0, 0), (0, 0), (WINDOW, 0)))
                vw = jnp.pad(vw, ((0, 0), (0, 0), (WINDOW, 0), (0, 0)))
                kv = (kct, vc, ovl_t, kts, vs, ktw, vw)
            j = i - N_A
            q_aug, gates = _q_project(x, norm_mix[i], w_in_b[j], rc, ra, rb)
            attn = _attention(q_aug, gates, *kv)
            w_out = w_out_b[j]
        x = _ffn_layer(x, attn, w_out, norm_ffn[i], ffn_up[i], ffn_conv[i], ffn_conv_b[i], ffn_down[i],
                       norm_ple[i], ple_gate[i], p[i], ple_proj[i],
                       norm_final if i == DEPTH - 1 else None)
    return x
```

```python
import functools
import math

import jax
import jax.numpy as jnp
import numpy as np
from jax import lax
from jax.experimental import pallas as pl
from jax.experimental.pallas import tpu as pltpu

D_MODEL = 1024
DEPTH = 4
N_A = DEPTH // 2
POOL_WINDOWS = (2, 4, 8, 16)
POOL_GROUP = D_MODEL // len(POOL_WINDOWS)
POOL_HALO = 16
N_HEADS = 16
HEAD_DIM = 64
N_KV_GROUPS = 4
HEADS_PER_GROUP = N_HEADS // N_KV_GROUPS
CMP_STRIDE = 16
CMP_LEN = 2 * CMP_STRIDE
CMP_HIDDEN = 128
SLC_BLOCK = 64
N_SELECT = 16
WINDOW = 512
N_BRANCH = 3
ROPE_THETA = 500000.0
ROT_DIM = HEAD_DIM // 4
D_FF = 2816
CONV_W = 3
PLE_DIM = 256
EPS = 1e-6
NEG = -1e30
FORCE_SCORE = 1e4

LANES = 128
SUBLANES = 8
VMEM_LIMIT = 48 * 1024 * 1024

TS = 512
FC = 256
TQ = 256
KC = 512
GATE_ROWS = 16

BF16 = jnp.bfloat16
F32 = jnp.float32


def _dot(a, b):
    return jnp.dot(a, b, preferred_element_type=F32)


def _rmsnorm(x, g):
    return x * lax.rsqrt(jnp.mean(x * x, axis=-1, keepdims=True) + EPS) * g


def _sigmoid(x):
    return 1.0 / (1.0 + jnp.exp(-x))


def _rope128(x, c, a, b):
    return x * c + pltpu.roll(x, LANES - ROT_DIM // 2, 1) * a + pltpu.roll(x, ROT_DIM // 2, 1) * b


def _params(n_grid):
    return pltpu.CompilerParams(dimension_semantics=("arbitrary",) * n_grid,
                                vmem_limit_bytes=VMEM_LIMIT)


def _pool_kernel(x_ref, halo_ref, g_ref, w_ref, scale_ref, o_ref):
    s = pl.program_id(1)
    x = x_ref[0]
    g = g_ref[...]
    h = _rmsnorm(x, g)
    hh = jnp.where(s > 0, _rmsnorm(halo_ref[0], g), 0.0)
    ext = jnp.concatenate([hh, h], axis=0)
    sums = {1: ext}
    w = 1
    while w < POOL_WINDOWS[-1]:
        prev = sums[w]
        sums[2 * w] = prev + pltpu.roll(prev, w, 0)
        w *= 2
    t = s * TS + lax.broadcasted_iota(jnp.int32, (TS, 1), 0)
    outs = []
    for gi, win in enumerate(POOL_WINDOWS):
        lo, hi = gi * POOL_GROUP, (gi + 1) * POOL_GROUP
        tot = sums[win][POOL_HALO:, lo:hi]
        cnt = jnp.minimum(t + 1, win).astype(F32)
        u = tot / cnt - h[:, lo:hi]
        outs.append(_dot(u.astype(BF16), w_ref[gi]))
    o_ref[0] = x + jnp.concatenate(outs, axis=-1) * scale_ref[...]


def _pool_layer(x, g, w, scale):
    B, S, D = x.shape
    halo_blocks = TS // POOL_HALO
    return pl.pallas_call(
        _pool_kernel,
        grid=(B, S // TS),
        in_specs=[
            pl.BlockSpec((1, TS, D), lambda b, s: (b, s, 0)),
            pl.BlockSpec((1, POOL_HALO, D), lambda b, s: (b, jnp.maximum(s * halo_blocks - 1, 0), 0)),
            pl.BlockSpec((1, D), lambda b, s: (0, 0)),
            pl.BlockSpec((len(POOL_WINDOWS), POOL_GROUP, POOL_GROUP), lambda b, s: (0, 0, 0)),
            pl.BlockSpec((1, D), lambda b, s: (0, 0)),
        ],
        out_specs=pl.BlockSpec((1, TS, D), lambda b, s: (b, s, 0)),
        out_shape=jax.ShapeDtypeStruct(x.shape, F32),
        compiler_params=_params(2),
        name="pool_mixer",
    )(x, x, g.reshape(1, D), w.astype(BF16), scale.reshape(1, D))


def _ffn_kernel(*refs, has_attn, final_norm):
    it = iter(refs)
    x_ref = next(it)
    if has_attn:
        attn_ref, wout_ref = next(it), next(it)
    (gffn_ref, wa_ref, wg_ref, cwa_ref, cwg_ref, cba_ref, cbg_ref, wdown_ref,
     gple_ref, pgate_ref, p_ref, pproj_ref) = (next(it) for _ in range(12))
    if final_norm:
        gfin_ref = next(it)
    o_ref = next(it)
    xres_ref, h_ref, acc_ref, carry_ref = (next(it) for _ in range(4))

    s = pl.program_id(1)
    c = pl.program_id(2)
    n_c = pl.num_programs(2)

    @pl.when(c == 0)
    def _():
        xin = x_ref[0]
        if has_attn:
            xin = xin + _dot(attn_ref[0], wout_ref[...])
        xres_ref[...] = xin
        h_ref[...] = _rmsnorm(xin, gffn_ref[...]).astype(BF16)
        acc_ref[...] = jnp.zeros_like(acc_ref)

    @pl.when(s == 0)
    def _():
        carry_ref[c] = jnp.zeros(carry_ref.shape[1:], F32)

    h = h_ref[...]

    def conv_half(w_ref, cw_ref, cb_ref, slot):
        u = _dot(h, w_ref[...])
        prev = carry_ref[c, slot]
        carry_ref[c, slot] = u[TS - SUBLANES:, :]
        ext = jnp.concatenate([prev, u], axis=0)
        cw = cw_ref[...]
        out = cw[CONV_W - 1:CONV_W] * u + cb_ref[...]
        for back in range(1, CONV_W):
            shifted = pltpu.roll(ext, back, 0)[SUBLANES:, :]
            out = out + cw[CONV_W - 1 - back:CONV_W - back] * shifted
        return out

    a = conv_half(wa_ref, cwa_ref, cba_ref, 0)
    g = conv_half(wg_ref, cwg_ref, cbg_ref, 1)
    act = (g * _sigmoid(g) * a).astype(BF16)
    acc_ref[...] += _dot(act, wdown_ref[...])

    @pl.when(c == n_c - 1)
    def _():
        x2 = xres_ref[...] + acc_ref[...]
        hp = _rmsnorm(x2, gple_ref[...]).astype(BF16)
        gate = _sigmoid(_dot(hp, pgate_ref[...]))
        emb = _dot(p_ref[0].astype(BF16), pproj_ref[...])
        x3 = x2 + gate * emb
        if final_norm:
            x3 = _rmsnorm(x3, gfin_ref[...])
        o_ref[0] = x3


def _ffn_layer(x, attn, w_out, g_ffn, w_up, conv_w, conv_b, w_down, g_ple, ple_gate, p, ple_proj, g_final):
    B, S, D = x.shape
    n_c = D_FF // FC
    has_attn = attn is not None
    final_norm = g_final is not None
    tok = lambda b, s, c: (b, s, 0)
    const2 = lambda b, s, c: (0, 0)
    w_up = w_up.astype(BF16)
    conv_b = conv_b.reshape(1, 2 * D_FF)
    args, specs = [x], [pl.BlockSpec((1, TS, D), tok)]
    if has_attn:
        args += [attn, w_out.astype(BF16)]
        specs += [pl.BlockSpec((1, TS, D), tok), pl.BlockSpec((D, D), const2)]
    args += [g_ffn.reshape(1, D), w_up, w_up, conv_w, conv_w, conv_b, conv_b, w_down.astype(BF16),
             g_ple.reshape(1, D), ple_gate.astype(BF16), p, ple_proj.astype(BF16)]
    specs += [
        pl.BlockSpec((1, D), const2),
        pl.BlockSpec((D, FC), lambda b, s, c: (0, c)),
        pl.BlockSpec((D, FC), lambda b, s, c: (0, n_c + c)),
        pl.BlockSpec((CONV_W, FC), lambda b, s, c: (0, c)),
        pl.BlockSpec((CONV_W, FC), lambda b, s, c: (0, n_c + c)),
        pl.BlockSpec((1, FC), lambda b, s, c: (0, c)),
        pl.BlockSpec((1, FC), lambda b, s, c: (0, n_c + c)),
        pl.BlockSpec((FC, D), lambda b, s, c: (c, 0)),
        pl.BlockSpec((1, D), const2),
        pl.BlockSpec((D, D), const2),
        pl.BlockSpec((1, TS, PLE_DIM), tok),
        pl.BlockSpec((PLE_DIM, D), const2),
    ]
    if final_norm:
        args.append(g_final.reshape(1, D))
        specs.append(pl.BlockSpec((1, D), const2))
    return pl.pallas_call(
        functools.partial(_ffn_kernel, has_attn=has_attn, final_norm=final_norm),
        grid=(B, S // TS, n_c),
        in_specs=specs,
        out_specs=pl.BlockSpec((1, TS, D), tok),
        out_shape=jax.ShapeDtypeStruct(x.shape, F32),
        scratch_shapes=[
            pltpu.VMEM((TS, D), F32),
            pltpu.VMEM((TS, D), BF16),
            pltpu.VMEM((TS, D), F32),
            pltpu.VMEM((n_c, 2, SUBLANES, FC), F32),
        ],
        compiler_params=_params(3),
        name="ffn_embed",
    )(*args)


def _kv_kernel(x_ref, g_ref, w_ref, rc_ref, ra_ref, rb_ref,
               kc_ref, vc_ref, ks_ref, vts_ref, kw_ref, vtw_ref):
    s = pl.program_id(1)
    h = _rmsnorm(x_ref[0], g_ref[...]).astype(BF16)
    kv = _dot(h, w_ref[...])
    gw = N_KV_GROUPS * HEAD_DIM
    rc, ra, rb = rc_ref[0], ra_ref[0], rb_ref[0]
    lane = lax.broadcasted_iota(jnp.int32, (TS, LANES), 1)
    tok = s * TS + lax.broadcasted_iota(jnp.int32, (TS, LANES), 0)
    onehot = jnp.where(lane - HEAD_DIM == tok // SLC_BLOCK, 1.0, 0.0)
    row = lax.broadcasted_iota(jnp.int32, (LANES - HEAD_DIM, TS), 0)
    ones_row = jnp.where(row == 0, 1.0, 0.0).astype(BF16)

    def slabs(i):
        return [kv[:, i * gw + j * LANES:i * gw + (j + 1) * LANES] for j in range(gw // LANES)]

    for g in range(N_KV_GROUPS):
        kc_ref[0, g] = kv[:, 0 * gw + g * HEAD_DIM:0 * gw + (g + 1) * HEAD_DIM]
        vc_ref[0, g] = kv[:, 1 * gw + g * HEAD_DIM:1 * gw + (g + 1) * HEAD_DIM]

    for i, k_ref, vt_ref in ((2, ks_ref, vts_ref), (4, kw_ref, vtw_ref)):
        for j, (kslab, vslab) in enumerate(zip(slabs(i), slabs(i + 1))):
            kr = _rope128(kslab, rc, ra, rb)
            vt = vslab.T
            for half in range(2):
                g = 2 * j + half
                kh = kr if half == 0 else pltpu.roll(kr, HEAD_DIM, 1)
                upper = onehot if k_ref is ks_ref else 0.0
                k_ref[0, g] = jnp.where(lane < HEAD_DIM, kh, upper).astype(BF16)
                vt_ref[0, g, 0:HEAD_DIM, :] = vt[half * HEAD_DIM:(half + 1) * HEAD_DIM, :].astype(BF16)
                vt_ref[0, g, HEAD_DIM:, :] = ones_row


def _kv_project(x, g, w_kv, rc, ra, rb):
    B, S, D = x.shape
    G = N_KV_GROUPS
    tok = lambda b, s: (b, s, 0)
    rows = lambda b, s: (b, 0, s, 0)
    cols = lambda b, s: (b, 0, 0, s)
    return pl.pallas_call(
        _kv_kernel,
        grid=(B, S // TS),
        in_specs=[
            pl.BlockSpec((1, TS, D), tok),
            pl.BlockSpec((1, D), lambda b, s: (0, 0)),
            pl.BlockSpec(w_kv.shape, lambda b, s: (0, 0)),
            pl.BlockSpec((1, TS, LANES), tok),
            pl.BlockSpec((1, TS, LANES), tok),
            pl.BlockSpec((1, TS, LANES), tok),
        ],
        out_specs=[
            pl.BlockSpec((1, G, TS, HEAD_DIM), rows),
            pl.BlockSpec((1, G, TS, HEAD_DIM), rows),
            pl.BlockSpec((1, G, TS, LANES), rows),
            pl.BlockSpec((1, G, LANES, TS), cols),
            pl.BlockSpec((1, G, TS, LANES), rows),
            pl.BlockSpec((1, G, LANES, TS), cols),
        ],
        out_shape=[
            jax.ShapeDtypeStruct((B, G, S, HEAD_DIM), F32),
            jax.ShapeDtypeStruct((B, G, S, HEAD_DIM), F32),
            jax.ShapeDtypeStruct((B, G, S, LANES), BF16),
            jax.ShapeDtypeStruct((B, G, LANES, S), BF16),
            jax.ShapeDtypeStruct((B, G, S, LANES), BF16),
            jax.ShapeDtypeStruct((B, G, LANES, S), BF16),
        ],
        compiler_params=_params(2),
        name="kv_project",
    )(x, g.reshape(1, D), w_kv.astype(BF16), rc, ra, rb)


def _gelu_tanh(x):
    return 0.5 * x * (1.0 + jnp.tanh(math.sqrt(2.0 / math.pi) * (x + 0.044715 * (x * x * x))))


def _compress_kernel(kf_ref, vf_ref, pk_ref, w1k_ref, w2k_ref, pv_ref, w1v_ref, w2v_ref,
                     rc_ref, ra_ref, rb_ref, kc_ref, vct_ref):
    half = CMP_STRIDE * HEAD_DIM

    def hidden(f_ref, pos_ref, w1_ref):
        flat = f_ref[0, 0]
        pos = pos_ref[...]
        first = _dot((flat + pos[0:1]).astype(BF16), w1_ref[0:half, :])
        second = _dot((flat + pos[1:2]).astype(BF16), w1_ref[half:, :])
        n = flat.shape[0]
        hid = first + pltpu.roll(second, n - 1, 0)
        return _gelu_tanh(hid).astype(BF16)

    kc = _dot(hidden(kf_ref, pk_ref, w1k_ref), w2k_ref[...])
    kc_ref[0, 0] = _rope128(kc, rc_ref[0], ra_ref[0], rb_ref[0])[:, :HEAD_DIM].astype(BF16)
    vct = lax.dot_general(w2v_ref[...], hidden(vf_ref, pv_ref, w1v_ref), (((1,), (1,)), ((), ())),
                          preferred_element_type=F32)
    vct_ref[0, 0] = vct[:HEAD_DIM, :].astype(BF16)


def _compress(kc_flat, vc_flat, pos_k, w1_k, w2_k, pos_v, w1_v, w2_v, rc, ra, rb):
    B, G, n_chunk, width = kc_flat.shape
    blk = lambda b, g: (b, g, 0, 0)
    const2 = lambda b, g: (0, 0)
    pad2 = lambda w: jnp.pad(w, ((0, 0), (0, LANES - HEAD_DIM))).astype(BF16)
    return pl.pallas_call(
        _compress_kernel,
        grid=(B, G),
        in_specs=[
            pl.BlockSpec((1, 1, n_chunk, width), blk),
            pl.BlockSpec((1, 1, n_chunk, width), blk),
            pl.BlockSpec((2, width), const2),
            pl.BlockSpec((2 * width, CMP_HIDDEN), const2),
            pl.BlockSpec((CMP_HIDDEN, LANES), const2),
            pl.BlockSpec((2, width), const2),
            pl.BlockSpec((2 * width, CMP_HIDDEN), const2),
            pl.BlockSpec((CMP_HIDDEN, LANES), const2),
            pl.BlockSpec((1, n_chunk, LANES), lambda b, g: (b, 0, 0)),
            pl.BlockSpec((1, n_chunk, LANES), lambda b, g: (b, 0, 0)),
            pl.BlockSpec((1, n_chunk, LANES), lambda b, g: (b, 0, 0)),
        ],
        out_specs=[
            pl.BlockSpec((1, 1, n_chunk, HEAD_DIM), blk),
            pl.BlockSpec((1, 1, HEAD_DIM, n_chunk), blk),
        ],
        out_shape=[
            jax.ShapeDtypeStruct((B, G, n_chunk, HEAD_DIM), BF16),
            jax.ShapeDtypeStruct((B, G, HEAD_DIM, n_chunk), BF16),
        ],
        compiler_params=_params(2),
        name="compress",
    )(kc_flat, vc_flat, pos_k.reshape(2, width), w1_k.astype(BF16), pad2(w2_k),
      pos_v.reshape(2, width), w1_v.astype(BF16), pad2(w2_v).T, rc, ra, rb)


def _qproj_kernel(x_ref, g_ref, w_ref, rc_ref, ra_ref, rb_ref, q_ref, gate_ref):
    h = _rmsnorm(x_ref[0], g_ref[...]).astype(BF16)
    proj = _dot(h, w_ref[...])
    rc, ra, rb = rc_ref[0], ra_ref[0], rb_ref[0]
    scale = HEAD_DIM ** -0.5
    for j in range(N_HEADS // 2):
        qt = (_rope128(proj[:, j * LANES:(j + 1) * LANES], rc, ra, rb) * scale).T
        q_ref[0, 2 * j] = qt[:HEAD_DIM].astype(BF16)
        q_ref[0, 2 * j + 1] = qt[HEAD_DIM:].astype(BF16)
    n_q = N_HEADS * HEAD_DIM
    for g in range(N_KV_GROUPS):
        gt = _sigmoid(proj[:, n_q + g * LANES:n_q + (g + 1) * LANES]).T
        gate_ref[0, g] = gt[:gate_ref.shape[2]]


def _q_project(x, g, w_in, rc, ra, rb):
    B, S, D = x.shape
    n_q = N_HEADS * HEAD_DIM
    per_group = HEADS_PER_GROUP * N_BRANCH
    gate_w = w_in[:, n_q:].reshape(D, N_KV_GROUPS, per_group)
    gate_w = jnp.pad(gate_w, ((0, 0), (0, 0), (0, LANES - per_group))).reshape(D, N_KV_GROUPS * LANES)
    w = jnp.concatenate([w_in[:, :n_q], gate_w], axis=1).astype(BF16)
    tok = lambda b, s: (b, s, 0)
    return pl.pallas_call(
        _qproj_kernel,
        grid=(B, S // TS),
        in_specs=[
            pl.BlockSpec((1, TS, D), tok),
            pl.BlockSpec((1, D), lambda b, s: (0, 0)),
            pl.BlockSpec(w.shape, lambda b, s: (0, 0)),
            pl.BlockSpec((1, TS, LANES), tok),
            pl.BlockSpec((1, TS, LANES), tok),
            pl.BlockSpec((1, TS, LANES), tok),
        ],
        out_specs=[
            pl.BlockSpec((1, N_HEADS, HEAD_DIM, TS), lambda b, s: (b, 0, 0, s)),
            pl.BlockSpec((1, N_KV_GROUPS, GATE_ROWS, TS), lambda b, s: (b, 0, 0, s)),
        ],
        out_shape=[
            jax.ShapeDtypeStruct((B, N_HEADS, HEAD_DIM, S), BF16),
            jax.ShapeDtypeStruct((B, N_KV_GROUPS, GATE_ROWS, S), F32),
        ],
        compiler_params=_params(2),
        name="q_project",
    )(x, g.reshape(1, D), w, rc, ra, rb)


def _attn_kernel(q_ref, gate_ref, kc_ref, vct_ref, ovl_ref, ks_ref, vts_ref, kw_ref, vtw_ref,
                 o_ref, m_ref, acc_ref):
    qi = pl.program_id(2)
    s0 = qi * TQ
    hpg = HEADS_PER_GROUP
    n_slc = ovl_ref.shape[0]

    qt = jnp.concatenate([q_ref[0, hh] for hh in range(hpg)], axis=1)

    def t_of(shape):
        return s0 + (lax.broadcasted_iota(jnp.int32, shape, 1) & (TQ - 1))

    sc = _dot(kc_ref[0, 0], qt)
    c_end = CMP_STRIDE * lax.broadcasted_iota(jnp.int32, sc.shape, 0) + CMP_LEN - 1
    mask_c = c_end <= t_of(sc.shape)
    sc = jnp.where(mask_c, sc, NEG)
    e = jnp.where(mask_c, jnp.exp(sc - jnp.max(sc, axis=0, keepdims=True)), 0.0)
    p_c = e * (1.0 / jnp.maximum(jnp.sum(e, axis=0, keepdims=True), 1e-30))
    oc_t = _dot(vct_ref[0, 0], p_c.astype(BF16))
    p_sum = p_c[:, 0:TQ]
    for hh in range(1, hpg):
        p_sum = p_sum + p_c[:, hh * TQ:(hh + 1) * TQ]
    imp_t = jnp.dot(ovl_ref[...], p_sum, precision=lax.Precision.HIGHEST,
                    preferred_element_type=F32)

    blk = lax.broadcasted_iota(jnp.int32, (n_slc, TQ), 0)
    t_col = s0 + lax.broadcasted_iota(jnp.int32, (n_slc, TQ), 1)
    cur = t_col // SLC_BLOCK
    forced = (blk == 0) | (blk == cur) | (blk == cur - 1)
    valid = blk * SLC_BLOCK <= t_col
    score = jnp.where(valid, jnp.where(forced, FORCE_SCORE, imp_t), NEG)
    groups = [score[r:r + SUBLANES] for r in range(0, n_slc, SUBLANES)]
    blk_in = lax.broadcasted_iota(jnp.int32, (SUBLANES, TQ), 0)
    ranks = [jnp.zeros((SUBLANES, TQ), F32) for _ in groups]
    for m in range(n_slc):
        other = score[m:m + 1, :]
        for gi, grp in enumerate(groups):
            lo = gi * SUBLANES
            if lo > m:
                ahead = other >= grp
            elif lo + SUBLANES - 1 <= m:
                ahead = other > grp
            else:
                ahead = (other > grp) | ((other == grp) & (blk_in + lo > m))
            ranks[gi] = ranks[gi] + jnp.where(ahead, 1.0, 0.0)
    rank = jnp.concatenate(ranks, axis=0)
    bias_t = jnp.where(valid & (rank < min(N_SELECT, n_slc)), 0.0, NEG).astype(BF16)
    qt_sel = jnp.concatenate([qt, jnp.concatenate([bias_t] * hpg, axis=1)], axis=0)

    tri = lax.broadcasted_iota(jnp.int32, (TQ, hpg * TQ), 0) <= (
        lax.broadcasted_iota(jnp.int32, (TQ, hpg * TQ), 1) & (TQ - 1))

    m_ref[...] = jnp.full_like(m_ref, NEG)
    acc_ref[...] = jnp.zeros_like(acc_ref)

    def sel_step(k0, size, mask=None):
        s = _dot(ks_ref[0, 0, pl.ds(k0, size), :], qt_sel)
        if mask is not None:
            s = jnp.where(mask, s, NEG)
        m_old = m_ref[...]
        m_new = jnp.maximum(m_old, jnp.max(s, axis=0, keepdims=True))
        p = jnp.exp(s - m_new).astype(BF16)
        acc_ref[...] = jnp.exp(m_old - m_new) * acc_ref[...] + _dot(vts_ref[0, 0, :, pl.ds(k0, size)], p)
        m_ref[...] = m_new

    def sel_body(j, carry):
        sel_step(pl.multiple_of(j * KC, KC), KC)
        return carry

    lax.fori_loop(0, s0 // KC, sel_body, 0)
    for r in range(KC // TQ - 1, 0, -1):
        @pl.when((qi % (KC // TQ)) >= r)
        def _():
            sel_step(pl.multiple_of(s0 - r * TQ, TQ), TQ)
    sel_step(pl.multiple_of(s0, TQ), TQ, tri)
    acc = acc_ref[...]
    os_t = acc[:HEAD_DIM] * (1.0 / acc[HEAD_DIM:HEAD_DIM + 1])

    span = WINDOW + TQ
    w0 = pl.multiple_of(s0, TQ)
    flag_row = lax.broadcasted_iota(jnp.int32, (LANES - HEAD_DIM, hpg * TQ), 0) == 0
    qt_win = jnp.concatenate([qt, jnp.where(flag_row, NEG, 0.0).astype(BF16)], axis=0)
    sw = _dot(kw_ref[0, 0, pl.ds(w0, span), :], qt_win)
    sw = jnp.concatenate([jnp.where(tri, NEG, sw[:TQ]), sw[TQ:WINDOW], jnp.where(tri, sw[WINDOW:], NEG)], axis=0)
    ew = jnp.exp(sw - jnp.max(sw, axis=0, keepdims=True)).astype(BF16)
    rw = _dot(vtw_ref[0, 0, :, pl.ds(w0, span)], ew)
    ow_t = rw[:HEAD_DIM] * (1.0 / jnp.maximum(rw[HEAD_DIM:HEAD_DIM + 1], 1e-30))

    gates = gate_ref[0, 0]
    outs = []
    for hh in range(hpg):
        c = slice(hh * TQ, (hh + 1) * TQ)
        r = hh * N_BRANCH
        outs.append(gates[r:r + 1] * oc_t[:, c] + gates[r + 1:r + 2] * os_t[:, c]
                    + gates[r + 2:r + 3] * ow_t[:, c])
    o_ref[0] = jnp.concatenate(outs, axis=0).T.astype(o_ref.dtype)


def _attention(q_t, gates_t, kc, vct, ovl_t, ks, vts, kw, vtw):
    B, _, _, S = q_t.shape
    G = N_KV_GROUPS
    hpg = HEADS_PER_GROUP
    per_bg = lambda b, g, qi: (b, g, 0, 0)
    return pl.pallas_call(
        _attn_kernel,
        grid=(B, G, S // TQ),
        in_specs=[
            pl.BlockSpec((1, hpg, HEAD_DIM, TQ), lambda b, g, qi: (b, g, 0, qi)),
            pl.BlockSpec((1, 1, GATE_ROWS, TQ), lambda b, g, qi: (b, g, 0, qi)),
            pl.BlockSpec((1, 1) + kc.shape[2:], per_bg),
            pl.BlockSpec((1, 1) + vct.shape[2:], per_bg),
            pl.BlockSpec(ovl_t.shape, lambda b, g, qi: (0, 0)),
            pl.BlockSpec((1, 1) + ks.shape[2:], per_bg),
            pl.BlockSpec((1, 1) + vts.shape[2:], per_bg),
            pl.BlockSpec((1, 1) + kw.shape[2:], per_bg),
            pl.BlockSpec((1, 1) + vtw.shape[2:], per_bg),
        ],
        out_specs=pl.BlockSpec((1, TQ, hpg * HEAD_DIM), lambda b, g, qi: (b, qi, g)),
        out_shape=jax.ShapeDtypeStruct((B, S, N_HEADS * HEAD_DIM), BF16),
        scratch_shapes=[
            pltpu.VMEM((1, hpg * TQ), F32),
            pltpu.VMEM((LANES, hpg * TQ), F32),
        ],
        compiler_params=_params(3),
        name="sparse_attention",
    )(q_t, gates_t, kc, vct, ovl_t, ks, vts, kw, vtw)


def _rope_tables(pos):
    inv_freq = ROPE_THETA ** (-jnp.arange(0, ROT_DIM, 2, dtype=F32) / ROT_DIM)
    ang = pos.astype(F32)[..., None] * inv_freq
    cos, sin = jnp.cos(ang), jnp.sin(ang)
    one = jnp.ones(cos.shape[:-1] + (HEAD_DIM - ROT_DIM,), F32)
    zero = jnp.zeros(cos.shape[:-1] + (HEAD_DIM - ROT_DIM // 2,), F32)
    c = jnp.concatenate([cos, cos, one], axis=-1)
    a = jnp.concatenate([-sin, zero], axis=-1)
    b = jnp.concatenate([jnp.zeros_like(sin), sin, zero[..., :HEAD_DIM - ROT_DIM]], axis=-1)
    tile = lambda v: jnp.concatenate([v, v], axis=-1)
    return tile(c), tile(a), tile(b)


def _overlap_t(n_chunk, n_slc):
    c_start = CMP_STRIDE * np.arange(n_chunk)
    blk = np.arange(n_slc)
    ovl = ((c_start[None] < (blk[:, None] + 1) * SLC_BLOCK)
           & (c_start[None] + CMP_LEN > blk[:, None] * SLC_BLOCK)
           & (np.arange(n_chunk)[None] < n_chunk - 1))
    return jnp.asarray(ovl.astype(np.float32))


def kernel(x, p, positions, norm_mix, pool_w, pool_scale, norm_kv, w_kv, cmp_pos_k, cmp_w1_k, cmp_w2_k, cmp_pos_v, cmp_w1_v, cmp_w2_v, w_in_b, w_out_b, norm_ffn, ffn_up, ffn_conv, ffn_conv_b, ffn_down, norm_ple, ple_gate, ple_proj, norm_final):
    B, S, D = x.shape
    n_chunk = S // CMP_STRIDE
    assert S // SLC_BLOCK == LANES - HEAD_DIM
    rc, ra, rb = _rope_tables(positions)
    end_idx = jnp.minimum(CMP_STRIDE * jnp.arange(n_chunk) + CMP_LEN - 1, S - 1)
    rc_c, ra_c, rb_c = _rope_tables(positions[:, end_idx])
    ovl_t = _overlap_t(n_chunk, S // SLC_BLOCK)

    kv = None
    attn = w_out = None
    for i in range(DEPTH):
        if i < N_A:
            x = _pool_layer(x, norm_mix[i], pool_w[i], pool_scale[i])
            attn = w_out = None
        else:
            if i == N_A:
                kc_raw, vc_raw, ks, vts, kw, vtw = _kv_project(x, norm_kv, w_kv, rc, ra, rb)
                flat = lambda z: z.reshape(B, N_KV_GROUPS, n_chunk, CMP_STRIDE * HEAD_DIM)
                kc, vct = _compress(flat(kc_raw), flat(vc_raw), cmp_pos_k, cmp_w1_k, cmp_w2_k,
                                    cmp_pos_v, cmp_w1_v, cmp_w2_v, rc_c, ra_c, rb_c)
                flag = jnp.zeros((LANES,), BF16).at[HEAD_DIM].set(1.0)
                kw = jnp.concatenate([jnp.broadcast_to(flag, kw.shape[:2] + (WINDOW, LANES)), kw], axis=2)
                vtw = jnp.pad(vtw, ((0, 0), (0, 0), (0, 0), (WINDOW, 0)))
                kv = (kc, vct, ovl_t, ks, vts, kw, vtw)
            j = i - N_A
            q_t, gates_t = _q_project(x, norm_mix[i], w_in_b[j], rc, ra, rb)
            attn = _attention(q_t, gates_t, *kv)
            w_out = w_out_b[j]
        x = _ffn_layer(x, attn, w_out, norm_ffn[i], ffn_up[i], ffn_conv[i], ffn_conv_b[i], ffn_down[i],
                       norm_ple[i], ple_gate[i], p[i], ple_proj[i],
                       norm_final if i == DEPTH - 1 else None)
    return x
```

```python
import functools
import math

import jax
import jax.numpy as jnp
import numpy as np
from jax import lax
from jax.experimental import pallas as pl
from jax.experimental.pallas import tpu as pltpu

D_MODEL = 1024
DEPTH = 4
N_A = DEPTH // 2
POOL_WINDOWS = (2, 4, 8, 16)
POOL_GROUP = D_MODEL // len(POOL_WINDOWS)
POOL_HALO = 16
N_HEADS = 16
HEAD_DIM = 64
N_KV_GROUPS = 4
HEADS_PER_GROUP = N_HEADS // N_KV_GROUPS
CMP_STRIDE = 16
CMP_LEN = 2 * CMP_STRIDE
CMP_HIDDEN = 128
SLC_BLOCK = 64
N_SELECT = 16
WINDOW = 512
N_BRANCH = 3
ROPE_THETA = 500000.0
ROT_DIM = HEAD_DIM // 4
D_FF = 2816
CONV_W = 3
PLE_DIM = 256
EPS = 1e-6
NEG = -1e30
FORCE_SCORE = 1e4

LANES = 128
SUBLANES = 8
VMEM_LIMIT = 48 * 1024 * 1024

TS = 512
FC = 256
TQ = 256
KC = 512
GATE_ROWS = 16

BF16 = jnp.bfloat16
F32 = jnp.float32


def _dot(a, b):
    return jnp.dot(a, b, preferred_element_type=F32)


def _rmsnorm(x, g):
    return x * lax.rsqrt(jnp.mean(x * x, axis=-1, keepdims=True) + EPS) * g


def _sigmoid(x):
    return 1.0 / (1.0 + jnp.exp(-x))


def _rope128(x, c, a, b):
    return x * c + pltpu.roll(x, LANES - ROT_DIM // 2, 1) * a + pltpu.roll(x, ROT_DIM // 2, 1) * b


def _params(n_grid):
    return pltpu.CompilerParams(dimension_semantics=("arbitrary",) * n_grid,
                                vmem_limit_bytes=VMEM_LIMIT)


def _pool_kernel(x_ref, halo_ref, g_ref, w_ref, scale_ref, o_ref):
    s = pl.program_id(1)
    x = x_ref[0]
    g = g_ref[...]
    h = _rmsnorm(x, g)
    hh = jnp.where(s > 0, _rmsnorm(halo_ref[0], g), 0.0)
    ext = jnp.concatenate([hh, h], axis=0)
    sums = {1: ext}
    w = 1
    while w < POOL_WINDOWS[-1]:
        prev = sums[w]
        sums[2 * w] = prev + pltpu.roll(prev, w, 0)
        w *= 2
    t = s * TS + lax.broadcasted_iota(jnp.int32, (TS, 1), 0)
    outs = []
    for gi, win in enumerate(POOL_WINDOWS):
        lo, hi = gi * POOL_GROUP, (gi + 1) * POOL_GROUP
        tot = sums[win][POOL_HALO:, lo:hi]
        cnt = jnp.minimum(t + 1, win).astype(F32)
        u = tot / cnt - h[:, lo:hi]
        outs.append(_dot(u.astype(BF16), w_ref[gi]))
    o_ref[0] = x + jnp.concatenate(outs, axis=-1) * scale_ref[...]


def _pool_layer(x, i, g, w, scale):
    B, S, D = x.shape
    halo_blocks = TS // POOL_HALO
    layer3 = lambda b, s: (i, 0, 0)
    return pl.pallas_call(
        _pool_kernel,
        grid=(B, S // TS),
        in_specs=[
            pl.BlockSpec((1, TS, D), lambda b, s: (b, s, 0)),
            pl.BlockSpec((1, POOL_HALO, D), lambda b, s: (b, jnp.maximum(s * halo_blocks - 1, 0), 0)),
            pl.BlockSpec((None, 1, D), layer3),
            pl.BlockSpec((None, len(POOL_WINDOWS), POOL_GROUP, POOL_GROUP), lambda b, s: (i, 0, 0, 0)),
            pl.BlockSpec((None, 1, D), layer3),
        ],
        out_specs=pl.BlockSpec((1, TS, D), lambda b, s: (b, s, 0)),
        out_shape=jax.ShapeDtypeStruct(x.shape, F32),
        compiler_params=_params(2),
        name="pool_mixer",
    )(x, x, g, w, scale)


def _ffn_kernel(*refs, has_attn, final_norm):
    it = iter(refs)
    x_ref = next(it)
    if has_attn:
        attn_ref, wout_ref = next(it), next(it)
    gffn_ref = next(it)
    up_refs = [(next(it), next(it)) for _ in range(2)]
    mix_refs = [tuple(next(it) for _ in range(5)) for _ in range(2)]
    gple_ref, pgate_ref, p_ref, pproj_ref = (next(it) for _ in range(4))
    if final_norm:
        gfin_ref = next(it)
    o_ref = next(it)
    xres_ref, h_ref, acc_ref, carry_ref, u0_ref, u1_ref = (next(it) for _ in range(6))

    s = pl.program_id(1)
    k = pl.program_id(2)
    n_k = pl.num_programs(2)

    @pl.when(k == 0)
    def _():
        xin = x_ref[...]
        if has_attn:
            xin = xin + _dot(attn_ref[...], wout_ref[...])
        xres_ref[...] = xin
        h_ref[...] = _rmsnorm(xin, gffn_ref[...]).astype(BF16)
        acc_ref[...] = jnp.zeros_like(acc_ref)
        u1_ref[...] = jnp.zeros_like(u1_ref)

    @pl.when((s == 0) & (k == 0))
    def _():
        carry_ref[...] = jnp.zeros_like(carry_ref)

    h = h_ref[...]

    def half_step(u_in_ref, u_out_ref, chunk, valid, up, mix):
        wa_ref, wg_ref = up
        cwa_ref, cwg_ref, cba_ref, cbg_ref, wdown_ref = mix
        u_out_ref[0, SUBLANES:, :] = _dot(h, wa_ref[...])
        u_out_ref[1, SUBLANES:, :] = _dot(h, wg_ref[...])

        def conv_half(cw_ref, cb_ref, slot):
            u = u_in_ref[slot, SUBLANES:, :]
            last = carry_ref[chunk, slot]
            u_in_ref[slot, 0:SUBLANES, :] = last
            carry_ref[chunk, slot] = u[TS - SUBLANES:, :] if valid is None else jnp.where(
                valid, u[TS - SUBLANES:, :], last)
            cw = cw_ref[...]
            out = cw[CONV_W - 1:CONV_W] * u + cb_ref[...]
            for back in range(1, CONV_W):
                shifted = u_in_ref[slot, pl.ds(SUBLANES - back, TS), :]
                out = out + cw[CONV_W - 1 - back:CONV_W - back] * shifted
            return out

        a = conv_half(cwa_ref, cba_ref, 0)
        g = conv_half(cwg_ref, cbg_ref, 1)
        act = g * _sigmoid(g) * a
        if valid is not None:
            act = jnp.where(valid, act, 0.0)
        acc_ref[...] += _dot(act.astype(BF16), wdown_ref[...])

    half_step(u1_ref, u0_ref, jnp.maximum(2 * k - 1, 0), k > 0, up_refs[0], mix_refs[0])
    half_step(u0_ref, u1_ref, 2 * k, None, up_refs[1], mix_refs[1])

    @pl.when(k == n_k - 1)
    def _():
        x2 = xres_ref[...] + acc_ref[...]
        hp = _rmsnorm(x2, gple_ref[...]).astype(BF16)
        gate = _sigmoid(_dot(hp, pgate_ref[...]))
        emb = _dot(p_ref[...].astype(BF16), pproj_ref[...])
        x3 = x2 + gate * emb
        if final_norm:
            x3 = _rmsnorm(x3, gfin_ref[...])
        o_ref[...] = x3


def _ffn_layer(x, i, attn, w_out, j, g_ffn, w_up, conv_w, conv_b, w_down, g_ple, ple_gate, p, ple_proj, g_final):
    B, S, D = x.shape
    n_c = D_FF // FC
    assert n_c % 2 == 1
    n_k = (n_c + 1) // 2
    has_attn = attn is not None
    final_norm = g_final is not None
    tok = lambda b, s, k: (b, s, 0)
    layer = lambda b, s, k: (i, 0, 0)
    vec = pl.BlockSpec((None, 1, D), layer)
    args, specs = [x], [pl.BlockSpec((None, TS, D), tok)]
    if has_attn:
        args += [attn, w_out]
        specs += [pl.BlockSpec((None, TS, D), tok), pl.BlockSpec((None, D, D), lambda b, s, k: (j, 0, 0))]
    args.append(g_ffn)
    specs.append(vec)
    for chunk in (lambda k: 2 * k, lambda k: jnp.minimum(2 * k + 1, n_c - 1)):
        args += [w_up, w_up]
        specs += [pl.BlockSpec((None, D, FC), lambda b, s, k, f=chunk: (i, 0, f(k))),
                  pl.BlockSpec((None, D, FC), lambda b, s, k, f=chunk: (i, 0, n_c + f(k)))]
    for chunk in (lambda k: jnp.maximum(2 * k - 1, 0), lambda k: 2 * k):
        args += [conv_w, conv_w, conv_b, conv_b, w_down]
        specs += [pl.BlockSpec((None, CONV_W, FC), lambda b, s, k, f=chunk: (i, 0, f(k))),
                  pl.BlockSpec((None, CONV_W, FC), lambda b, s, k, f=chunk: (i, 0, n_c + f(k))),
                  pl.BlockSpec((None, 1, FC), lambda b, s, k, f=chunk: (i, 0, f(k))),
                  pl.BlockSpec((None, 1, FC), lambda b, s, k, f=chunk: (i, 0, n_c + f(k))),
                  pl.BlockSpec((None, FC, D), lambda b, s, k, f=chunk: (i, f(k), 0))]
    args += [g_ple, ple_gate, p, ple_proj]
    specs += [
        vec,
        pl.BlockSpec((None, D, D), layer),
        pl.BlockSpec((None, None, TS, PLE_DIM), lambda b, s, k: (i, b, s, 0)),
        pl.BlockSpec((None, PLE_DIM, D), layer),
    ]
    if final_norm:
        args.append(g_final.reshape(1, D))
        specs.append(pl.BlockSpec((1, D), lambda b, s, k: (0, 0)))
    return pl.pallas_call(
        functools.partial(_ffn_kernel, has_attn=has_attn, final_norm=final_norm),
        grid=(B, S // TS, n_k),
        in_specs=specs,
        out_specs=pl.BlockSpec((None, TS, D), tok),
        out_shape=jax.ShapeDtypeStruct(x.shape, F32),
        scratch_shapes=[
            pltpu.VMEM((TS, D), F32),
            pltpu.VMEM((TS, D), BF16),
            pltpu.VMEM((TS, D), F32),
            pltpu.VMEM((n_c, 2, SUBLANES, FC), F32),
            pltpu.VMEM((2, SUBLANES + TS, FC), F32),
            pltpu.VMEM((2, SUBLANES + TS, FC), F32),
        ],
        compiler_params=_params(3),
        name="ffn_embed",
    )(*args)


def _kv_kernel(x_ref, g_ref, w_ref, rc_ref, ra_ref, rb_ref,
               kc_ref, vc_ref, ks_ref, vts_ref, kw_ref, vtw_ref):
    s = pl.program_id(1)
    lane = lax.broadcasted_iota(jnp.int32, (TS, LANES), 1)

    @pl.when(s == 0)
    def _():
        flag = jnp.where(lane == HEAD_DIM, 1.0, 0.0).astype(BF16)
        for g in range(N_KV_GROUPS):
            kw_ref[0, g] = flag
            vtw_ref[0, g] = jnp.zeros((LANES, TS), BF16)

    @pl.when(s > 0)
    def _():
        h = _rmsnorm(x_ref[0], g_ref[...]).astype(BF16)
        kv = _dot(h, w_ref[...])
        gw = N_KV_GROUPS * HEAD_DIM
        rc, ra, rb = rc_ref[0], ra_ref[0], rb_ref[0]
        tok = (s - 1) * TS + lax.broadcasted_iota(jnp.int32, (TS, LANES), 0)
        onehot = jnp.where(lane - HEAD_DIM == tok // SLC_BLOCK, 1.0, 0.0)
        row = lax.broadcasted_iota(jnp.int32, (LANES - HEAD_DIM, TS), 0)
        ones_row = jnp.where(row == 0, 1.0, 0.0).astype(BF16)

        def slabs(i):
            return [kv[:, i * gw + j * LANES:i * gw + (j + 1) * LANES] for j in range(gw // LANES)]

        for g in range(N_KV_GROUPS):
            kc_ref[0, g] = kv[:, 0 * gw + g * HEAD_DIM:0 * gw + (g + 1) * HEAD_DIM]
            vc_ref[0, g] = kv[:, 1 * gw + g * HEAD_DIM:1 * gw + (g + 1) * HEAD_DIM]

        for i, k_ref, vt_ref in ((2, ks_ref, vts_ref), (4, kw_ref, vtw_ref)):
            for j, (kslab, vslab) in enumerate(zip(slabs(i), slabs(i + 1))):
                kr = _rope128(kslab, rc, ra, rb)
                vt = vslab.T
                for half in range(2):
                    g = 2 * j + half
                    kh = kr if half == 0 else pltpu.roll(kr, HEAD_DIM, 1)
                    upper = onehot if k_ref is ks_ref else 0.0
                    k_ref[0, g] = jnp.where(lane < HEAD_DIM, kh, upper).astype(BF16)
                    vt_ref[0, g, 0:HEAD_DIM, :] = vt[half * HEAD_DIM:(half + 1) * HEAD_DIM, :].astype(BF16)
                    vt_ref[0, g, HEAD_DIM:, :] = ones_row


def _kv_project(x, g, w_kv, rc, ra, rb):
    B, S, D = x.shape
    G = N_KV_GROUPS
    assert WINDOW == TS
    tile = lambda s: jnp.maximum(s - 1, 0)
    tok = lambda b, s: (b, tile(s), 0)
    rows = lambda b, s: (b, 0, tile(s), 0)
    cols = lambda b, s: (b, 0, 0, tile(s))
    return pl.pallas_call(
        _kv_kernel,
        grid=(B, S // TS + 1),
        in_specs=[
            pl.BlockSpec((1, TS, D), tok),
            pl.BlockSpec((1, D), lambda b, s: (0, 0)),
            pl.BlockSpec(w_kv.shape, lambda b, s: (0, 0)),
            pl.BlockSpec((1, TS, LANES), tok),
            pl.BlockSpec((1, TS, LANES), tok),
            pl.BlockSpec((1, TS, LANES), tok),
        ],
        out_specs=[
            pl.BlockSpec((1, G, TS, HEAD_DIM), rows),
            pl.BlockSpec((1, G, TS, HEAD_DIM), rows),
            pl.BlockSpec((1, G, TS, LANES), rows),
            pl.BlockSpec((1, G, LANES, TS), cols),
            pl.BlockSpec((1, G, TS, LANES), lambda b, s: (b, 0, s, 0)),
            pl.BlockSpec((1, G, LANES, TS), lambda b, s: (b, 0, 0, s)),
        ],
        out_shape=[
            jax.ShapeDtypeStruct((B, G, S, HEAD_DIM), F32),
            jax.ShapeDtypeStruct((B, G, S, HEAD_DIM), F32),
            jax.ShapeDtypeStruct((B, G, S, LANES), BF16),
            jax.ShapeDtypeStruct((B, G, LANES, S), BF16),
            jax.ShapeDtypeStruct((B, G, WINDOW + S, LANES), BF16),
            jax.ShapeDtypeStruct((B, G, LANES, WINDOW + S), BF16),
        ],
        compiler_params=_params(2),
        name="kv_project",
    )(x, g.reshape(1, D), w_kv.astype(BF16), rc, ra, rb)


def _gelu_tanh(x):
    return 0.5 * x * (1.0 + jnp.tanh(math.sqrt(2.0 / math.pi) * (x + 0.044715 * (x * x * x))))


def _compress_kernel(kf_ref, vf_ref, pk_ref, w1k_ref, w2k_ref, pv_ref, w1v_ref, w2v_ref,
                     rc_ref, ra_ref, rb_ref, kc_ref, vct_ref):
    half = CMP_STRIDE * HEAD_DIM

    def hidden(f_ref, pos_ref, w1_ref):
        flat = f_ref[0, 0]
        pos = pos_ref[...]
        first = _dot((flat + pos[0:1]).astype(BF16), w1_ref[0:half, :])
        second = _dot((flat + pos[1:2]).astype(BF16), w1_ref[half:, :])
        n = flat.shape[0]
        hid = first + pltpu.roll(second, n - 1, 0)
        return _gelu_tanh(hid).astype(BF16)

    kc = _dot(hidden(kf_ref, pk_ref, w1k_ref), w2k_ref[...])
    kc_ref[0, 0] = _rope128(kc, rc_ref[0], ra_ref[0], rb_ref[0])[:, :HEAD_DIM].astype(BF16)
    vct = lax.dot_general(w2v_ref[...], hidden(vf_ref, pv_ref, w1v_ref), (((1,), (1,)), ((), ())),
                          preferred_element_type=F32)
    vct_ref[0, 0] = vct[:HEAD_DIM, :].astype(BF16)


def _compress(kc_flat, vc_flat, pos_k, w1_k, w2_k, pos_v, w1_v, w2_v, rc, ra, rb):
    B, G, n_chunk, width = kc_flat.shape
    blk = lambda b, g: (b, g, 0, 0)
    const2 = lambda b, g: (0, 0)
    pad2 = lambda w: jnp.pad(w, ((0, 0), (0, LANES - HEAD_DIM))).astype(BF16)
    return pl.pallas_call(
        _compress_kernel,
        grid=(B, G),
        in_specs=[
            pl.BlockSpec((1, 1, n_chunk, width), blk),
            pl.BlockSpec((1, 1, n_chunk, width), blk),
            pl.BlockSpec((2, width), const2),
            pl.BlockSpec((2 * width, CMP_HIDDEN), const2),
            pl.BlockSpec((CMP_HIDDEN, LANES), const2),
            pl.BlockSpec((2, width), const2),
            pl.BlockSpec((2 * width, CMP_HIDDEN), const2),
            pl.BlockSpec((CMP_HIDDEN, LANES), const2),
            pl.BlockSpec((1, n_chunk, LANES), lambda b, g: (b, 0, 0)),
            pl.BlockSpec((1, n_chunk, LANES), lambda b, g: (b, 0, 0)),
            pl.BlockSpec((1, n_chunk, LANES), lambda b, g: (b, 0, 0)),
        ],
        out_specs=[
            pl.BlockSpec((1, 1, n_chunk, HEAD_DIM), blk),
            pl.BlockSpec((1, 1, HEAD_DIM, n_chunk), blk),
        ],
        out_shape=[
            jax.ShapeDtypeStruct((B, G, n_chunk, HEAD_DIM), BF16),
            jax.ShapeDtypeStruct((B, G, HEAD_DIM, n_chunk), BF16),
        ],
        compiler_params=_params(2),
        name="compress",
    )(kc_flat, vc_flat, pos_k.reshape(2, width), w1_k.astype(BF16), pad2(w2_k),
      pos_v.reshape(2, width), w1_v.astype(BF16), pad2(w2_v).T, rc, ra, rb)


def _qproj_kernel(x_ref, g_ref, w_ref, rc_ref, ra_ref, rb_ref, q_ref, gate_ref):
    h = _rmsnorm(x_ref[0], g_ref[...]).astype(BF16)
    proj = _dot(h, w_ref[...])
    rc, ra, rb = rc_ref[0], ra_ref[0], rb_ref[0]
    scale = HEAD_DIM ** -0.5
    for j in range(N_HEADS // 2):
        qt = (_rope128(proj[:, j * LANES:(j + 1) * LANES], rc, ra, rb) * scale).T
        q_ref[0, 2 * j] = qt[:HEAD_DIM].astype(BF16)
        q_ref[0, 2 * j + 1] = qt[HEAD_DIM:].astype(BF16)
    n_q = N_HEADS * HEAD_DIM
    for g in range(N_KV_GROUPS):
        gt = _sigmoid(proj[:, n_q + g * LANES:n_q + (g + 1) * LANES]).T
        gate_ref[0, g] = gt[:gate_ref.shape[2]]


def _q_weights(w_in_b):
    L, D, _ = w_in_b.shape
    n_q = N_HEADS * HEAD_DIM
    per_group = HEADS_PER_GROUP * N_BRANCH
    gate_w = w_in_b[:, :, n_q:].reshape(L, D, N_KV_GROUPS, per_group)
    gate_w = jnp.pad(gate_w, ((0, 0), (0, 0), (0, 0), (0, LANES - per_group))).reshape(L, D, N_KV_GROUPS * LANES)
    return jnp.concatenate([w_in_b[:, :, :n_q], gate_w], axis=2).astype(BF16)


def _q_project(x, i, g, j, w, rc, ra, rb):
    B, S, D = x.shape
    tok = lambda b, s: (b, s, 0)
    return pl.pallas_call(
        _qproj_kernel,
        grid=(B, S // TS),
        in_specs=[
            pl.BlockSpec((1, TS, D), tok),
            pl.BlockSpec((None, 1, D), lambda b, s: (i, 0, 0)),
            pl.BlockSpec((None,) + w.shape[1:], lambda b, s: (j, 0, 0)),
            pl.BlockSpec((1, TS, LANES), tok),
            pl.BlockSpec((1, TS, LANES), tok),
            pl.BlockSpec((1, TS, LANES), tok),
        ],
        out_specs=[
            pl.BlockSpec((1, N_HEADS, HEAD_DIM, TS), lambda b, s: (b, 0, 0, s)),
            pl.BlockSpec((1, N_KV_GROUPS, GATE_ROWS, TS), lambda b, s: (b, 0, 0, s)),
        ],
        out_shape=[
            jax.ShapeDtypeStruct((B, N_HEADS, HEAD_DIM, S), BF16),
            jax.ShapeDtypeStruct((B, N_KV_GROUPS, GATE_ROWS, S), F32),
        ],
        compiler_params=_params(2),
        name="q_project",
    )(x, g, w, rc, ra, rb)


def _attn_kernel(q_ref, gate_ref, kc_ref, vct_ref, ovl_ref, ks_ref, vts_ref, kw_ref, vtw_ref,
                 o_ref, m_ref, acc_ref):
    qi = pl.program_id(2)
    s0 = qi * TQ
    hpg = HEADS_PER_GROUP
    n_slc = ovl_ref.shape[0]

    qt = jnp.concatenate([q_ref[0, hh] for hh in range(hpg)], axis=1)

    def t_of(shape):
        return s0 + (lax.broadcasted_iota(jnp.int32, shape, 1) & (TQ - 1))

    sc = _dot(kc_ref[0, 0], qt)
    c_end = CMP_STRIDE * lax.broadcasted_iota(jnp.int32, sc.shape, 0) + CMP_LEN - 1
    mask_c = c_end <= t_of(sc.shape)
    sc = jnp.where(mask_c, sc, NEG)
    e = jnp.where(mask_c, jnp.exp(sc - jnp.max(sc, axis=0, keepdims=True)), 0.0)
    p_c = e * (1.0 / jnp.maximum(jnp.sum(e, axis=0, keepdims=True), 1e-30))
    oc_t = _dot(vct_ref[0, 0], p_c.astype(BF16))
    p_sum = p_c[:, 0:TQ]
    for hh in range(1, hpg):
        p_sum = p_sum + p_c[:, hh * TQ:(hh + 1) * TQ]
    imp_t = jnp.dot(ovl_ref[...], p_sum, precision=lax.Precision.HIGHEST,
                    preferred_element_type=F32)

    blk = lax.broadcasted_iota(jnp.int32, (n_slc, TQ), 0)
    t_col = s0 + lax.broadcasted_iota(jnp.int32, (n_slc, TQ), 1)
    cur = t_col // SLC_BLOCK
    forced = (blk == 0) | (blk == cur) | (blk == cur - 1)
    valid = blk * SLC_BLOCK <= t_col
    score = jnp.where(valid, jnp.where(forced, FORCE_SCORE, imp_t), NEG)
    groups = [score[r:r + SUBLANES] for r in range(0, n_slc, SUBLANES)]
    blk_in = lax.broadcasted_iota(jnp.int32, (SUBLANES, TQ), 0)
    ranks = [jnp.zeros((SUBLANES, TQ), F32) for _ in groups]
    for m in range(n_slc):
        other = score[m:m + 1, :]
        for gi, grp in enumerate(groups):
            lo = gi * SUBLANES
            if lo > m:
                ahead = other >= grp
            elif lo + SUBLANES - 1 <= m:
                ahead = other > grp
            else:
                ahead = (other > grp) | ((other == grp) & (blk_in + lo > m))
            ranks[gi] = ranks[gi] + jnp.where(ahead, 1.0, 0.0)
    rank = jnp.concatenate(ranks, axis=0)
    bias_t = jnp.where(valid & (rank < min(N_SELECT, n_slc)), 0.0, NEG).astype(BF16)
    qt_sel = jnp.concatenate([qt, jnp.concatenate([bias_t] * hpg, axis=1)], axis=0)

    tri = lax.broadcasted_iota(jnp.int32, (TQ, hpg * TQ), 0) <= (
        lax.broadcasted_iota(jnp.int32, (TQ, hpg * TQ), 1) & (TQ - 1))

    m_ref[...] = jnp.full_like(m_ref, NEG)
    acc_ref[...] = jnp.zeros_like(acc_ref)

    def sel_step(k0, size, mask=None):
        s = _dot(ks_ref[0, 0, pl.ds(k0, size), :], qt_sel)
        if mask is not None:
            s = jnp.where(mask, s, NEG)
        m_old = m_ref[...]
        m_new = jnp.maximum(m_old, jnp.max(s, axis=0, keepdims=True))
        p = jnp.exp(s - m_new).astype(BF16)
        acc_ref[...] = jnp.exp(m_old - m_new) * acc_ref[...] + _dot(vts_ref[0, 0, :, pl.ds(k0, size)], p)
        m_ref[...] = m_new

    def sel_body(j, carry):
        sel_step(pl.multiple_of(j * KC, KC), KC)
        return carry

    lax.fori_loop(0, s0 // KC, sel_body, 0)
    for r in range(KC // TQ - 1, 0, -1):
        @pl.when((qi % (KC // TQ)) >= r)
        def _():
            sel_step(pl.multiple_of(s0 - r * TQ, TQ), TQ)
    sel_step(pl.multiple_of(s0, TQ), TQ, tri)
    acc = acc_ref[...]
    os_t = acc[:HEAD_DIM] * (1.0 / acc[HEAD_DIM:HEAD_DIM + 1])

    span = WINDOW + TQ
    w0 = pl.multiple_of(s0, TQ)
    flag_row = lax.broadcasted_iota(jnp.int32, (LANES - HEAD_DIM, hpg * TQ), 0) == 0
    qt_win = jnp.concatenate([qt, jnp.where(flag_row, NEG, 0.0).astype(BF16)], axis=0)
    sw = _dot(kw_ref[0, 0, pl.ds(w0, span), :], qt_win)
    sw = jnp.concatenate([jnp.where(tri, NEG, sw[:TQ]), sw[TQ:WINDOW], jnp.where(tri, sw[WINDOW:], NEG)], axis=0)
    ew = jnp.exp(sw - jnp.max(sw, axis=0, keepdims=True)).astype(BF16)
    rw = _dot(vtw_ref[0, 0, :, pl.ds(w0, span)], ew)
    ow_t = rw[:HEAD_DIM] * (1.0 / jnp.maximum(rw[HEAD_DIM:HEAD_DIM + 1], 1e-30))

    gates = gate_ref[0, 0]
    outs = []
    for hh in range(hpg):
        c = slice(hh * TQ, (hh + 1) * TQ)
        r = hh * N_BRANCH
        outs.append(gates[r:r + 1] * oc_t[:, c] + gates[r + 1:r + 2] * os_t[:, c]
                    + gates[r + 2:r + 3] * ow_t[:, c])
    o_ref[0] = jnp.concatenate(outs, axis=0).T.astype(o_ref.dtype)


def _attention(q_t, gates_t, kc, vct, ovl_t, ks, vts, kw, vtw):
    B, _, _, S = q_t.shape
    G = N_KV_GROUPS
    hpg = HEADS_PER_GROUP
    per_bg = lambda b, g, qi: (b, g, 0, 0)
    return pl.pallas_call(
        _attn_kernel,
        grid=(B, G, S // TQ),
        in_specs=[
            pl.BlockSpec((1, hpg, HEAD_DIM, TQ), lambda b, g, qi: (b, g, 0, qi)),
            pl.BlockSpec((1, 1, GATE_ROWS, TQ), lambda b, g, qi: (b, g, 0, qi)),
            pl.BlockSpec((1, 1) + kc.shape[2:], per_bg),
            pl.BlockSpec((1, 1) + vct.shape[2:], per_bg),
            pl.BlockSpec(ovl_t.shape, lambda b, g, qi: (0, 0)),
            pl.BlockSpec((1, 1) + ks.shape[2:], per_bg),
            pl.BlockSpec((1, 1) + vts.shape[2:], per_bg),
            pl.BlockSpec((1, 1) + kw.shape[2:], per_bg),
            pl.BlockSpec((1, 1) + vtw.shape[2:], per_bg),
        ],
        out_specs=pl.BlockSpec((1, TQ, hpg * HEAD_DIM), lambda b, g, qi: (b, qi, g)),
        out_shape=jax.ShapeDtypeStruct((B, S, N_HEADS * HEAD_DIM), BF16),
        scratch_shapes=[
            pltpu.VMEM((1, hpg * TQ), F32),
            pltpu.VMEM((LANES, hpg * TQ), F32),
        ],
        compiler_params=_params(3),
        name="sparse_attention",
    )(q_t, gates_t, kc, vct, ovl_t, ks, vts, kw, vtw)


def _rope_tables(pos):
    inv_freq = ROPE_THETA ** (-jnp.arange(0, ROT_DIM, 2, dtype=F32) / ROT_DIM)
    ang = pos.astype(F32)[..., None] * inv_freq
    cos, sin = jnp.cos(ang), jnp.sin(ang)
    one = jnp.ones(cos.shape[:-1] + (HEAD_DIM - ROT_DIM,), F32)
    zero = jnp.zeros(cos.shape[:-1] + (HEAD_DIM - ROT_DIM // 2,), F32)
    c = jnp.concatenate([cos, cos, one], axis=-1)
    a = jnp.concatenate([-sin, zero], axis=-1)
    b = jnp.concatenate([jnp.zeros_like(sin), sin, zero[..., :HEAD_DIM - ROT_DIM]], axis=-1)
    tile = lambda v: jnp.concatenate([v, v], axis=-1)
    return tile(c), tile(a), tile(b)


def _overlap_t(n_chunk, n_slc):
    c_start = CMP_STRIDE * np.arange(n_chunk)
    blk = np.arange(n_slc)
    ovl = ((c_start[None] < (blk[:, None] + 1) * SLC_BLOCK)
           & (c_start[None] + CMP_LEN > blk[:, None] * SLC_BLOCK)
           & (np.arange(n_chunk)[None] < n_chunk - 1))
    return jnp.asarray(ovl.astype(np.float32))


def kernel(x, p, positions, norm_mix, pool_w, pool_scale, norm_kv, w_kv, cmp_pos_k, cmp_w1_k, cmp_w2_k, cmp_pos_v, cmp_w1_v, cmp_w2_v, w_in_b, w_out_b, norm_ffn, ffn_up, ffn_conv, ffn_conv_b, ffn_down, norm_ple, ple_gate, ple_proj, norm_final):
    B, S, D = x.shape
    n_chunk = S // CMP_STRIDE
    assert S // SLC_BLOCK == LANES - HEAD_DIM
    rc, ra, rb = _rope_tables(positions)
    end_idx = jnp.minimum(CMP_STRIDE * jnp.arange(n_chunk) + CMP_LEN - 1, S - 1)
    rc_c, ra_c, rb_c = _rope_tables(positions[:, end_idx])
    ovl_t = _overlap_t(n_chunk, S // SLC_BLOCK)

    stack = lambda g: g.reshape(g.shape[0], 1, g.shape[1])
    norm_mix3, norm_ffn3, norm_ple3, pool_scale3 = map(stack, (norm_mix, norm_ffn, norm_ple, pool_scale))
    conv_b3 = stack(ffn_conv_b)
    pool_w16, up16, down16, gate16, proj16, w_out16 = (
        w.astype(BF16) for w in (pool_w, ffn_up, ffn_down, ple_gate, ple_proj, w_out_b))
    w_q16 = _q_weights(w_in_b)

    kv = None
    for i in range(DEPTH):
        attn = None
        j = i - N_A
        if i < N_A:
            x = _pool_layer(x, i, norm_mix3, pool_w16, pool_scale3)
        else:
            if i == N_A:
                kc_raw, vc_raw, ks, vts, kw, vtw = _kv_project(x, norm_kv, w_kv, rc, ra, rb)
                flat = lambda z: z.reshape(B, N_KV_GROUPS, n_chunk, CMP_STRIDE * HEAD_DIM)
                kc, vct = _compress(flat(kc_raw), flat(vc_raw), cmp_pos_k, cmp_w1_k, cmp_w2_k,
                                    cmp_pos_v, cmp_w1_v, cmp_w2_v, rc_c, ra_c, rb_c)
                kv = (kc, vct, ovl_t, ks, vts, kw, vtw)
            q_t, gates_t = _q_project(x, i, norm_mix3, j, w_q16, rc, ra, rb)
            attn = _attention(q_t, gates_t, *kv)
        x = _ffn_layer(x, i, attn, w_out16, j, norm_ffn3, up16, ffn_conv, conv_b3, down16,
                       norm_ple3, gate16, p, proj16, norm_final if i == DEPTH - 1 else None)
    return x
```

```python
import functools
import math

import jax
import jax.numpy as jnp
import numpy as np
from jax import lax
from jax.experimental import pallas as pl
from jax.experimental.pallas import tpu as pltpu

D_MODEL = 1024
DEPTH = 4
N_A = DEPTH // 2
POOL_WINDOWS = (2, 4, 8, 16)
POOL_GROUP = D_MODEL // len(POOL_WINDOWS)
POOL_HALO = 16
N_HEADS = 16
HEAD_DIM = 64
N_KV_GROUPS = 4
HEADS_PER_GROUP = N_HEADS // N_KV_GROUPS
CMP_STRIDE = 16
CMP_LEN = 2 * CMP_STRIDE
CMP_HIDDEN = 128
SLC_BLOCK = 64
N_SELECT = 16
WINDOW = 512
N_BRANCH = 3
ROPE_THETA = 500000.0
ROT_DIM = HEAD_DIM // 4
D_FF = 2816
CONV_W = 3
PLE_DIM = 256
EPS = 1e-6
NEG = -1e30
FORCE_SCORE = 1e4

LANES = 128
SUBLANES = 8
VMEM_LIMIT = 48 * 1024 * 1024

TS = 512
FC = 256
TQ = 256
KC = TQ
GATE_ROWS = 16

BF16 = jnp.bfloat16
F32 = jnp.float32


def _dot(a, b):
    return jnp.dot(a, b, preferred_element_type=F32)


def _rmsnorm(x, g):
    return x * lax.rsqrt(jnp.mean(x * x, axis=-1, keepdims=True) + EPS) * g


def _sigmoid(x):
    return 1.0 / (1.0 + jnp.exp(-x))


def _rope128(x, c, a, b):
    return x * c + pltpu.roll(x, LANES - ROT_DIM // 2, 1) * a + pltpu.roll(x, ROT_DIM // 2, 1) * b


def _params(n_grid):
    return pltpu.CompilerParams(dimension_semantics=("arbitrary",) * n_grid,
                                vmem_limit_bytes=VMEM_LIMIT)


def _pool_kernel(x_ref, halo_ref, g_ref, w_ref, scale_ref, o_ref):
    s = pl.program_id(1)
    x = x_ref[0]
    g = g_ref[...]
    h = _rmsnorm(x, g)
    hh = jnp.where(s > 0, _rmsnorm(halo_ref[0], g), 0.0)
    ext = jnp.concatenate([hh, h], axis=0)
    sums = {1: ext}
    w = 1
    while w < POOL_WINDOWS[-1]:
        prev = sums[w]
        sums[2 * w] = prev + pltpu.roll(prev, w, 0)
        w *= 2
    t = s * TS + lax.broadcasted_iota(jnp.int32, (TS, 1), 0)
    outs = []
    for gi, win in enumerate(POOL_WINDOWS):
        lo, hi = gi * POOL_GROUP, (gi + 1) * POOL_GROUP
        tot = sums[win][POOL_HALO:, lo:hi]
        cnt = jnp.minimum(t + 1, win).astype(F32)
        u = tot / cnt - h[:, lo:hi]
        outs.append(_dot(u.astype(BF16), w_ref[gi]))
    o_ref[0] = x + jnp.concatenate(outs, axis=-1) * scale_ref[...]


def _pool_layer(x, i, g, w, scale):
    B, S, D = x.shape
    halo_blocks = TS // POOL_HALO
    layer3 = lambda b, s: (i, 0, 0)
    return pl.pallas_call(
        _pool_kernel,
        grid=(B, S // TS),
        in_specs=[
            pl.BlockSpec((1, TS, D), lambda b, s: (b, s, 0)),
            pl.BlockSpec((1, POOL_HALO, D), lambda b, s: (b, jnp.maximum(s * halo_blocks - 1, 0), 0)),
            pl.BlockSpec((None, 1, D), layer3),
            pl.BlockSpec((None, len(POOL_WINDOWS), POOL_GROUP, POOL_GROUP), lambda b, s: (i, 0, 0, 0)),
            pl.BlockSpec((None, 1, D), layer3),
        ],
        out_specs=pl.BlockSpec((1, TS, D), lambda b, s: (b, s, 0)),
        out_shape=jax.ShapeDtypeStruct(x.shape, F32),
        compiler_params=_params(2),
        name="pool_mixer",
    )(x, x, g, w, scale)


def _ffn_kernel(*refs, has_attn, final_norm):
    it = iter(refs)
    x_ref = next(it)
    if has_attn:
        attn_ref, wout_ref = next(it), next(it)
    gffn_ref = next(it)
    up_refs = [(next(it), next(it)) for _ in range(2)]
    mix_refs = [tuple(next(it) for _ in range(5)) for _ in range(2)]
    gple_ref, pgate_ref, p_ref, pproj_ref = (next(it) for _ in range(4))
    if final_norm:
        gfin_ref = next(it)
    o_ref = next(it)
    xres_ref, h_ref, acc_ref, carry_ref, u0_ref, u1_ref = (next(it) for _ in range(6))

    s = pl.program_id(1)
    k = pl.program_id(2)
    n_k = pl.num_programs(2)

    @pl.when(k == 0)
    def _():
        xin = x_ref[...]
        if has_attn:
            xin = xin + _dot(attn_ref[...], wout_ref[...])
        xres_ref[...] = xin
        h_ref[...] = _rmsnorm(xin, gffn_ref[...]).astype(BF16)
        acc_ref[...] = jnp.zeros_like(acc_ref)
        u1_ref[...] = jnp.zeros_like(u1_ref)

    @pl.when((s == 0) & (k == 0))
    def _():
        carry_ref[...] = jnp.zeros_like(carry_ref)

    h = h_ref[...]

    def half_step(u_in_ref, u_out_ref, chunk, valid, up, mix):
        wa_ref, wg_ref = up
        cwa_ref, cwg_ref, cba_ref, cbg_ref, wdown_ref = mix
        u_out_ref[0, SUBLANES:, :] = _dot(h, wa_ref[...])
        u_out_ref[1, SUBLANES:, :] = _dot(h, wg_ref[...])

        def conv_half(cw_ref, cb_ref, slot):
            u = u_in_ref[slot, SUBLANES:, :]
            last = carry_ref[chunk, slot]
            u_in_ref[slot, 0:SUBLANES, :] = last
            carry_ref[chunk, slot] = u[TS - SUBLANES:, :] if valid is None else jnp.where(
                valid, u[TS - SUBLANES:, :], last)
            cw = cw_ref[...]
            out = cw[CONV_W - 1:CONV_W] * u + cb_ref[...]
            for back in range(1, CONV_W):
                shifted = u_in_ref[slot, pl.ds(SUBLANES - back, TS), :]
                out = out + cw[CONV_W - 1 - back:CONV_W - back] * shifted
            return out

        a = conv_half(cwa_ref, cba_ref, 0)
        g = conv_half(cwg_ref, cbg_ref, 1)
        act = g * _sigmoid(g) * a
        if valid is not None:
            act = jnp.where(valid, act, 0.0)
        acc_ref[...] += _dot(act.astype(BF16), wdown_ref[...])

    half_step(u1_ref, u0_ref, jnp.maximum(2 * k - 1, 0), k > 0, up_refs[0], mix_refs[0])
    half_step(u0_ref, u1_ref, 2 * k, None, up_refs[1], mix_refs[1])

    @pl.when(k == n_k - 1)
    def _():
        x2 = xres_ref[...] + acc_ref[...]
        hp = _rmsnorm(x2, gple_ref[...]).astype(BF16)
        gate = _sigmoid(_dot(hp, pgate_ref[...]))
        emb = _dot(p_ref[...].astype(BF16), pproj_ref[...])
        x3 = x2 + gate * emb
        if final_norm:
            x3 = _rmsnorm(x3, gfin_ref[...])
        o_ref[...] = x3


def _ffn_layer(x, i, attn, w_out, j, g_ffn, w_up, conv_w, conv_b, w_down, g_ple, ple_gate, p, ple_proj, g_final):
    B, S, D = x.shape
    n_c = D_FF // FC
    assert n_c % 2 == 1
    n_k = (n_c + 1) // 2
    has_attn = attn is not None
    final_norm = g_final is not None
    tok = lambda b, s, k: (b, s, 0)
    layer = lambda b, s, k: (i, 0, 0)
    vec = pl.BlockSpec((None, 1, D), layer)
    args, specs = [x], [pl.BlockSpec((None, TS, D), tok)]
    if has_attn:
        args += [attn, w_out]
        specs += [pl.BlockSpec((None, TS, D), tok), pl.BlockSpec((None, D, D), lambda b, s, k: (j, 0, 0))]
    args.append(g_ffn)
    specs.append(vec)
    for chunk in (lambda k: 2 * k, lambda k: jnp.minimum(2 * k + 1, n_c - 1)):
        args += [w_up, w_up]
        specs += [pl.BlockSpec((None, D, FC), lambda b, s, k, f=chunk: (i, 0, f(k))),
                  pl.BlockSpec((None, D, FC), lambda b, s, k, f=chunk: (i, 0, n_c + f(k)))]
    for chunk in (lambda k: jnp.maximum(2 * k - 1, 0), lambda k: 2 * k):
        args += [conv_w, conv_w, conv_b, conv_b, w_down]
        specs += [pl.BlockSpec((None, CONV_W, FC), lambda b, s, k, f=chunk: (i, 0, f(k))),
                  pl.BlockSpec((None, CONV_W, FC), lambda b, s, k, f=chunk: (i, 0, n_c + f(k))),
                  pl.BlockSpec((None, 1, FC), lambda b, s, k, f=chunk: (i, 0, f(k))),
                  pl.BlockSpec((None, 1, FC), lambda b, s, k, f=chunk: (i, 0, n_c + f(k))),
                  pl.BlockSpec((None, FC, D), lambda b, s, k, f=chunk: (i, f(k), 0))]
    args += [g_ple, ple_gate, p, ple_proj]
    specs += [
        vec,
        pl.BlockSpec((None, D, D), layer),
        pl.BlockSpec((None, None, TS, PLE_DIM), lambda b, s, k: (i, b, s, 0)),
        pl.BlockSpec((None, PLE_DIM, D), layer),
    ]
    if final_norm:
        args.append(g_final.reshape(1, D))
        specs.append(pl.BlockSpec((1, D), lambda b, s, k: (0, 0)))
    return pl.pallas_call(
        functools.partial(_ffn_kernel, has_attn=has_attn, final_norm=final_norm),
        grid=(B, S // TS, n_k),
        in_specs=specs,
        out_specs=pl.BlockSpec((None, TS, D), tok),
        out_shape=jax.ShapeDtypeStruct(x.shape, F32),
        scratch_shapes=[
            pltpu.VMEM((TS, D), F32),
            pltpu.VMEM((TS, D), BF16),
            pltpu.VMEM((TS, D), F32),
            pltpu.VMEM((n_c, 2, SUBLANES, FC), F32),
            pltpu.VMEM((2, SUBLANES + TS, FC), F32),
            pltpu.VMEM((2, SUBLANES + TS, FC), F32),
        ],
        compiler_params=_params(3),
        name="ffn_embed",
    )(*args)


def _kv_kernel(x_ref, g_ref, w_ref, rc_ref, ra_ref, rb_ref,
               kc_ref, vc_ref, ks_ref, vts_ref, kw_ref, vtw_ref):
    s = pl.program_id(1)
    lane = lax.broadcasted_iota(jnp.int32, (TS, LANES), 1)

    @pl.when(s == 0)
    def _():
        flag = jnp.where(lane == HEAD_DIM, 1.0, 0.0).astype(BF16)
        for g in range(N_KV_GROUPS):
            kw_ref[0, g] = flag
            vtw_ref[0, g] = jnp.zeros((LANES, TS), BF16)

    @pl.when(s > 0)
    def _():
        h = _rmsnorm(x_ref[0], g_ref[...]).astype(BF16)
        kv = _dot(h, w_ref[...])
        gw = N_KV_GROUPS * HEAD_DIM
        rc, ra, rb = rc_ref[0], ra_ref[0], rb_ref[0]
        tok = (s - 1) * TS + lax.broadcasted_iota(jnp.int32, (TS, LANES), 0)
        onehot = jnp.where(lane - HEAD_DIM == tok // SLC_BLOCK, 1.0, 0.0)
        row = lax.broadcasted_iota(jnp.int32, (LANES - HEAD_DIM, TS), 0)
        ones_row = jnp.where(row == 0, 1.0, 0.0).astype(BF16)

        def slabs(i):
            return [kv[:, i * gw + j * LANES:i * gw + (j + 1) * LANES] for j in range(gw // LANES)]

        for g in range(N_KV_GROUPS):
            kc_ref[0, g] = kv[:, 0 * gw + g * HEAD_DIM:0 * gw + (g + 1) * HEAD_DIM]
            vc_ref[0, g] = kv[:, 1 * gw + g * HEAD_DIM:1 * gw + (g + 1) * HEAD_DIM]

        for i, k_ref, vt_ref in ((2, ks_ref, vts_ref), (4, kw_ref, vtw_ref)):
            for j, (kslab, vslab) in enumerate(zip(slabs(i), slabs(i + 1))):
                kr = _rope128(kslab, rc, ra, rb)
                vt = vslab.T
                for half in range(2):
                    g = 2 * j + half
                    kh = kr if half == 0 else pltpu.roll(kr, HEAD_DIM, 1)
                    upper = onehot if k_ref is ks_ref else 0.0
                    k_ref[0, g] = jnp.where(lane < HEAD_DIM, kh, upper).astype(BF16)
                    vt_ref[0, g, 0:HEAD_DIM, :] = vt[half * HEAD_DIM:(half + 1) * HEAD_DIM, :].astype(BF16)
                    vt_ref[0, g, HEAD_DIM:, :] = ones_row


def _kv_project(x, g, w_kv, rc, ra, rb):
    B, S, D = x.shape
    G = N_KV_GROUPS
    assert WINDOW == TS
    tile = lambda s: jnp.maximum(s - 1, 0)
    tok = lambda b, s: (b, tile(s), 0)
    rows = lambda b, s: (b, 0, tile(s), 0)
    cols = lambda b, s: (b, 0, 0, tile(s))
    return pl.pallas_call(
        _kv_kernel,
        grid=(B, S // TS + 1),
        in_specs=[
            pl.BlockSpec((1, TS, D), tok),
            pl.BlockSpec((1, D), lambda b, s: (0, 0)),
            pl.BlockSpec(w_kv.shape, lambda b, s: (0, 0)),
            pl.BlockSpec((1, TS, LANES), tok),
            pl.BlockSpec((1, TS, LANES), tok),
            pl.BlockSpec((1, TS, LANES), tok),
        ],
        out_specs=[
            pl.BlockSpec((1, G, TS, HEAD_DIM), rows),
            pl.BlockSpec((1, G, TS, HEAD_DIM), rows),
            pl.BlockSpec((1, G, TS, LANES), rows),
            pl.BlockSpec((1, G, LANES, TS), cols),
            pl.BlockSpec((1, G, TS, LANES), lambda b, s: (b, 0, s, 0)),
            pl.BlockSpec((1, G, LANES, TS), lambda b, s: (b, 0, 0, s)),
        ],
        out_shape=[
            jax.ShapeDtypeStruct((B, G, S, HEAD_DIM), F32),
            jax.ShapeDtypeStruct((B, G, S, HEAD_DIM), F32),
            jax.ShapeDtypeStruct((B, G, S, LANES), BF16),
            jax.ShapeDtypeStruct((B, G, LANES, S), BF16),
            jax.ShapeDtypeStruct((B, G, WINDOW + S, LANES), BF16),
            jax.ShapeDtypeStruct((B, G, LANES, WINDOW + S), BF16),
        ],
        compiler_params=_params(2),
        name="kv_project",
    )(x, g.reshape(1, D), w_kv.astype(BF16), rc, ra, rb)


def _gelu_tanh(x):
    return 0.5 * x * (1.0 + jnp.tanh(math.sqrt(2.0 / math.pi) * (x + 0.044715 * (x * x * x))))


def _compress_kernel(kf_ref, vf_ref, pk_ref, w1k_ref, w2k_ref, pv_ref, w1v_ref, w2v_ref,
                     rc_ref, ra_ref, rb_ref, kc_ref, vct_ref):
    half = CMP_STRIDE * HEAD_DIM

    def hidden(f_ref, pos_ref, w1_ref):
        flat = f_ref[0, 0]
        pos = pos_ref[...]
        first = _dot((flat + pos[0:1]).astype(BF16), w1_ref[0:half, :])
        second = _dot((flat + pos[1:2]).astype(BF16), w1_ref[half:, :])
        n = flat.shape[0]
        hid = first + pltpu.roll(second, n - 1, 0)
        return _gelu_tanh(hid).astype(BF16)

    kc = _dot(hidden(kf_ref, pk_ref, w1k_ref), w2k_ref[...])
    kc_ref[0, 0] = _rope128(kc, rc_ref[0], ra_ref[0], rb_ref[0])[:, :HEAD_DIM].astype(BF16)
    vct = lax.dot_general(w2v_ref[...], hidden(vf_ref, pv_ref, w1v_ref), (((1,), (1,)), ((), ())),
                          preferred_element_type=F32)
    vct_ref[0, 0] = vct[:HEAD_DIM, :].astype(BF16)


def _compress(kc_flat, vc_flat, pos_k, w1_k, w2_k, pos_v, w1_v, w2_v, rc, ra, rb):
    B, G, n_chunk, width = kc_flat.shape
    blk = lambda b, g: (b, g, 0, 0)
    const2 = lambda b, g: (0, 0)
    pad2 = lambda w: jnp.pad(w, ((0, 0), (0, LANES - HEAD_DIM))).astype(BF16)
    return pl.pallas_call(
        _compress_kernel,
        grid=(B, G),
        in_specs=[
            pl.BlockSpec((1, 1, n_chunk, width), blk),
            pl.BlockSpec((1, 1, n_chunk, width), blk),
            pl.BlockSpec((2, width), const2),
            pl.BlockSpec((2 * width, CMP_HIDDEN), const2),
            pl.BlockSpec((CMP_HIDDEN, LANES), const2),
            pl.BlockSpec((2, width), const2),
            pl.BlockSpec((2 * width, CMP_HIDDEN), const2),
            pl.BlockSpec((CMP_HIDDEN, LANES), const2),
            pl.BlockSpec((1, n_chunk, LANES), lambda b, g: (b, 0, 0)),
            pl.BlockSpec((1, n_chunk, LANES), lambda b, g: (b, 0, 0)),
            pl.BlockSpec((1, n_chunk, LANES), lambda b, g: (b, 0, 0)),
        ],
        out_specs=[
            pl.BlockSpec((1, 1, n_chunk, HEAD_DIM), blk),
            pl.BlockSpec((1, 1, HEAD_DIM, n_chunk), blk),
        ],
        out_shape=[
            jax.ShapeDtypeStruct((B, G, n_chunk, HEAD_DIM), BF16),
            jax.ShapeDtypeStruct((B, G, HEAD_DIM, n_chunk), BF16),
        ],
        compiler_params=_params(2),
        name="compress",
    )(kc_flat, vc_flat, pos_k.reshape(2, width), w1_k.astype(BF16), pad2(w2_k),
      pos_v.reshape(2, width), w1_v.astype(BF16), pad2(w2_v).T, rc, ra, rb)


def _qproj_kernel(x_ref, g_ref, w_ref, rc_ref, ra_ref, rb_ref, q_ref, gate_ref):
    h = _rmsnorm(x_ref[0], g_ref[...]).astype(BF16)
    proj = _dot(h, w_ref[...])
    rc, ra, rb = rc_ref[0], ra_ref[0], rb_ref[0]
    scale = HEAD_DIM ** -0.5
    for j in range(N_HEADS // 2):
        qt = (_rope128(proj[:, j * LANES:(j + 1) * LANES], rc, ra, rb) * scale).T
        q_ref[0, 2 * j] = qt[:HEAD_DIM].astype(BF16)
        q_ref[0, 2 * j + 1] = qt[HEAD_DIM:].astype(BF16)
    n_q = N_HEADS * HEAD_DIM
    for g in range(N_KV_GROUPS):
        gt = _sigmoid(proj[:, n_q + g * LANES:n_q + (g + 1) * LANES]).T
        gate_ref[0, g] = gt[:gate_ref.shape[2]]


def _q_weights(w_in_b):
    L, D, _ = w_in_b.shape
    n_q = N_HEADS * HEAD_DIM
    per_group = HEADS_PER_GROUP * N_BRANCH
    gate_w = w_in_b[:, :, n_q:].reshape(L, D, N_KV_GROUPS, per_group)
    gate_w = jnp.pad(gate_w, ((0, 0), (0, 0), (0, 0), (0, LANES - per_group))).reshape(L, D, N_KV_GROUPS * LANES)
    return jnp.concatenate([w_in_b[:, :, :n_q], gate_w], axis=2).astype(BF16)


def _q_project(x, i, g, j, w, rc, ra, rb):
    B, S, D = x.shape
    tok = lambda b, s: (b, s, 0)
    return pl.pallas_call(
        _qproj_kernel,
        grid=(B, S // TS),
        in_specs=[
            pl.BlockSpec((1, TS, D), tok),
            pl.BlockSpec((None, 1, D), lambda b, s: (i, 0, 0)),
            pl.BlockSpec((None,) + w.shape[1:], lambda b, s: (j, 0, 0)),
            pl.BlockSpec((1, TS, LANES), tok),
            pl.BlockSpec((1, TS, LANES), tok),
            pl.BlockSpec((1, TS, LANES), tok),
        ],
        out_specs=[
            pl.BlockSpec((1, N_HEADS, HEAD_DIM, TS), lambda b, s: (b, 0, 0, s)),
            pl.BlockSpec((1, N_KV_GROUPS, GATE_ROWS, TS), lambda b, s: (b, 0, 0, s)),
        ],
        out_shape=[
            jax.ShapeDtypeStruct((B, N_HEADS, HEAD_DIM, S), BF16),
            jax.ShapeDtypeStruct((B, N_KV_GROUPS, GATE_ROWS, S), F32),
        ],
        compiler_params=_params(2),
        name="q_project",
    )(x, g, w, rc, ra, rb)


def _attn_kernel(q_ref, gate_ref, kc_ref, vct_ref, ovl_ref, cend_ref, edge_ref, ks_ref, vts_ref, kw_ref, vtw_ref,
                 o_ref, m_ref, acc_ref, sa_ref, sb_ref, sw_ref):
    qi = pl.program_id(2)
    s0 = qi * TQ
    hpg = HEADS_PER_GROUP
    n_slc = ovl_ref.shape[0]
    span = WINDOW + TQ
    w0 = pl.multiple_of(s0, TQ)

    qt = jnp.concatenate([q_ref[0, hh] for hh in range(hpg)], axis=1)

    sc = _dot(kc_ref[0, 0], qt)
    flag_row = lax.broadcasted_iota(jnp.int32, (LANES - HEAD_DIM, hpg * TQ), 0) == 0
    qt_win = jnp.concatenate([qt, jnp.where(flag_row, NEG, 0.0).astype(BF16)], axis=0)
    sw_ref[...] = _dot(kw_ref[0, 0, pl.ds(w0, span), :], qt_win)

    mask_c = cend_ref[...] <= s0
    sc = jnp.where(mask_c, sc, NEG)
    e = jnp.where(mask_c, jnp.exp(sc - jnp.max(sc, axis=0, keepdims=True)), 0.0)
    p_c = e * (1.0 / jnp.maximum(jnp.sum(e, axis=0, keepdims=True), 1e-30))
    oc_t = _dot(vct_ref[0, 0], p_c.astype(BF16))
    p_sum = p_c[:, 0:TQ]
    for hh in range(1, hpg):
        p_sum = p_sum + p_c[:, hh * TQ:(hh + 1) * TQ]
    imp_t = jnp.dot(ovl_ref[...], p_sum, precision=lax.Precision.HIGHEST,
                    preferred_element_type=F32)

    blk = lax.broadcasted_iota(jnp.int32, (n_slc, TQ), 0)
    t_col = s0 + lax.broadcasted_iota(jnp.int32, (n_slc, TQ), 1)
    cur = t_col // SLC_BLOCK
    forced = (blk == 0) | (blk == cur) | (blk == cur - 1)
    valid = blk * SLC_BLOCK <= t_col
    score = jnp.where(valid, jnp.where(forced, FORCE_SCORE, imp_t), NEG)
    groups = [score[r:r + SUBLANES] for r in range(0, n_slc, SUBLANES)]
    blk_in = lax.broadcasted_iota(jnp.int32, (SUBLANES, TQ), 0)
    ranks = [jnp.zeros((SUBLANES, TQ), F32) for _ in groups]
    for m in range(n_slc):
        other = score[m:m + 1, :]
        for gi, grp in enumerate(groups):
            lo = gi * SUBLANES
            if lo > m:
                ahead = other >= grp
            elif lo + SUBLANES - 1 <= m:
                ahead = other > grp
            else:
                ahead = (other > grp) | ((other == grp) & (blk_in + lo > m))
            ranks[gi] = ranks[gi] + jnp.where(ahead, 1.0, 0.0)
    rank = jnp.concatenate(ranks, axis=0)
    bias_t = jnp.where(valid & (rank < min(N_SELECT, n_slc)), 0.0, NEG).astype(BF16)
    qt_sel = jnp.concatenate([qt, jnp.concatenate([bias_t] * hpg, axis=1)], axis=0)

    def scores(j, out_ref):
        out_ref[...] = _dot(ks_ref[0, 0, pl.ds(pl.multiple_of(j * KC, KC), KC), :], qt_sel)

    def absorb(in_ref, j, diagonal=False):
        s = in_ref[...]
        if diagonal:
            s = s + edge_ref[0]
        m_old = m_ref[...]
        m_new = jnp.maximum(m_old, jnp.max(s, axis=0, keepdims=True))
        p = jnp.exp(s - m_new).astype(BF16)
        v_t = vts_ref[0, 0, :, pl.ds(pl.multiple_of(j * KC, KC), KC)]
        acc_ref[...] = jnp.exp(m_old - m_new) * acc_ref[...] + _dot(v_t, p)
        m_ref[...] = m_new

    m_ref[...] = jnp.full_like(m_ref, NEG)
    acc_ref[...] = jnp.zeros_like(acc_ref)
    scores(0, sa_ref)

    lo_edge = sw_ref[0:TQ] + edge_ref[1]
    middle = sw_ref[TQ:WINDOW]
    hi_edge = sw_ref[WINDOW:] + edge_ref[0]
    m_w = jnp.maximum(jnp.maximum(jnp.max(lo_edge, axis=0, keepdims=True), jnp.max(middle, axis=0, keepdims=True)),
                      jnp.max(hi_edge, axis=0, keepdims=True))
    rw = (_dot(vtw_ref[0, 0, :, pl.ds(w0, TQ)], jnp.exp(lo_edge - m_w).astype(BF16))
          + _dot(vtw_ref[0, 0, :, pl.ds(w0 + TQ, WINDOW - TQ)], jnp.exp(middle - m_w).astype(BF16))
          + _dot(vtw_ref[0, 0, :, pl.ds(w0 + WINDOW, TQ)], jnp.exp(hi_edge - m_w).astype(BF16)))
    ow_t = rw[:HEAD_DIM] * (1.0 / jnp.maximum(rw[HEAD_DIM:HEAD_DIM + 1], 1e-30))

    def pair(i, carry):
        scores(2 * i + 1, sb_ref)
        absorb(sa_ref, 2 * i)
        scores(2 * i + 2, sa_ref)
        absorb(sb_ref, 2 * i + 1)
        return carry

    lax.fori_loop(0, qi // 2, pair, 0)

    @pl.when(qi % 2 == 0)
    def _():
        absorb(sa_ref, qi, diagonal=True)

    @pl.when(qi % 2 == 1)
    def _():
        scores(qi, sb_ref)
        absorb(sa_ref, qi - 1)
        absorb(sb_ref, qi, diagonal=True)

    acc = acc_ref[...]
    os_t = acc[:HEAD_DIM] * (1.0 / acc[HEAD_DIM:HEAD_DIM + 1])

    gates = gate_ref[0, 0]
    outs = []
    for hh in range(hpg):
        c = slice(hh * TQ, (hh + 1) * TQ)
        r = hh * N_BRANCH
        outs.append(gates[r:r + 1] * oc_t[:, c] + gates[r + 1:r + 2] * os_t[:, c]
                    + gates[r + 2:r + 3] * ow_t[:, c])
    o_ref[0] = jnp.concatenate(outs, axis=0).T.astype(o_ref.dtype)


def _attention(q_t, gates_t, kc, vct, ovl_t, ks, vts, kw, vtw):
    B, _, _, S = q_t.shape
    G = N_KV_GROUPS
    hpg = HEADS_PER_GROUP
    assert KC == TQ and WINDOW % TQ == 0
    cols = hpg * TQ
    key = np.arange(TQ)[:, None]
    t_local = (np.arange(cols) % TQ)[None, :]
    cend = jnp.asarray(CMP_STRIDE * np.arange(kc.shape[2])[:, None] + CMP_LEN - 1 - t_local, jnp.int32)
    edges = jnp.asarray(np.where(np.stack([key <= t_local, key > t_local]), 0.0, NEG), F32)
    per_bg = lambda b, g, qi: (b, g, 0, 0)
    return pl.pallas_call(
        _attn_kernel,
        grid=(B, G, S // TQ),
        in_specs=[
            pl.BlockSpec((1, hpg, HEAD_DIM, TQ), lambda b, g, qi: (b, g, 0, qi)),
            pl.BlockSpec((1, 1, GATE_ROWS, TQ), lambda b, g, qi: (b, g, 0, qi)),
            pl.BlockSpec((1, 1) + kc.shape[2:], per_bg),
            pl.BlockSpec((1, 1) + vct.shape[2:], per_bg),
            pl.BlockSpec(ovl_t.shape, lambda b, g, qi: (0, 0)),
            pl.BlockSpec(cend.shape, lambda b, g, qi: (0, 0)),
            pl.BlockSpec(edges.shape, lambda b, g, qi: (0, 0, 0)),
            pl.BlockSpec((1, 1) + ks.shape[2:], per_bg),
            pl.BlockSpec((1, 1) + vts.shape[2:], per_bg),
            pl.BlockSpec((1, 1) + kw.shape[2:], per_bg),
            pl.BlockSpec((1, 1) + vtw.shape[2:], per_bg),
        ],
        out_specs=pl.BlockSpec((1, TQ, hpg * HEAD_DIM), lambda b, g, qi: (b, qi, g)),
        out_shape=jax.ShapeDtypeStruct((B, S, N_HEADS * HEAD_DIM), BF16),
        scratch_shapes=[
            pltpu.VMEM((1, cols), F32),
            pltpu.VMEM((LANES, cols), F32),
            pltpu.VMEM((KC, cols), F32),
            pltpu.VMEM((KC, cols), F32),
            pltpu.VMEM((WINDOW + TQ, cols), F32),
        ],
        compiler_params=_params(3),
        name="sparse_attention",
    )(q_t, gates_t, kc, vct, ovl_t, cend, edges, ks, vts, kw, vtw)


def _rope_tables(pos):
    inv_freq = ROPE_THETA ** (-jnp.arange(0, ROT_DIM, 2, dtype=F32) / ROT_DIM)
    ang = pos.astype(F32)[..., None] * inv_freq
    cs = jnp.concatenate([jnp.cos(ang), jnp.sin(ang)], axis=-1)
    half = ROT_DIM // 2
    place = np.zeros((3, ROT_DIM, LANES), np.float32)
    const = np.zeros((3, LANES), np.float32)
    for lane in range(LANES):
        d = lane % HEAD_DIM
        if d < ROT_DIM:
            place[0, d % half, lane] = 1.0
            if d < half:
                place[1, half + d, lane] = -1.0
            else:
                place[2, half + d - half, lane] = 1.0
        else:
            const[0, lane] = 1.0
    tabs = jnp.einsum('...k,tkn->t...n', cs, jnp.asarray(place), precision=lax.Precision.HIGHEST)
    return tuple(tabs[t] + const[t] for t in range(3))


def _overlap_t(n_chunk, n_slc):
    c_start = CMP_STRIDE * np.arange(n_chunk)
    blk = np.arange(n_slc)
    ovl = ((c_start[None] < (blk[:, None] + 1) * SLC_BLOCK)
           & (c_start[None] + CMP_LEN > blk[:, None] * SLC_BLOCK)
           & (np.arange(n_chunk)[None] < n_chunk - 1))
    return jnp.asarray(ovl.astype(np.float32))


def kernel(x, p, positions, norm_mix, pool_w, pool_scale, norm_kv, w_kv, cmp_pos_k, cmp_w1_k, cmp_w2_k, cmp_pos_v, cmp_w1_v, cmp_w2_v, w_in_b, w_out_b, norm_ffn, ffn_up, ffn_conv, ffn_conv_b, ffn_down, norm_ple, ple_gate, ple_proj, norm_final):
    B, S, D = x.shape
    n_chunk = S // CMP_STRIDE
    assert S // SLC_BLOCK == LANES - HEAD_DIM
    rc, ra, rb = _rope_tables(positions)
    end_idx = jnp.minimum(CMP_STRIDE * jnp.arange(n_chunk) + CMP_LEN - 1, S - 1)
    rc_c, ra_c, rb_c = _rope_tables(positions[:, end_idx])
    ovl_t = _overlap_t(n_chunk, S // SLC_BLOCK)

    stack = lambda g: g.reshape(g.shape[0], 1, g.shape[1])
    norm_mix3, norm_ffn3, norm_ple3, pool_scale3 = map(stack, (norm_mix, norm_ffn, norm_ple, pool_scale))
    conv_b3 = stack(ffn_conv_b)
    pool_w16, up16, down16, gate16, proj16, w_out16 = (
        w.astype(BF16) for w in (pool_w, ffn_up, ffn_down, ple_gate, ple_proj, w_out_b))
    w_q16 = _q_weights(w_in_b)

    kv = None
    for i in range(DEPTH):
        attn = None
        j = i - N_A
        if i < N_A:
            x = _pool_layer(x, i, norm_mix3, pool_w16, pool_scale3)
        else:
            if i == N_A:
                kc_raw, vc_raw, ks, vts, kw, vtw = _kv_project(x, norm_kv, w_kv, rc, ra, rb)
                flat = lambda z: z.reshape(B, N_KV_GROUPS, n_chunk, CMP_STRIDE * HEAD_DIM)
                kc, vct = _compress(flat(kc_raw), flat(vc_raw), cmp_pos_k, cmp_w1_k, cmp_w2_k,
                                    cmp_pos_v, cmp_w1_v, cmp_w2_v, rc_c, ra_c, rb_c)
                kv = (kc, vct, ovl_t, ks, vts, kw, vtw)
            q_t, gates_t = _q_project(x, i, norm_mix3, j, w_q16, rc, ra, rb)
            attn = _attention(q_t, gates_t, *kv)
        x = _ffn_layer(x, i, attn, w_out16, j, norm_ffn3, up16, ffn_conv, conv_b3, down16,
                       norm_ple3, gate16, p, proj16, norm_final if i == DEPTH - 1 else None)
    return x
```

```python
import functools
import math

import jax
import jax.numpy as jnp
import numpy as np
from jax import lax
from jax.experimental import pallas as pl
from jax.experimental.pallas import tpu as pltpu

D_MODEL = 1024
DEPTH = 4
N_A = DEPTH // 2
POOL_WINDOWS = (2, 4, 8, 16)
POOL_GROUP = D_MODEL // len(POOL_WINDOWS)
POOL_HALO = 16
N_HEADS = 16
HEAD_DIM = 64
N_KV_GROUPS = 4
HEADS_PER_GROUP = N_HEADS // N_KV_GROUPS
CMP_STRIDE = 16
CMP_LEN = 2 * CMP_STRIDE
CMP_HIDDEN = 128
SLC_BLOCK = 64
N_SELECT = 16
WINDOW = 512
N_BRANCH = 3
ROPE_THETA = 500000.0
ROT_DIM = HEAD_DIM // 4
D_FF = 2816
CONV_W = 3
PLE_DIM = 256
EPS = 1e-6
NEG = -1e30
FORCE_SCORE = 1e4

LANES = 128
SUBLANES = 8
VMEM_LIMIT = 48 * 1024 * 1024

TS = 512
FC = 256
FFN_UNROLL = 2
TQ = 256
KC = TQ
GATE_ROWS = 16

BF16 = jnp.bfloat16
F32 = jnp.float32


def _dot(a, b):
    return jnp.dot(a, b, preferred_element_type=F32)


def _rmsnorm(x, g):
    return x * lax.rsqrt(jnp.mean(x * x, axis=-1, keepdims=True) + EPS) * g


def _sigmoid(x):
    return 1.0 / (1.0 + jnp.exp(-x))


def _rope128(x, c, a, b):
    return x * c + pltpu.roll(x, LANES - ROT_DIM // 2, 1) * a + pltpu.roll(x, ROT_DIM // 2, 1) * b


def _params(n_grid):
    return pltpu.CompilerParams(dimension_semantics=("arbitrary",) * n_grid,
                                vmem_limit_bytes=VMEM_LIMIT)


def _pool_mix(h, history, s, w_ref, scale):
    ext = jnp.concatenate([history, h], axis=0)
    sums = {1: ext}
    w = 1
    while w < POOL_WINDOWS[-1]:
        prev = sums[w]
        sums[2 * w] = prev + pltpu.roll(prev, w, 0)
        w *= 2
    t = s * TS + lax.broadcasted_iota(jnp.int32, (TS, 1), 0)
    outs = []
    for gi, win in enumerate(POOL_WINDOWS):
        lo, hi = gi * POOL_GROUP, (gi + 1) * POOL_GROUP
        tot = sums[win][POOL_HALO:, lo:hi]
        cnt = jnp.minimum(t + 1, win).astype(F32)
        u = tot / cnt - h[:, lo:hi]
        outs.append(_dot(u.astype(BF16), w_ref[gi]))
    return jnp.concatenate(outs, axis=-1) * scale


def _ffn_kernel(*refs, has_attn, final_norm):
    it = iter(refs)
    x_ref = next(it)
    if has_attn:
        attn_ref, wout_ref = next(it), next(it)
    else:
        gmix_ref, poolw_ref, pscale_ref = next(it), next(it), next(it)
    gffn_ref = next(it)
    up_refs = [(next(it), next(it)) for _ in range(FFN_UNROLL)]
    mix_refs = [tuple(next(it) for _ in range(5)) for _ in range(FFN_UNROLL)]
    gple_ref, pgate_ref, p_ref, pproj_ref = (next(it) for _ in range(4))
    if final_norm:
        gfin_ref = next(it)
    o_ref = next(it)
    xres_ref, h_ref, acc_ref, carry_ref = (next(it) for _ in range(4))
    u_refs = (next(it), next(it))
    if not has_attn:
        hist_ref = next(it)

    s = pl.program_id(1)
    k = pl.program_id(2)
    n_k = pl.num_programs(2)

    @pl.when((s == 0) & (k == 0))
    def _():
        carry_ref[...] = jnp.zeros_like(carry_ref)
        if not has_attn:
            hist_ref[...] = jnp.zeros_like(hist_ref)

    @pl.when(k == 0)
    def _():
        xin = x_ref[...]
        if has_attn:
            xin = xin + _dot(attn_ref[...], wout_ref[...])
        else:
            hm = _rmsnorm(xin, gmix_ref[...])
            xin = xin + _pool_mix(hm, hist_ref[...], s, poolw_ref, pscale_ref[...])
            hist_ref[...] = hm[TS - POOL_HALO:, :]
        xres_ref[...] = xin
        h_ref[...] = _rmsnorm(xin, gffn_ref[...]).astype(BF16)
        acc_ref[...] = jnp.zeros_like(acc_ref)
        u_refs[1][...] = jnp.zeros_like(u_refs[1])

    def half_step(u_in_ref, u_out_ref, chunk, valid, up, mix):
        wa_ref, wg_ref = up
        cwa_ref, cwg_ref, cba_ref, cbg_ref, wdown_ref = mix
        u_out_ref[0, SUBLANES:, :] = _dot(h_ref[...], wa_ref[...])
        u_out_ref[1, SUBLANES:, :] = _dot(h_ref[...], wg_ref[...])

        def conv_half(cw_ref, cb_ref, slot):
            u = u_in_ref[slot, SUBLANES:, :]
            last = carry_ref[chunk, slot]
            u_in_ref[slot, 0:SUBLANES, :] = last
            carry_ref[chunk, slot] = u[TS - SUBLANES:, :] if valid is None else jnp.where(
                valid, u[TS - SUBLANES:, :], last)
            cw = cw_ref[...]
            out = cw[CONV_W - 1:CONV_W] * u + cb_ref[...]
            for back in range(1, CONV_W):
                shifted = u_in_ref[slot, pl.ds(SUBLANES - back, TS), :]
                out = out + cw[CONV_W - 1 - back:CONV_W - back] * shifted
            return out

        a = conv_half(cwa_ref, cba_ref, 0)
        g = conv_half(cwg_ref, cbg_ref, 1)
        act = g * _sigmoid(g) * a
        if valid is not None:
            act = jnp.where(valid, act, 0.0)
        acc_ref[...] += _dot(act.astype(BF16), wdown_ref[...])

    for r in range(FFN_UNROLL):
        half_step(u_refs[(r + 1) % 2], u_refs[r % 2], jnp.maximum(FFN_UNROLL * k + r - 1, 0),
                  (k > 0) if r == 0 else None, up_refs[r], mix_refs[r])

    @pl.when(k == n_k - 1)
    def _():
        x2 = xres_ref[...] + acc_ref[...]
        hp = _rmsnorm(x2, gple_ref[...]).astype(BF16)
        gate = _sigmoid(_dot(hp, pgate_ref[...]))
        emb = _dot(p_ref[...].astype(BF16), pproj_ref[...])
        x3 = x2 + gate * emb
        if final_norm:
            x3 = _rmsnorm(x3, gfin_ref[...])
        o_ref[...] = x3


def _ffn_layer(x, i, attn, w_out, j, g_mix, pool_w, pool_scale, g_ffn, w_up, conv_w, conv_b, w_down,
               g_ple, ple_gate, p, ple_proj, g_final):
    B, S, D = x.shape
    n_c = D_FF // FC
    assert (n_c + 1) % FFN_UNROLL == 0 and FFN_UNROLL % 2 == 0
    n_k = (n_c + 1) // FFN_UNROLL
    has_attn = attn is not None
    final_norm = g_final is not None
    tok = lambda b, s, k: (b, s, 0)
    layer = lambda b, s, k: (i, 0, 0)
    vec = pl.BlockSpec((None, 1, D), layer)
    args, specs = [x], [pl.BlockSpec((None, TS, D), tok)]
    if has_attn:
        args += [attn, w_out]
        specs += [pl.BlockSpec((None, TS, D), tok), pl.BlockSpec((None, D, D), lambda b, s, k: (j, 0, 0))]
    else:
        args += [g_mix, pool_w, pool_scale]
        specs += [vec, pl.BlockSpec((None, len(POOL_WINDOWS), POOL_GROUP, POOL_GROUP), lambda b, s, k: (i, 0, 0, 0)),
                  vec]
    args.append(g_ffn)
    specs.append(vec)
    for r in range(FFN_UNROLL):
        chunk = lambda k, r=r: jnp.minimum(FFN_UNROLL * k + r, n_c - 1)
        args += [w_up, w_up]
        specs += [pl.BlockSpec((None, D, FC), lambda b, s, k, f=chunk: (i, 0, f(k))),
                  pl.BlockSpec((None, D, FC), lambda b, s, k, f=chunk: (i, 0, n_c + f(k)))]
    for r in range(FFN_UNROLL):
        chunk = lambda k, r=r: jnp.maximum(FFN_UNROLL * k + r - 1, 0)
        args += [conv_w, conv_w, conv_b, conv_b, w_down]
        specs += [pl.BlockSpec((None, CONV_W, FC), lambda b, s, k, f=chunk: (i, 0, f(k))),
                  pl.BlockSpec((None, CONV_W, FC), lambda b, s, k, f=chunk: (i, 0, n_c + f(k))),
                  pl.BlockSpec((None, 1, FC), lambda b, s, k, f=chunk: (i, 0, f(k))),
                  pl.BlockSpec((None, 1, FC), lambda b, s, k, f=chunk: (i, 0, n_c + f(k))),
                  pl.BlockSpec((None, FC, D), lambda b, s, k, f=chunk: (i, f(k), 0))]
    args += [g_ple, ple_gate, p, ple_proj]
    specs += [
        vec,
        pl.BlockSpec((None, D, D), layer),
        pl.BlockSpec((None, None, TS, PLE_DIM), lambda b, s, k: (i, b, s, 0)),
        pl.BlockSpec((None, PLE_DIM, D), layer),
    ]
    if final_norm:
        args.append(g_final.reshape(1, D))
        specs.append(pl.BlockSpec((1, D), lambda b, s, k: (0, 0)))
    return pl.pallas_call(
        functools.partial(_ffn_kernel, has_attn=has_attn, final_norm=final_norm),
        grid=(B, S // TS, n_k),
        in_specs=specs,
        out_specs=pl.BlockSpec((None, TS, D), tok),
        out_shape=jax.ShapeDtypeStruct(x.shape, F32),
        scratch_shapes=[
            pltpu.VMEM((TS, D), F32),
            pltpu.VMEM((TS, D), BF16),
            pltpu.VMEM((TS, D), F32),
            pltpu.VMEM((n_c, 2, SUBLANES, FC), F32),
            pltpu.VMEM((2, SUBLANES + TS, FC), F32),
            pltpu.VMEM((2, SUBLANES + TS, FC), F32),
        ] + ([] if has_attn else [pltpu.VMEM((POOL_HALO, D), F32)]),
        compiler_params=_params(3),
        name="ffn_embed",
    )(*args)


def _kv_kernel(x_ref, g_ref, w_ref, rc_ref, ra_ref, rb_ref,
               kc_ref, vc_ref, ks_ref, vts_ref, kw_ref, vtw_ref):
    s = pl.program_id(1)
    lane = lax.broadcasted_iota(jnp.int32, (TS, LANES), 1)

    @pl.when(s == 0)
    def _():
        flag = jnp.where(lane == HEAD_DIM, 1.0, 0.0).astype(BF16)
        for g in range(N_KV_GROUPS):
            kw_ref[0, g] = flag
            vtw_ref[0, g] = jnp.zeros((LANES, TS), BF16)

    @pl.when(s > 0)
    def _():
        h = _rmsnorm(x_ref[0], g_ref[...]).astype(BF16)
        kv = _dot(h, w_ref[...])
        gw = N_KV_GROUPS * HEAD_DIM
        rc, ra, rb = rc_ref[0], ra_ref[0], rb_ref[0]
        tok = (s - 1) * TS + lax.broadcasted_iota(jnp.int32, (TS, LANES), 0)
        onehot = jnp.where(lane - HEAD_DIM == tok // SLC_BLOCK, 1.0, 0.0)
        row = lax.broadcasted_iota(jnp.int32, (LANES - HEAD_DIM, TS), 0)
        ones_row = jnp.where(row == 0, 1.0, 0.0).astype(BF16)

        def slabs(i):
            return [kv[:, i * gw + j * LANES:i * gw + (j + 1) * LANES] for j in range(gw // LANES)]

        for g in range(N_KV_GROUPS):
            kc_ref[0, g] = kv[:, 0 * gw + g * HEAD_DIM:0 * gw + (g + 1) * HEAD_DIM]
            vc_ref[0, g] = kv[:, 1 * gw + g * HEAD_DIM:1 * gw + (g + 1) * HEAD_DIM]

        for i, k_ref, vt_ref in ((2, ks_ref, vts_ref), (4, kw_ref, vtw_ref)):
            for j, (kslab, vslab) in enumerate(zip(slabs(i), slabs(i + 1))):
                kr = _rope128(kslab, rc, ra, rb)
                vt = vslab.T
                for half in range(2):
                    g = 2 * j + half
                    kh = kr if half == 0 else pltpu.roll(kr, HEAD_DIM, 1)
                    upper = onehot if k_ref is ks_ref else 0.0
                    k_ref[0, g] = jnp.where(lane < HEAD_DIM, kh, upper).astype(BF16)
                    vt_ref[0, g, 0:HEAD_DIM, :] = vt[half * HEAD_DIM:(half + 1) * HEAD_DIM, :].astype(BF16)
                    vt_ref[0, g, HEAD_DIM:, :] = ones_row


def _kv_project(x, g, w_kv, rc, ra, rb):
    B, S, D = x.shape
    G = N_KV_GROUPS
    assert WINDOW == TS
    tile = lambda s: jnp.maximum(s - 1, 0)
    tok = lambda b, s: (b, tile(s), 0)
    rows = lambda b, s: (b, 0, tile(s), 0)
    cols = lambda b, s: (b, 0, 0, tile(s))
    return pl.pallas_call(
        _kv_kernel,
        grid=(B, S // TS + 1),
        in_specs=[
            pl.BlockSpec((1, TS, D), tok),
            pl.BlockSpec((1, D), lambda b, s: (0, 0)),
            pl.BlockSpec(w_kv.shape, lambda b, s: (0, 0)),
            pl.BlockSpec((1, TS, LANES), tok),
            pl.BlockSpec((1, TS, LANES), tok),
            pl.BlockSpec((1, TS, LANES), tok),
        ],
        out_specs=[
            pl.BlockSpec((1, G, TS, HEAD_DIM), rows),
            pl.BlockSpec((1, G, TS, HEAD_DIM), rows),
            pl.BlockSpec((1, G, TS, LANES), rows),
            pl.BlockSpec((1, G, LANES, TS), cols),
            pl.BlockSpec((1, G, TS, LANES), lambda b, s: (b, 0, s, 0)),
            pl.BlockSpec((1, G, LANES, TS), lambda b, s: (b, 0, 0, s)),
        ],
        out_shape=[
            jax.ShapeDtypeStruct((B, G, S, HEAD_DIM), F32),
            jax.ShapeDtypeStruct((B, G, S, HEAD_DIM), F32),
            jax.ShapeDtypeStruct((B, G, S, LANES), BF16),
            jax.ShapeDtypeStruct((B, G, LANES, S), BF16),
            jax.ShapeDtypeStruct((B, G, WINDOW + S, LANES), BF16),
            jax.ShapeDtypeStruct((B, G, LANES, WINDOW + S), BF16),
        ],
        compiler_params=_params(2),
        name="kv_project",
    )(x, g.reshape(1, D), w_kv.astype(BF16), rc, ra, rb)


def _gelu_tanh(x):
    return 0.5 * x * (1.0 + jnp.tanh(math.sqrt(2.0 / math.pi) * (x + 0.044715 * (x * x * x))))


def _compress_kernel(k_ref, v_ref, pk_ref, w1k_ref, w2k_ref, pv_ref, w1v_ref, w2v_ref,
                     rc_ref, ra_ref, rb_ref, kc_ref, vct_ref):
    n_chunk = kc_ref.shape[2]

    def hidden(x_ref, pos_ref, w1_ref):
        first = jnp.zeros((n_chunk, CMP_HIDDEN), F32)
        second = jnp.zeros((n_chunk, CMP_HIDDEN), F32)
        for r in range(CMP_STRIDE):
            tok = x_ref[0, 0, pl.ds(r, n_chunk, stride=CMP_STRIDE), :]
            lead = (tok + pos_ref[r:r + 1, :]).astype(BF16)
            trail = (tok + pos_ref[CMP_STRIDE + r:CMP_STRIDE + r + 1, :]).astype(BF16)
            first = first + _dot(lead, w1_ref[r * HEAD_DIM:(r + 1) * HEAD_DIM, :])
            second = second + _dot(trail, w1_ref[(CMP_STRIDE + r) * HEAD_DIM:(CMP_STRIDE + r + 1) * HEAD_DIM, :])
        hid = first + pltpu.roll(second, n_chunk - 1, 0)
        return _gelu_tanh(hid).astype(BF16)

    kc = _dot(hidden(k_ref, pk_ref, w1k_ref), w2k_ref[...])
    kc_ref[0, 0] = _rope128(kc, rc_ref[0], ra_ref[0], rb_ref[0])[:, :HEAD_DIM].astype(BF16)
    vct = lax.dot_general(w2v_ref[...], hidden(v_ref, pv_ref, w1v_ref), (((1,), (1,)), ((), ())),
                          preferred_element_type=F32)
    vct_ref[0, 0] = vct[:HEAD_DIM, :].astype(BF16)


def _compress(k_raw, v_raw, pos_k, w1_k, w2_k, pos_v, w1_v, w2_v, rc, ra, rb):
    B, G, S, _ = k_raw.shape
    n_chunk = S // CMP_STRIDE
    blk = lambda b, g: (b, g, 0, 0)
    const2 = lambda b, g: (0, 0)
    pad2 = lambda w: jnp.pad(w, ((0, 0), (0, LANES - HEAD_DIM))).astype(BF16)
    return pl.pallas_call(
        _compress_kernel,
        grid=(B, G),
        in_specs=[
            pl.BlockSpec((1, 1, S, HEAD_DIM), blk),
            pl.BlockSpec((1, 1, S, HEAD_DIM), blk),
            pl.BlockSpec((CMP_LEN, HEAD_DIM), const2),
            pl.BlockSpec((CMP_LEN * HEAD_DIM, CMP_HIDDEN), const2),
            pl.BlockSpec((CMP_HIDDEN, LANES), const2),
            pl.BlockSpec((CMP_LEN, HEAD_DIM), const2),
            pl.BlockSpec((CMP_LEN * HEAD_DIM, CMP_HIDDEN), const2),
            pl.BlockSpec((CMP_HIDDEN, LANES), const2),
            pl.BlockSpec((1, n_chunk, LANES), lambda b, g: (b, 0, 0)),
            pl.BlockSpec((1, n_chunk, LANES), lambda b, g: (b, 0, 0)),
            pl.BlockSpec((1, n_chunk, LANES), lambda b, g: (b, 0, 0)),
        ],
        out_specs=[
            pl.BlockSpec((1, 1, n_chunk, HEAD_DIM), blk),
            pl.BlockSpec((1, 1, HEAD_DIM, n_chunk), blk),
        ],
        out_shape=[
            jax.ShapeDtypeStruct((B, G, n_chunk, HEAD_DIM), BF16),
            jax.ShapeDtypeStruct((B, G, HEAD_DIM, n_chunk), BF16),
        ],
        compiler_params=_params(2),
        name="compress",
    )(k_raw, v_raw, pos_k, w1_k.astype(BF16), pad2(w2_k), pos_v, w1_v.astype(BF16), pad2(w2_v).T, rc, ra, rb)


def _qproj_kernel(x_ref, g_ref, w_ref, rc_ref, ra_ref, rb_ref, q_ref, gate_ref):
    h = _rmsnorm(x_ref[0], g_ref[...]).astype(BF16)
    proj = _dot(h, w_ref[...])
    rc, ra, rb = rc_ref[0], ra_ref[0], rb_ref[0]
    scale = HEAD_DIM ** -0.5
    for j in range(N_HEADS // 2):
        qt = (_rope128(proj[:, j * LANES:(j + 1) * LANES], rc, ra, rb) * scale).T
        q_ref[0, 2 * j] = qt[:HEAD_DIM].astype(BF16)
        q_ref[0, 2 * j + 1] = qt[HEAD_DIM:].astype(BF16)
    n_q = N_HEADS * HEAD_DIM
    for g in range(N_KV_GROUPS):
        gt = _sigmoid(proj[:, n_q + g * LANES:n_q + (g + 1) * LANES]).T
        gate_ref[0, g] = gt[:gate_ref.shape[2]]


def _q_weights(w_in_b):
    L, D, _ = w_in_b.shape
    n_q = N_HEADS * HEAD_DIM
    per_group = HEADS_PER_GROUP * N_BRANCH
    gate_w = w_in_b[:, :, n_q:].reshape(L, D, N_KV_GROUPS, per_group)
    gate_w = jnp.pad(gate_w, ((0, 0), (0, 0), (0, 0), (0, LANES - per_group))).reshape(L, D, N_KV_GROUPS * LANES)
    return jnp.concatenate([w_in_b[:, :, :n_q], gate_w], axis=2).astype(BF16)


def _q_project(x, i, g, j, w, rc, ra, rb):
    B, S, D = x.shape
    tok = lambda b, s: (b, s, 0)
    return pl.pallas_call(
        _qproj_kernel,
        grid=(B, S // TS),
        in_specs=[
            pl.BlockSpec((1, TS, D), tok),
            pl.BlockSpec((None, 1, D), lambda b, s: (i, 0, 0)),
            pl.BlockSpec((None,) + w.shape[1:], lambda b, s: (j, 0, 0)),
            pl.BlockSpec((1, TS, LANES), tok),
            pl.BlockSpec((1, TS, LANES), tok),
            pl.BlockSpec((1, TS, LANES), tok),
        ],
        out_specs=[
            pl.BlockSpec((1, N_HEADS, HEAD_DIM, TS), lambda b, s: (b, 0, 0, s)),
            pl.BlockSpec((1, N_KV_GROUPS, GATE_ROWS, TS), lambda b, s: (b, 0, 0, s)),
        ],
        out_shape=[
            jax.ShapeDtypeStruct((B, N_HEADS, HEAD_DIM, S), BF16),
            jax.ShapeDtypeStruct((B, N_KV_GROUPS, GATE_ROWS, S), F32),
        ],
        compiler_params=_params(2),
        name="q_project",
    )(x, g, w, rc, ra, rb)


def _attn_kernel(q_ref, gate_ref, kc_ref, vct_ref, ovl_ref, cend_ref, edge_ref, ks_ref, vts_ref, kw_ref, vtw_ref,
                 o_ref, m_ref, acc_ref, sa_ref, sb_ref, sw_ref):
    qi = pl.program_id(2)
    s0 = qi * TQ
    hpg = HEADS_PER_GROUP
    n_slc = ovl_ref.shape[0]
    span = WINDOW + TQ
    w0 = pl.multiple_of(s0, TQ)

    qt = jnp.concatenate([q_ref[0, hh] for hh in range(hpg)], axis=1)

    sc = _dot(kc_ref[0, 0], qt)
    flag_row = lax.broadcasted_iota(jnp.int32, (LANES - HEAD_DIM, hpg * TQ), 0) == 0
    qt_win = jnp.concatenate([qt, jnp.where(flag_row, NEG, 0.0).astype(BF16)], axis=0)
    sw_ref[...] = _dot(kw_ref[0, 0, pl.ds(w0, span), :], qt_win)

    mask_c = cend_ref[...] <= s0
    sc = jnp.where(mask_c, sc, NEG)
    e = jnp.where(mask_c, jnp.exp(sc - jnp.max(sc, axis=0, keepdims=True)), 0.0)
    p_c = e * (1.0 / jnp.maximum(jnp.sum(e, axis=0, keepdims=True), 1e-30))
    oc_t = _dot(vct_ref[0, 0], p_c.astype(BF16))
    p_sum = p_c[:, 0:TQ]
    for hh in range(1, hpg):
        p_sum = p_sum + p_c[:, hh * TQ:(hh + 1) * TQ]
    imp_t = jnp.dot(ovl_ref[...], p_sum, precision=lax.Precision.HIGHEST,
                    preferred_element_type=F32)

    blk = lax.broadcasted_iota(jnp.int32, (n_slc, TQ), 0)
    t_col = s0 + lax.broadcasted_iota(jnp.int32, (n_slc, TQ), 1)
    cur = t_col // SLC_BLOCK
    forced = (blk == 0) | (blk == cur) | (blk == cur - 1)
    valid = blk * SLC_BLOCK <= t_col
    score = jnp.where(valid, jnp.where(forced, FORCE_SCORE, imp_t), NEG)
    groups = [score[r:r + SUBLANES] for r in range(0, n_slc, SUBLANES)]
    blk_in = lax.broadcasted_iota(jnp.int32, (SUBLANES, TQ), 0)
    ranks = [jnp.zeros((SUBLANES, TQ), F32) for _ in groups]
    for m in range(n_slc):
        other = score[m:m + 1, :]
        for gi, grp in enumerate(groups):
            lo = gi * SUBLANES
            if lo > m:
                ahead = other >= grp
            elif lo + SUBLANES - 1 <= m:
                ahead = other > grp
            else:
                ahead = (other > grp) | ((other == grp) & (blk_in + lo > m))
            ranks[gi] = ranks[gi] + jnp.where(ahead, 1.0, 0.0)
    rank = jnp.concatenate(ranks, axis=0)
    bias_t = jnp.where(valid & (rank < min(N_SELECT, n_slc)), 0.0, NEG).astype(BF16)
    qt_sel = jnp.concatenate([qt, jnp.concatenate([bias_t] * hpg, axis=1)], axis=0)

    def scores(j, out_ref):
        out_ref[...] = _dot(ks_ref[0, 0, pl.ds(pl.multiple_of(j * KC, KC), KC), :], qt_sel)

    def absorb(in_ref, j, diagonal=False):
        s = in_ref[...]
        if diagonal:
            s = s + edge_ref[0]
        m_old = m_ref[...]
        m_new = jnp.maximum(m_old, jnp.max(s, axis=0, keepdims=True))
        p = jnp.exp(s - m_new).astype(BF16)
        v_t = vts_ref[0, 0, :, pl.ds(pl.multiple_of(j * KC, KC), KC)]
        acc_ref[...] = jnp.exp(m_old - m_new) * acc_ref[...] + _dot(v_t, p)
        m_ref[...] = m_new

    m_ref[...] = jnp.full_like(m_ref, NEG)
    acc_ref[...] = jnp.zeros_like(acc_ref)
    scores(0, sa_ref)

    lo_edge = sw_ref[0:TQ] + edge_ref[1]
    middle = sw_ref[TQ:WINDOW]
    hi_edge = sw_ref[WINDOW:] + edge_ref[0]
    m_w = jnp.maximum(jnp.maximum(jnp.max(lo_edge, axis=0, keepdims=True), jnp.max(middle, axis=0, keepdims=True)),
                      jnp.max(hi_edge, axis=0, keepdims=True))
    rw = (_dot(vtw_ref[0, 0, :, pl.ds(w0, TQ)], jnp.exp(lo_edge - m_w).astype(BF16))
          + _dot(vtw_ref[0, 0, :, pl.ds(w0 + TQ, WINDOW - TQ)], jnp.exp(middle - m_w).astype(BF16))
          + _dot(vtw_ref[0, 0, :, pl.ds(w0 + WINDOW, TQ)], jnp.exp(hi_edge - m_w).astype(BF16)))
    ow_t = rw[:HEAD_DIM] * (1.0 / jnp.maximum(rw[HEAD_DIM:HEAD_DIM + 1], 1e-30))

    def pair(i, carry):
        scores(2 * i + 1, sb_ref)
        absorb(sa_ref, 2 * i)
        scores(2 * i + 2, sa_ref)
        absorb(sb_ref, 2 * i + 1)
        return carry

    lax.fori_loop(0, qi // 2, pair, 0)

    @pl.when(qi % 2 == 0)
    def _():
        absorb(sa_ref, qi, diagonal=True)

    @pl.when(qi % 2 == 1)
    def _():
        scores(qi, sb_ref)
        absorb(sa_ref, qi - 1)
        absorb(sb_ref, qi, diagonal=True)

    acc = acc_ref[...]
    os_t = acc[:HEAD_DIM] * (1.0 / acc[HEAD_DIM:HEAD_DIM + 1])

    gates = gate_ref[0, 0]
    outs = []
    for hh in range(hpg):
        c = slice(hh * TQ, (hh + 1) * TQ)
        r = hh * N_BRANCH
        outs.append(gates[r:r + 1] * oc_t[:, c] + gates[r + 1:r + 2] * os_t[:, c]
                    + gates[r + 2:r + 3] * ow_t[:, c])
    o_ref[0] = jnp.concatenate(outs, axis=0).T.astype(o_ref.dtype)


def _attention(q_t, gates_t, kc, vct, ovl_t, ks, vts, kw, vtw):
    B, _, _, S = q_t.shape
    G = N_KV_GROUPS
    hpg = HEADS_PER_GROUP
    assert KC == TQ and WINDOW % TQ == 0
    cols = hpg * TQ
    key = np.arange(TQ)[:, None]
    t_local = (np.arange(cols) % TQ)[None, :]
    cend = jnp.asarray(CMP_STRIDE * np.arange(kc.shape[2])[:, None] + CMP_LEN - 1 - t_local, jnp.int32)
    edges = jnp.asarray(np.where(np.stack([key <= t_local, key > t_local]), 0.0, NEG), F32)
    per_bg = lambda b, g, qi: (b, g, 0, 0)
    return pl.pallas_call(
        _attn_kernel,
        grid=(B, G, S // TQ),
        in_specs=[
            pl.BlockSpec((1, hpg, HEAD_DIM, TQ), lambda b, g, qi: (b, g, 0, qi)),
            pl.BlockSpec((1, 1, GATE_ROWS, TQ), lambda b, g, qi: (b, g, 0, qi)),
            pl.BlockSpec((1, 1) + kc.shape[2:], per_bg),
            pl.BlockSpec((1, 1) + vct.shape[2:], per_bg),
            pl.BlockSpec(ovl_t.shape, lambda b, g, qi: (0, 0)),
            pl.BlockSpec(cend.shape, lambda b, g, qi: (0, 0)),
            pl.BlockSpec(edges.shape, lambda b, g, qi: (0, 0, 0)),
            pl.BlockSpec((1, 1) + ks.shape[2:], per_bg),
            pl.BlockSpec((1, 1) + vts.shape[2:], per_bg),
            pl.BlockSpec((1, 1) + kw.shape[2:], per_bg),
            pl.BlockSpec((1, 1) + vtw.shape[2:], per_bg),
        ],
        out_specs=pl.BlockSpec((1, TQ, hpg * HEAD_DIM), lambda b, g, qi: (b, qi, g)),
        out_shape=jax.ShapeDtypeStruct((B, S, N_HEADS * HEAD_DIM), BF16),
        scratch_shapes=[
            pltpu.VMEM((1, cols), F32),
            pltpu.VMEM((LANES, cols), F32),
            pltpu.VMEM((KC, cols), F32),
            pltpu.VMEM((KC, cols), F32),
            pltpu.VMEM((WINDOW + TQ, cols), F32),
        ],
        compiler_params=_params(3),
        name="sparse_attention",
    )(q_t, gates_t, kc, vct, ovl_t, cend, edges, ks, vts, kw, vtw)


def _rope_tables(pos):
    inv_freq = ROPE_THETA ** (-jnp.arange(0, ROT_DIM, 2, dtype=F32) / ROT_DIM)
    ang = pos.astype(F32)[..., None] * inv_freq
    cs = jnp.concatenate([jnp.cos(ang), jnp.sin(ang)], axis=-1)
    half = ROT_DIM // 2
    place = np.zeros((3, ROT_DIM, LANES), np.float32)
    const = np.zeros((3, LANES), np.float32)
    for lane in range(LANES):
        d = lane % HEAD_DIM
        if d < ROT_DIM:
            place[0, d % half, lane] = 1.0
            if d < half:
                place[1, half + d, lane] = -1.0
            else:
                place[2, half + d - half, lane] = 1.0
        else:
            const[0, lane] = 1.0
    tabs = jnp.einsum('...k,tkn->t...n', cs, jnp.asarray(place), precision=lax.Precision.HIGHEST)
    return tuple(tabs[t] + const[t] for t in range(3))


def _overlap_t(n_chunk, n_slc):
    c_start = CMP_STRIDE * np.arange(n_chunk)
    blk = np.arange(n_slc)
    ovl = ((c_start[None] < (blk[:, None] + 1) * SLC_BLOCK)
           & (c_start[None] + CMP_LEN > blk[:, None] * SLC_BLOCK)
           & (np.arange(n_chunk)[None] < n_chunk - 1))
    return jnp.asarray(ovl.astype(np.float32))


def kernel(x, p, positions, norm_mix, pool_w, pool_scale, norm_kv, w_kv, cmp_pos_k, cmp_w1_k, cmp_w2_k, cmp_pos_v, cmp_w1_v, cmp_w2_v, w_in_b, w_out_b, norm_ffn, ffn_up, ffn_conv, ffn_conv_b, ffn_down, norm_ple, ple_gate, ple_proj, norm_final):
    B, S, D = x.shape
    n_chunk = S // CMP_STRIDE
    assert S // SLC_BLOCK == LANES - HEAD_DIM
    rc, ra, rb = _rope_tables(positions)
    end_idx = jnp.minimum(CMP_STRIDE * jnp.arange(n_chunk) + CMP_LEN - 1, S - 1)
    rc_c, ra_c, rb_c = _rope_tables(positions[:, end_idx])
    ovl_t = _overlap_t(n_chunk, S // SLC_BLOCK)

    stack = lambda g: g.reshape(g.shape[0], 1, g.shape[1])
    norm_mix3, norm_ffn3, norm_ple3, pool_scale3 = map(stack, (norm_mix, norm_ffn, norm_ple, pool_scale))
    conv_b3 = stack(ffn_conv_b)
    pool_w16, up16, down16, gate16, proj16, w_out16 = (
        w.astype(BF16) for w in (pool_w, ffn_up, ffn_down, ple_gate, ple_proj, w_out_b))
    w_q16 = _q_weights(w_in_b)

    kv = None
    for i in range(DEPTH):
        attn = None
        j = i - N_A
        if i >= N_A:
            if i == N_A:
                kc_raw, vc_raw, ks, vts, kw, vtw = _kv_project(x, norm_kv, w_kv, rc, ra, rb)
                kc, vct = _compress(kc_raw, vc_raw, cmp_pos_k, cmp_w1_k, cmp_w2_k,
                                    cmp_pos_v, cmp_w1_v, cmp_w2_v, rc_c, ra_c, rb_c)
                kv = (kc, vct, ovl_t, ks, vts, kw, vtw)
            q_t, gates_t = _q_project(x, i, norm_mix3, j, w_q16, rc, ra, rb)
            attn = _attention(q_t, gates_t, *kv)
        x = _ffn_layer(x, i, attn, w_out16, j, norm_mix3, pool_w16, pool_scale3, norm_ffn3, up16, ffn_conv, conv_b3,
                       down16, norm_ple3, gate16, p, proj16, norm_final if i == DEPTH - 1 else None)
    return x
```

```python
import functools
import math

import jax
import jax.numpy as jnp
import numpy as np
from jax import lax
from jax.experimental import pallas as pl
from jax.experimental.pallas import tpu as pltpu

D_MODEL = 1024
DEPTH = 4
N_A = DEPTH // 2
POOL_WINDOWS = (2, 4, 8, 16)
POOL_GROUP = D_MODEL // len(POOL_WINDOWS)
POOL_HALO = 16
N_HEADS = 16
HEAD_DIM = 64
N_KV_GROUPS = 4
HEADS_PER_GROUP = N_HEADS // N_KV_GROUPS
CMP_STRIDE = 16
CMP_LEN = 2 * CMP_STRIDE
CMP_HIDDEN = 128
SLC_BLOCK = 64
N_SELECT = 16
WINDOW = 512
N_BRANCH = 3
ROPE_THETA = 500000.0
ROT_DIM = HEAD_DIM // 4
D_FF = 2816
CONV_W = 3
PLE_DIM = 256
EPS = 1e-6
NEG = -1e30
FORCE_SCORE = 1e4

LANES = 128
SUBLANES = 8
VMEM_LIMIT = 48 * 1024 * 1024

TS = 512
FC = 256
FFN_UNROLL = 2
TQ = 256
KC = TQ
ATTN_GROUPS = 2
GATE_ROWS = 16

BF16 = jnp.bfloat16
F32 = jnp.float32


def _dot(a, b):
    return jnp.dot(a, b, preferred_element_type=F32)


def _rmsnorm(x, g):
    return x * lax.rsqrt(jnp.mean(x * x, axis=-1, keepdims=True) + EPS) * g


def _sigmoid(x):
    return 1.0 / (1.0 + jnp.exp(-x))


def _rope128(x, c, a, b):
    return x * c + pltpu.roll(x, LANES - ROT_DIM // 2, 1) * a + pltpu.roll(x, ROT_DIM // 2, 1) * b


def _params(n_grid):
    return pltpu.CompilerParams(dimension_semantics=("arbitrary",) * n_grid,
                                vmem_limit_bytes=VMEM_LIMIT)


def _pool_mix(h, history, s, w_ref, scale):
    ext = jnp.concatenate([history, h], axis=0)
    sums = {1: ext}
    w = 1
    while w < POOL_WINDOWS[-1]:
        prev = sums[w]
        sums[2 * w] = prev + pltpu.roll(prev, w, 0)
        w *= 2
    t = s * TS + lax.broadcasted_iota(jnp.int32, (TS, 1), 0)
    outs = []
    for gi, win in enumerate(POOL_WINDOWS):
        lo, hi = gi * POOL_GROUP, (gi + 1) * POOL_GROUP
        tot = sums[win][POOL_HALO:, lo:hi]
        cnt = jnp.minimum(t + 1, win).astype(F32)
        u = tot / cnt - h[:, lo:hi]
        outs.append(_dot(u.astype(BF16), w_ref[gi]))
    return jnp.concatenate(outs, axis=-1) * scale


def _ffn_kernel(*refs, has_attn, final_norm):
    it = iter(refs)
    x_ref = next(it)
    if has_attn:
        attn_ref, wout_ref = next(it), next(it)
    else:
        gmix_ref, poolw_ref, pscale_ref = next(it), next(it), next(it)
    gffn_ref = next(it)
    up_refs = [(next(it), next(it)) for _ in range(FFN_UNROLL)]
    mix_refs = [tuple(next(it) for _ in range(5)) for _ in range(FFN_UNROLL)]
    gple_ref, pgate_ref, p_ref, pproj_ref = (next(it) for _ in range(4))
    if final_norm:
        gfin_ref = next(it)
    o_ref = next(it)
    xres_ref, h_ref, acc_ref, carry_ref = (next(it) for _ in range(4))
    u_refs = (next(it), next(it))
    if not has_attn:
        hist_ref = next(it)

    s = pl.program_id(1)
    k = pl.program_id(2)
    n_k = pl.num_programs(2)

    @pl.when((s == 0) & (k == 0))
    def _():
        carry_ref[...] = jnp.zeros_like(carry_ref)
        if not has_attn:
            hist_ref[...] = jnp.zeros_like(hist_ref)

    @pl.when(k == 0)
    def _():
        xin = x_ref[...]
        if has_attn:
            xin = xin + _dot(attn_ref[...], wout_ref[...])
        else:
            hm = _rmsnorm(xin, gmix_ref[...])
            xin = xin + _pool_mix(hm, hist_ref[...], s, poolw_ref, pscale_ref[...])
            hist_ref[...] = hm[TS - POOL_HALO:, :]
        xres_ref[...] = xin
        h_ref[...] = _rmsnorm(xin, gffn_ref[...]).astype(BF16)
        acc_ref[...] = jnp.zeros_like(acc_ref)
        u_refs[1][...] = jnp.zeros_like(u_refs[1])

    def half_step(u_in_ref, u_out_ref, chunk, valid, up, mix):
        wa_ref, wg_ref = up
        cwa_ref, cwg_ref, cba_ref, cbg_ref, wdown_ref = mix
        u_out_ref[0, SUBLANES:, :] = _dot(h_ref[...], wa_ref[...])
        u_out_ref[1, SUBLANES:, :] = _dot(h_ref[...], wg_ref[...])

        def conv_half(cw_ref, cb_ref, slot):
            u = u_in_ref[slot, SUBLANES:, :]
            last = carry_ref[chunk, slot]
            u_in_ref[slot, 0:SUBLANES, :] = last
            carry_ref[chunk, slot] = u[TS - SUBLANES:, :] if valid is None else jnp.where(
                valid, u[TS - SUBLANES:, :], last)
            cw = cw_ref[...]
            out = cw[CONV_W - 1:CONV_W] * u + cb_ref[...]
            for back in range(1, CONV_W):
                shifted = u_in_ref[slot, pl.ds(SUBLANES - back, TS), :]
                out = out + cw[CONV_W - 1 - back:CONV_W - back] * shifted
            return out

        a = conv_half(cwa_ref, cba_ref, 0)
        g = conv_half(cwg_ref, cbg_ref, 1)
        act = g * _sigmoid(g) * a
        if valid is not None:
            act = jnp.where(valid, act, 0.0)
        acc_ref[...] += _dot(act.astype(BF16), wdown_ref[...])

    for r in range(FFN_UNROLL):
        half_step(u_refs[(r + 1) % 2], u_refs[r % 2], jnp.maximum(FFN_UNROLL * k + r - 1, 0),
                  (k > 0) if r == 0 else None, up_refs[r], mix_refs[r])

    @pl.when(k == n_k - 1)
    def _():
        x2 = xres_ref[...] + acc_ref[...]
        hp = _rmsnorm(x2, gple_ref[...]).astype(BF16)
        gate = _sigmoid(_dot(hp, pgate_ref[...]))
        emb = _dot(p_ref[...].astype(BF16), pproj_ref[...])
        x3 = x2 + gate * emb
        if final_norm:
            x3 = _rmsnorm(x3, gfin_ref[...])
        o_ref[...] = x3


def _ffn_layer(x, i, attn, w_out, j, g_mix, pool_w, pool_scale, g_ffn, w_up, conv_w, conv_b, w_down,
               g_ple, ple_gate, p, ple_proj, g_final):
    B, S, D = x.shape
    n_c = D_FF // FC
    assert (n_c + 1) % FFN_UNROLL == 0 and FFN_UNROLL % 2 == 0
    n_k = (n_c + 1) // FFN_UNROLL
    has_attn = attn is not None
    final_norm = g_final is not None
    tok = lambda b, s, k: (b, s, 0)
    layer = lambda b, s, k: (i, 0, 0)
    vec = pl.BlockSpec((None, 1, D), layer)
    args, specs = [x], [pl.BlockSpec((None, TS, D), tok)]
    if has_attn:
        args += [attn, w_out]
        specs += [pl.BlockSpec((None, TS, D), tok), pl.BlockSpec((None, D, D), lambda b, s, k: (j, 0, 0))]
    else:
        args += [g_mix, pool_w, pool_scale]
        specs += [vec, pl.BlockSpec((None, len(POOL_WINDOWS), POOL_GROUP, POOL_GROUP), lambda b, s, k: (i, 0, 0, 0)),
                  vec]
    args.append(g_ffn)
    specs.append(vec)
    for r in range(FFN_UNROLL):
        chunk = lambda k, r=r: jnp.minimum(FFN_UNROLL * k + r, n_c - 1)
        args += [w_up, w_up]
        specs += [pl.BlockSpec((None, D, FC), lambda b, s, k, f=chunk: (i, 0, f(k))),
                  pl.BlockSpec((None, D, FC), lambda b, s, k, f=chunk: (i, 0, n_c + f(k)))]
    for r in range(FFN_UNROLL):
        chunk = lambda k, r=r: jnp.maximum(FFN_UNROLL * k + r - 1, 0)
        args += [conv_w, conv_w, conv_b, conv_b, w_down]
        specs += [pl.BlockSpec((None, CONV_W, FC), lambda b, s, k, f=chunk: (i, 0, f(k))),
                  pl.BlockSpec((None, CONV_W, FC), lambda b, s, k, f=chunk: (i, 0, n_c + f(k))),
                  pl.BlockSpec((None, 1, FC), lambda b, s, k, f=chunk: (i, 0, f(k))),
                  pl.BlockSpec((None, 1, FC), lambda b, s, k, f=chunk: (i, 0, n_c + f(k))),
                  pl.BlockSpec((None, FC, D), lambda b, s, k, f=chunk: (i, f(k), 0))]
    args += [g_ple, ple_gate, p, ple_proj]
    specs += [
        vec,
        pl.BlockSpec((None, D, D), layer),
        pl.BlockSpec((None, None, TS, PLE_DIM), lambda b, s, k: (i, b, s, 0)),
        pl.BlockSpec((None, PLE_DIM, D), layer),
    ]
    if final_norm:
        args.append(g_final.reshape(1, D))
        specs.append(pl.BlockSpec((1, D), lambda b, s, k: (0, 0)))
    return pl.pallas_call(
        functools.partial(_ffn_kernel, has_attn=has_attn, final_norm=final_norm),
        grid=(B, S // TS, n_k),
        in_specs=specs,
        out_specs=pl.BlockSpec((None, TS, D), tok),
        out_shape=jax.ShapeDtypeStruct(x.shape, F32),
        scratch_shapes=[
            pltpu.VMEM((TS, D), F32),
            pltpu.VMEM((TS, D), BF16),
            pltpu.VMEM((TS, D), F32),
            pltpu.VMEM((n_c, 2, SUBLANES, FC), F32),
            pltpu.VMEM((2, SUBLANES + TS, FC), F32),
            pltpu.VMEM((2, SUBLANES + TS, FC), F32),
        ] + ([] if has_attn else [pltpu.VMEM((POOL_HALO, D), F32)]),
        compiler_params=_params(3),
        name="ffn_embed",
    )(*args)


def _kv_kernel(x_ref, g_ref, w_ref, rc_ref, ra_ref, rb_ref,
               kc_ref, vc_ref, ks_ref, vts_ref, kw_ref, vtw_ref):
    s = pl.program_id(1)
    lane = lax.broadcasted_iota(jnp.int32, (TS, LANES), 1)

    @pl.when(s == 0)
    def _():
        flag = jnp.where(lane == HEAD_DIM, 1.0, 0.0).astype(BF16)
        for g in range(N_KV_GROUPS):
            kw_ref[0, g] = flag
            vtw_ref[0, g] = jnp.zeros((LANES, TS), BF16)

    @pl.when(s > 0)
    def _():
        h = _rmsnorm(x_ref[0], g_ref[...]).astype(BF16)
        kv = _dot(h, w_ref[...])
        gw = N_KV_GROUPS * HEAD_DIM
        rc, ra, rb = rc_ref[0], ra_ref[0], rb_ref[0]
        tok = (s - 1) * TS + lax.broadcasted_iota(jnp.int32, (TS, LANES), 0)
        onehot = jnp.where(lane - HEAD_DIM == tok // SLC_BLOCK, 1.0, 0.0)
        row = lax.broadcasted_iota(jnp.int32, (LANES - HEAD_DIM, TS), 0)
        ones_row = jnp.where(row == 0, 1.0, 0.0).astype(BF16)

        def slabs(i):
            return [kv[:, i * gw + j * LANES:i * gw + (j + 1) * LANES] for j in range(gw // LANES)]

        for g in range(N_KV_GROUPS):
            kc_ref[0, g] = kv[:, 0 * gw + g * HEAD_DIM:0 * gw + (g + 1) * HEAD_DIM]
            vc_ref[0, g] = kv[:, 1 * gw + g * HEAD_DIM:1 * gw + (g + 1) * HEAD_DIM]

        for i, k_ref, vt_ref in ((2, ks_ref, vts_ref), (4, kw_ref, vtw_ref)):
            for j, (kslab, vslab) in enumerate(zip(slabs(i), slabs(i + 1))):
                kr = _rope128(kslab, rc, ra, rb)
                vt = vslab.T
                for half in range(2):
                    g = 2 * j + half
                    kh = kr if half == 0 else pltpu.roll(kr, HEAD_DIM, 1)
                    upper = onehot if k_ref is ks_ref else 0.0
                    k_ref[0, g] = jnp.where(lane < HEAD_DIM, kh, upper).astype(BF16)
                    vt_ref[0, g, 0:HEAD_DIM, :] = vt[half * HEAD_DIM:(half + 1) * HEAD_DIM, :].astype(BF16)
                    vt_ref[0, g, HEAD_DIM:, :] = ones_row


def _kv_project(x, g, w_kv, rc, ra, rb):
    B, S, D = x.shape
    G = N_KV_GROUPS
    assert WINDOW == TS
    tile = lambda s: jnp.maximum(s - 1, 0)
    tok = lambda b, s: (b, tile(s), 0)
    rows = lambda b, s: (b, 0, tile(s), 0)
    cols = lambda b, s: (b, 0, 0, tile(s))
    return pl.pallas_call(
        _kv_kernel,
        grid=(B, S // TS + 1),
        in_specs=[
            pl.BlockSpec((1, TS, D), tok),
            pl.BlockSpec((1, D), lambda b, s: (0, 0)),
            pl.BlockSpec(w_kv.shape, lambda b, s: (0, 0)),
            pl.BlockSpec((1, TS, LANES), tok),
            pl.BlockSpec((1, TS, LANES), tok),
            pl.BlockSpec((1, TS, LANES), tok),
        ],
        out_specs=[
            pl.BlockSpec((1, G, TS, HEAD_DIM), rows),
            pl.BlockSpec((1, G, TS, HEAD_DIM), rows),
            pl.BlockSpec((1, G, TS, LANES), rows),
            pl.BlockSpec((1, G, LANES, TS), cols),
            pl.BlockSpec((1, G, TS, LANES), lambda b, s: (b, 0, s, 0)),
            pl.BlockSpec((1, G, LANES, TS), lambda b, s: (b, 0, 0, s)),
        ],
        out_shape=[
            jax.ShapeDtypeStruct((B, G, S, HEAD_DIM), F32),
            jax.ShapeDtypeStruct((B, G, S, HEAD_DIM), F32),
            jax.ShapeDtypeStruct((B, G, S, LANES), BF16),
            jax.ShapeDtypeStruct((B, G, LANES, S), BF16),
            jax.ShapeDtypeStruct((B, G, WINDOW + S, LANES), BF16),
            jax.ShapeDtypeStruct((B, G, LANES, WINDOW + S), BF16),
        ],
        compiler_params=_params(2),
        name="kv_project",
    )(x, g.reshape(1, D), w_kv.astype(BF16), rc, ra, rb)


def _gelu_tanh(x):
    return 0.5 * x * (1.0 + jnp.tanh(math.sqrt(2.0 / math.pi) * (x + 0.044715 * (x * x * x))))


def _compress_kernel(k_ref, v_ref, pk_ref, w1k_ref, w2k_ref, pv_ref, w1v_ref, w2v_ref,
                     rc_ref, ra_ref, rb_ref, kc_ref, vct_ref):
    n_chunk = kc_ref.shape[2]

    def hidden(x_ref, pos_ref, w1_ref):
        first = jnp.zeros((n_chunk, CMP_HIDDEN), F32)
        second = jnp.zeros((n_chunk, CMP_HIDDEN), F32)
        for r in range(CMP_STRIDE):
            tok = x_ref[0, 0, pl.ds(r, n_chunk, stride=CMP_STRIDE), :]
            lead = (tok + pos_ref[r:r + 1, :]).astype(BF16)
            trail = (tok + pos_ref[CMP_STRIDE + r:CMP_STRIDE + r + 1, :]).astype(BF16)
            first = first + _dot(lead, w1_ref[r * HEAD_DIM:(r + 1) * HEAD_DIM, :])
            second = second + _dot(trail, w1_ref[(CMP_STRIDE + r) * HEAD_DIM:(CMP_STRIDE + r + 1) * HEAD_DIM, :])
        hid = first + pltpu.roll(second, n_chunk - 1, 0)
        return _gelu_tanh(hid).astype(BF16)

    kc = _dot(hidden(k_ref, pk_ref, w1k_ref), w2k_ref[...])
    kc_ref[0, 0] = _rope128(kc, rc_ref[0], ra_ref[0], rb_ref[0])[:, :HEAD_DIM].astype(BF16)
    vct = lax.dot_general(w2v_ref[...], hidden(v_ref, pv_ref, w1v_ref), (((1,), (1,)), ((), ())),
                          preferred_element_type=F32)
    vct_ref[0, 0] = vct[:HEAD_DIM, :].astype(BF16)


def _compress(k_raw, v_raw, pos_k, w1_k, w2_k, pos_v, w1_v, w2_v, rc, ra, rb):
    B, G, S, _ = k_raw.shape
    n_chunk = S // CMP_STRIDE
    blk = lambda b, g: (b, g, 0, 0)
    const2 = lambda b, g: (0, 0)
    pad2 = lambda w: jnp.pad(w, ((0, 0), (0, LANES - HEAD_DIM))).astype(BF16)
    return pl.pallas_call(
        _compress_kernel,
        grid=(B, G),
        in_specs=[
            pl.BlockSpec((1, 1, S, HEAD_DIM), blk),
            pl.BlockSpec((1, 1, S, HEAD_DIM), blk),
            pl.BlockSpec((CMP_LEN, HEAD_DIM), const2),
            pl.BlockSpec((CMP_LEN * HEAD_DIM, CMP_HIDDEN), const2),
            pl.BlockSpec((CMP_HIDDEN, LANES), const2),
            pl.BlockSpec((CMP_LEN, HEAD_DIM), const2),
            pl.BlockSpec((CMP_LEN * HEAD_DIM, CMP_HIDDEN), const2),
            pl.BlockSpec((CMP_HIDDEN, LANES), const2),
            pl.BlockSpec((1, n_chunk, LANES), lambda b, g: (b, 0, 0)),
            pl.BlockSpec((1, n_chunk, LANES), lambda b, g: (b, 0, 0)),
            pl.BlockSpec((1, n_chunk, LANES), lambda b, g: (b, 0, 0)),
        ],
        out_specs=[
            pl.BlockSpec((1, 1, n_chunk, HEAD_DIM), blk),
            pl.BlockSpec((1, 1, HEAD_DIM, n_chunk), blk),
        ],
        out_shape=[
            jax.ShapeDtypeStruct((B, G, n_chunk, HEAD_DIM), BF16),
            jax.ShapeDtypeStruct((B, G, HEAD_DIM, n_chunk), BF16),
        ],
        compiler_params=_params(2),
        name="compress",
    )(k_raw, v_raw, pos_k, w1_k.astype(BF16), pad2(w2_k), pos_v, w1_v.astype(BF16), pad2(w2_v).T, rc, ra, rb)


def _qproj_kernel(x_ref, g_ref, w_ref, rc_ref, ra_ref, rb_ref, q_ref, gate_ref):
    h = _rmsnorm(x_ref[0], g_ref[...]).astype(BF16)
    proj = _dot(h, w_ref[...])
    rc, ra, rb = rc_ref[0], ra_ref[0], rb_ref[0]
    scale = HEAD_DIM ** -0.5 * math.log2(math.e)
    for j in range(N_HEADS // 2):
        qt = (_rope128(proj[:, j * LANES:(j + 1) * LANES], rc, ra, rb) * scale).T
        q_ref[0, 2 * j] = qt[:HEAD_DIM].astype(BF16)
        q_ref[0, 2 * j + 1] = qt[HEAD_DIM:].astype(BF16)
    n_q = N_HEADS * HEAD_DIM
    for g in range(N_KV_GROUPS):
        gt = _sigmoid(proj[:, n_q + g * LANES:n_q + (g + 1) * LANES]).T
        gate_ref[0, g] = gt[:gate_ref.shape[2]]


def _q_weights(w_in_b):
    L, D, _ = w_in_b.shape
    n_q = N_HEADS * HEAD_DIM
    per_group = HEADS_PER_GROUP * N_BRANCH
    gate_w = w_in_b[:, :, n_q:].reshape(L, D, N_KV_GROUPS, per_group)
    gate_w = jnp.pad(gate_w, ((0, 0), (0, 0), (0, 0), (0, LANES - per_group))).reshape(L, D, N_KV_GROUPS * LANES)
    return jnp.concatenate([w_in_b[:, :, :n_q], gate_w], axis=2).astype(BF16)


def _q_project(x, i, g, j, w, rc, ra, rb):
    B, S, D = x.shape
    tok = lambda b, s: (b, s, 0)
    return pl.pallas_call(
        _qproj_kernel,
        grid=(B, S // TS),
        in_specs=[
            pl.BlockSpec((1, TS, D), tok),
            pl.BlockSpec((None, 1, D), lambda b, s: (i, 0, 0)),
            pl.BlockSpec((None,) + w.shape[1:], lambda b, s: (j, 0, 0)),
            pl.BlockSpec((1, TS, LANES), tok),
            pl.BlockSpec((1, TS, LANES), tok),
            pl.BlockSpec((1, TS, LANES), tok),
        ],
        out_specs=[
            pl.BlockSpec((1, N_HEADS, HEAD_DIM, TS), lambda b, s: (b, 0, 0, s)),
            pl.BlockSpec((1, N_KV_GROUPS, GATE_ROWS, TS), lambda b, s: (b, 0, 0, s)),
        ],
        out_shape=[
            jax.ShapeDtypeStruct((B, N_HEADS, HEAD_DIM, S), BF16),
            jax.ShapeDtypeStruct((B, N_KV_GROUPS, GATE_ROWS, S), F32),
        ],
        compiler_params=_params(2),
        name="q_project",
    )(x, g, w, rc, ra, rb)


def _attn_kernel(q_ref, gate_ref, kc_ref, vct_ref, ovl_ref, cend_ref, edge_ref, ks_ref, vts_ref, kw_ref, vtw_ref,
                 o_ref, m_ref, acc_ref, sa_ref, sb_ref, sw_ref):
    qi = pl.program_id(2)
    s0 = qi * TQ
    hpg = HEADS_PER_GROUP
    n_slc = ovl_ref.shape[0]
    span = WINDOW + TQ
    w0 = pl.multiple_of(s0, TQ)
    groups = range(ATTN_GROUPS)

    flag_row = lax.broadcasted_iota(jnp.int32, (LANES - HEAD_DIM, hpg * TQ), 0) == 0
    flag_rows = jnp.where(flag_row, NEG, 0.0).astype(BF16)
    qt, sc = [], []
    for g in groups:
        qt.append(jnp.concatenate([q_ref[0, g * hpg + hh] for hh in range(hpg)], axis=1))
        sc.append(_dot(kc_ref[0, g], qt[g]))
        sw_ref[g] = _dot(kw_ref[0, g, pl.ds(w0, span), :], jnp.concatenate([qt[g], flag_rows], axis=0))

    mask_c = cend_ref[...] <= s0
    oc_t, imp_t = [], []
    for g in groups:
        s_c = jnp.where(mask_c, sc[g], NEG)
        e = jnp.where(mask_c, jnp.exp2(s_c - jnp.max(s_c, axis=0, keepdims=True)), 0.0)
        p_c = e * (1.0 / jnp.maximum(jnp.sum(e, axis=0, keepdims=True), 1e-30))
        oc_t.append(_dot(vct_ref[0, g], p_c.astype(BF16)))
        p_sum = p_c[:, 0:TQ]
        for hh in range(1, hpg):
            p_sum = p_sum + p_c[:, hh * TQ:(hh + 1) * TQ]
        imp_t.append(jnp.dot(ovl_ref[...], p_sum, precision=lax.Precision.HIGHEST,
                             preferred_element_type=F32))

    blk = lax.broadcasted_iota(jnp.int32, (n_slc, TQ), 0)
    t_col = s0 + lax.broadcasted_iota(jnp.int32, (n_slc, TQ), 1)
    cur = t_col // SLC_BLOCK
    forced = (blk == 0) | (blk == cur) | (blk == cur - 1)
    valid = blk * SLC_BLOCK <= t_col
    blk_in = lax.broadcasted_iota(jnp.int32, (SUBLANES, TQ), 0)
    qt_sel = []
    for g in groups:
        score = jnp.where(valid, jnp.where(forced, FORCE_SCORE, imp_t[g]), NEG)
        rows = [score[r:r + SUBLANES] for r in range(0, n_slc, SUBLANES)]
        ranks = [jnp.zeros((SUBLANES, TQ), F32) for _ in rows]
        for m in range(n_slc):
            other = score[m:m + 1, :]
            for gi, grp in enumerate(rows):
                lo = gi * SUBLANES
                if lo > m:
                    ahead = other >= grp
                elif lo + SUBLANES - 1 <= m:
                    ahead = other > grp
                else:
                    ahead = (other > grp) | ((other == grp) & (blk_in + lo > m))
                ranks[gi] = ranks[gi] + jnp.where(ahead, 1.0, 0.0)
        rank = jnp.concatenate(ranks, axis=0)
        bias_t = jnp.where(valid & (rank < min(N_SELECT, n_slc)), 0.0, NEG).astype(BF16)
        qt_sel.append(jnp.concatenate([qt[g], jnp.concatenate([bias_t] * hpg, axis=1)], axis=0))

    def scores(j, out_ref):
        for g in groups:
            out_ref[g] = _dot(ks_ref[0, g, pl.ds(pl.multiple_of(j * KC, KC), KC), :], qt_sel[g])

    def absorb(in_ref, j, diagonal=False):
        for g in groups:
            s = in_ref[g]
            if diagonal:
                s = s + edge_ref[0]
            m_old = m_ref[g]
            m_new = jnp.maximum(m_old, jnp.max(s, axis=0, keepdims=True))
            p = jnp.exp2(s - m_new).astype(BF16)
            v_t = vts_ref[0, g, :, pl.ds(pl.multiple_of(j * KC, KC), KC)]
            acc_ref[g] = jnp.exp2(m_old - m_new) * acc_ref[g] + _dot(v_t, p)
            m_ref[g] = m_new

    m_ref[...] = jnp.full_like(m_ref, NEG)
    acc_ref[...] = jnp.zeros_like(acc_ref)
    scores(0, sa_ref)

    ow_t = []
    for g in groups:
        lo_edge = sw_ref[g, 0:TQ] + edge_ref[1]
        middle = sw_ref[g, TQ:WINDOW]
        hi_edge = sw_ref[g, WINDOW:] + edge_ref[0]
        m_w = jnp.maximum(jnp.maximum(jnp.max(lo_edge, axis=0, keepdims=True),
                                      jnp.max(middle, axis=0, keepdims=True)),
                          jnp.max(hi_edge, axis=0, keepdims=True))
        rw = (_dot(vtw_ref[0, g, :, pl.ds(w0, TQ)], jnp.exp2(lo_edge - m_w).astype(BF16))
              + _dot(vtw_ref[0, g, :, pl.ds(w0 + TQ, WINDOW - TQ)], jnp.exp2(middle - m_w).astype(BF16))
              + _dot(vtw_ref[0, g, :, pl.ds(w0 + WINDOW, TQ)], jnp.exp2(hi_edge - m_w).astype(BF16)))
        ow_t.append(rw[:HEAD_DIM] * (1.0 / jnp.maximum(rw[HEAD_DIM:HEAD_DIM + 1], 1e-30)))

    def pair(i, carry):
        scores(2 * i + 1, sb_ref)
        absorb(sa_ref, 2 * i)
        scores(2 * i + 2, sa_ref)
        absorb(sb_ref, 2 * i + 1)
        return carry

    lax.fori_loop(0, qi // 2, pair, 0)

    @pl.when(qi % 2 == 0)
    def _():
        absorb(sa_ref, qi, diagonal=True)

    @pl.when(qi % 2 == 1)
    def _():
        scores(qi, sb_ref)
        absorb(sa_ref, qi - 1)
        absorb(sb_ref, qi, diagonal=True)

    outs = []
    for g in groups:
        acc = acc_ref[g]
        os_t = acc[:HEAD_DIM] * (1.0 / acc[HEAD_DIM:HEAD_DIM + 1])
        gates = gate_ref[0, g]
        for hh in range(hpg):
            c = slice(hh * TQ, (hh + 1) * TQ)
            r = hh * N_BRANCH
            outs.append(gates[r:r + 1] * oc_t[g][:, c] + gates[r + 1:r + 2] * os_t[:, c]
                        + gates[r + 2:r + 3] * ow_t[g][:, c])
    o_ref[0] = jnp.concatenate(outs, axis=0).T.astype(o_ref.dtype)


def _attention(q_t, gates_t, kc, vct, ovl_t, ks, vts, kw, vtw):
    B, _, _, S = q_t.shape
    G = N_KV_GROUPS
    hpg = HEADS_PER_GROUP
    ng = ATTN_GROUPS
    assert KC == TQ and WINDOW % TQ == 0 and G % ng == 0
    cols = hpg * TQ
    key = np.arange(TQ)[:, None]
    t_local = (np.arange(cols) % TQ)[None, :]
    cend = jnp.asarray(CMP_STRIDE * np.arange(kc.shape[2])[:, None] + CMP_LEN - 1 - t_local, jnp.int32)
    edges = jnp.asarray(np.where(np.stack([key <= t_local, key > t_local]), 0.0, NEG), F32)
    per_bg = lambda b, g, qi: (b, g, 0, 0)
    return pl.pallas_call(
        _attn_kernel,
        grid=(B, G // ng, S // TQ),
        in_specs=[
            pl.BlockSpec((1, ng * hpg, HEAD_DIM, TQ), lambda b, g, qi: (b, g, 0, qi)),
            pl.BlockSpec((1, ng, GATE_ROWS, TQ), lambda b, g, qi: (b, g, 0, qi)),
            pl.BlockSpec((1, ng) + kc.shape[2:], per_bg),
            pl.BlockSpec((1, ng) + vct.shape[2:], per_bg),
            pl.BlockSpec(ovl_t.shape, lambda b, g, qi: (0, 0)),
            pl.BlockSpec(cend.shape, lambda b, g, qi: (0, 0)),
            pl.BlockSpec(edges.shape, lambda b, g, qi: (0, 0, 0)),
            pl.BlockSpec((1, ng) + ks.shape[2:], per_bg),
            pl.BlockSpec((1, ng) + vts.shape[2:], per_bg),
            pl.BlockSpec((1, ng) + kw.shape[2:], per_bg),
            pl.BlockSpec((1, ng) + vtw.shape[2:], per_bg),
        ],
        out_specs=pl.BlockSpec((1, TQ, ng * hpg * HEAD_DIM), lambda b, g, qi: (b, qi, g)),
        out_shape=jax.ShapeDtypeStruct((B, S, N_HEADS * HEAD_DIM), BF16),
        scratch_shapes=[
            pltpu.VMEM((ng, 1, cols), F32),
            pltpu.VMEM((ng, LANES, cols), F32),
            pltpu.VMEM((ng, KC, cols), F32),
            pltpu.VMEM((ng, KC, cols), F32),
            pltpu.VMEM((ng, WINDOW + TQ, cols), F32),
        ],
        compiler_params=_params(3),
        name="sparse_attention",
    )(q_t, gates_t, kc, vct, ovl_t, cend, edges, ks, vts, kw, vtw)


def _rope_tables(pos):
    inv_freq = ROPE_THETA ** (-jnp.arange(0, ROT_DIM, 2, dtype=F32) / ROT_DIM)
    ang = pos.astype(F32)[..., None] * inv_freq
    cs = jnp.concatenate([jnp.cos(ang), jnp.sin(ang)], axis=-1)
    half = ROT_DIM // 2
    place = np.zeros((3, ROT_DIM, LANES), np.float32)
    const = np.zeros((3, LANES), np.float32)
    for lane in range(LANES):
        d = lane % HEAD_DIM
        if d < ROT_DIM:
            place[0, d % half, lane] = 1.0
            if d < half:
                place[1, half + d, lane] = -1.0
            else:
                place[2, half + d - half, lane] = 1.0
        else:
            const[0, lane] = 1.0
    tabs = jnp.einsum('...k,tkn->t...n', cs, jnp.asarray(place), precision=lax.Precision.HIGHEST)
    return tuple(tabs[t] + const[t] for t in range(3))


def _overlap_t(n_chunk, n_slc):
    c_start = CMP_STRIDE * np.arange(n_chunk)
    blk = np.arange(n_slc)
    ovl = ((c_start[None] < (blk[:, None] + 1) * SLC_BLOCK)
           & (c_start[None] + CMP_LEN > blk[:, None] * SLC_BLOCK)
           & (np.arange(n_chunk)[None] < n_chunk - 1))
    return jnp.asarray(ovl.astype(np.float32))


def kernel(x, p, positions, norm_mix, pool_w, pool_scale, norm_kv, w_kv, cmp_pos_k, cmp_w1_k, cmp_w2_k, cmp_pos_v, cmp_w1_v, cmp_w2_v, w_in_b, w_out_b, norm_ffn, ffn_up, ffn_conv, ffn_conv_b, ffn_down, norm_ple, ple_gate, ple_proj, norm_final):
    B, S, D = x.shape
    n_chunk = S // CMP_STRIDE
    assert S // SLC_BLOCK == LANES - HEAD_DIM
    rc, ra, rb = _rope_tables(positions)
    end_idx = jnp.minimum(CMP_STRIDE * jnp.arange(n_chunk) + CMP_LEN - 1, S - 1)
    rc_c, ra_c, rb_c = _rope_tables(positions[:, end_idx])
    ovl_t = _overlap_t(n_chunk, S // SLC_BLOCK)

    stack = lambda g: g.reshape(g.shape[0], 1, g.shape[1])
    norm_mix3, norm_ffn3, norm_ple3, pool_scale3 = map(stack, (norm_mix, norm_ffn, norm_ple, pool_scale))
    conv_b3 = stack(ffn_conv_b)
    pool_w16, up16, down16, gate16, proj16, w_out16 = (
        w.astype(BF16) for w in (pool_w, ffn_up, ffn_down, ple_gate, ple_proj, w_out_b))
    w_q16 = _q_weights(w_in_b)

    kv = None
    for i in range(DEPTH):
        attn = None
        j = i - N_A
        if i >= N_A:
            if i == N_A:
                kc_raw, vc_raw, ks, vts, kw, vtw = _kv_project(x, norm_kv, w_kv, rc, ra, rb)
                kc, vct = _compress(kc_raw, vc_raw, cmp_pos_k, cmp_w1_k, cmp_w2_k,
                                    cmp_pos_v, cmp_w1_v, cmp_w2_v, rc_c, ra_c, rb_c)
                kv = (kc, vct, ovl_t, ks, vts, kw, vtw)
            q_t, gates_t = _q_project(x, i, norm_mix3, j, w_q16, rc, ra, rb)
            attn = _attention(q_t, gates_t, *kv)
        x = _ffn_layer(x, i, attn, w_out16, j, norm_mix3, pool_w16, pool_scale3, norm_ffn3, up16, ffn_conv, conv_b3,
                       down16, norm_ple3, gate16, p, proj16, norm_final if i == DEPTH - 1 else None)
    return x
```

```python
import functools
import math

import jax
import jax.numpy as jnp
import numpy as np
from jax import lax
from jax.experimental import pallas as pl
from jax.experimental.pallas import tpu as pltpu

D_MODEL = 1024
DEPTH = 4
N_A = DEPTH // 2
POOL_WINDOWS = (2, 4, 8, 16)
POOL_GROUP = D_MODEL // len(POOL_WINDOWS)
POOL_HALO = 16
N_HEADS = 16
HEAD_DIM = 64
N_KV_GROUPS = 4
HEADS_PER_GROUP = N_HEADS // N_KV_GROUPS
CMP_STRIDE = 16
CMP_LEN = 2 * CMP_STRIDE
CMP_HIDDEN = 128
SLC_BLOCK = 64
N_SELECT = 16
WINDOW = 512
N_BRANCH = 3
ROPE_THETA = 500000.0
ROT_DIM = HEAD_DIM // 4
D_FF = 2816
CONV_W = 3
PLE_DIM = 256
EPS = 1e-6
NEG = -1e30
FORCE_SCORE = 1e4

LANES = 128
SUBLANES = 8
VMEM_LIMIT = 48 * 1024 * 1024

TS = 512
FC = 256
FFN_UNROLL = 2
TQ = 256
KC = TQ
ATTN_GROUPS = 2
GATE_ROWS = 16

BF16 = jnp.bfloat16
F32 = jnp.float32


def _dot(a, b):
    return jnp.dot(a, b, preferred_element_type=F32)


def _rmsnorm(x, g):
    return x * lax.rsqrt(jnp.mean(x * x, axis=-1, keepdims=True) + EPS) * g


def _sigmoid(x):
    return 1.0 / (1.0 + jnp.exp(-x))


def _rope128(x, c, a, b):
    return x * c + pltpu.roll(x, LANES - ROT_DIM // 2, 1) * a + pltpu.roll(x, ROT_DIM // 2, 1) * b


def _params(n_grid):
    return pltpu.CompilerParams(dimension_semantics=("arbitrary",) * n_grid,
                                vmem_limit_bytes=VMEM_LIMIT)


def _pool_mix(h, history, s, w_ref, scale):
    ext = jnp.concatenate([history, h], axis=0)
    sums = {1: ext}
    w = 1
    while w < POOL_WINDOWS[-1]:
        prev = sums[w]
        sums[2 * w] = prev + pltpu.roll(prev, w, 0)
        w *= 2
    t = s * TS + lax.broadcasted_iota(jnp.int32, (TS, 1), 0)
    outs = []
    for gi, win in enumerate(POOL_WINDOWS):
        lo, hi = gi * POOL_GROUP, (gi + 1) * POOL_GROUP
        tot = sums[win][POOL_HALO:, lo:hi]
        cnt = jnp.minimum(t + 1, win).astype(F32)
        u = tot / cnt - h[:, lo:hi]
        outs.append(_dot(u.astype(BF16), w_ref[gi]))
    return jnp.concatenate(outs, axis=-1) * scale


def _ffn_kernel(*refs, has_attn, final_norm):
    it = iter(refs)
    x_ref = next(it)
    if has_attn:
        attn_ref, wout_ref = next(it), next(it)
    else:
        gmix_ref, poolw_ref, pscale_ref = next(it), next(it), next(it)
    gffn_ref, wup_ref, cw_ref, cb_ref, wdown_ref = (next(it) for _ in range(5))
    gple_ref, pgate_ref, p_ref, pproj_ref = (next(it) for _ in range(4))
    if final_norm:
        gfin_ref = next(it)
    o_ref = next(it)
    xres_ref, h_ref, acc_ref, carry_ref = (next(it) for _ in range(4))
    u_refs = (next(it), next(it))
    if not has_attn:
        hist_ref = next(it)

    s = pl.program_id(1)
    k = pl.program_id(2)
    n_k = pl.num_programs(2)

    @pl.when((s == 0) & (k == 0))
    def _():
        carry_ref[...] = jnp.zeros_like(carry_ref)
        if not has_attn:
            hist_ref[...] = jnp.zeros_like(hist_ref)

    @pl.when(k == 0)
    def _():
        xin = x_ref[...]
        if has_attn:
            xin = xin + _dot(attn_ref[...], wout_ref[...])
        else:
            hm = _rmsnorm(xin, gmix_ref[...])
            xin = xin + _pool_mix(hm, hist_ref[...], s, poolw_ref, pscale_ref[...])
            hist_ref[...] = hm[TS - POOL_HALO:, :]
        xres_ref[...] = xin
        h_ref[...] = _rmsnorm(xin, gffn_ref[...]).astype(BF16)
        acc_ref[...] = jnp.zeros_like(acc_ref)
        u_refs[1][...] = jnp.zeros_like(u_refs[1])

    def cols(chunk, gate):
        return pl.ds(pl.multiple_of(gate * D_FF + chunk * FC, FC), FC)

    def half_step(u_in_ref, u_out_ref, chunk, valid, chunk_up):
        for slot in range(2):
            u_out_ref[slot, SUBLANES:, :] = _dot(h_ref[...], wup_ref[:, cols(chunk_up, slot)])

        def conv_half(slot):
            u = u_in_ref[slot, SUBLANES:, :]
            last = carry_ref[chunk, slot]
            u_in_ref[slot, 0:SUBLANES, :] = last
            carry_ref[chunk, slot] = u[TS - SUBLANES:, :] if valid is None else jnp.where(
                valid, u[TS - SUBLANES:, :], last)
            cw = cw_ref[:, cols(chunk, slot)]
            out = cw[CONV_W - 1:CONV_W] * u + cb_ref[:, cols(chunk, slot)]
            for back in range(1, CONV_W):
                shifted = u_in_ref[slot, pl.ds(SUBLANES - back, TS), :]
                out = out + cw[CONV_W - 1 - back:CONV_W - back] * shifted
            return out

        a = conv_half(0)
        g = conv_half(1)
        act = g * _sigmoid(g) * a
        if valid is not None:
            act = jnp.where(valid, act, 0.0)
        acc_ref[...] += _dot(act.astype(BF16), wdown_ref[pl.ds(pl.multiple_of(chunk * FC, FC), FC), :])

    for r in range(FFN_UNROLL):
        half_step(u_refs[(r + 1) % 2], u_refs[r % 2], jnp.maximum(FFN_UNROLL * k + r - 1, 0),
                  (k > 0) if r == 0 else None, jnp.minimum(FFN_UNROLL * k + r, D_FF // FC - 1))

    @pl.when(k == n_k - 1)
    def _():
        x2 = xres_ref[...] + acc_ref[...]
        hp = _rmsnorm(x2, gple_ref[...]).astype(BF16)
        gate = _sigmoid(_dot(hp, pgate_ref[...]))
        emb = _dot(p_ref[...].astype(BF16), pproj_ref[...])
        x3 = x2 + gate * emb
        if final_norm:
            x3 = _rmsnorm(x3, gfin_ref[...])
        o_ref[...] = x3


def _ffn_layer(x, i, attn, w_out, j, g_mix, pool_w, pool_scale, g_ffn, w_up, conv_w, conv_b, w_down,
               g_ple, ple_gate, p, ple_proj, g_final):
    B, S, D = x.shape
    n_c = D_FF // FC
    assert (n_c + 1) % FFN_UNROLL == 0 and FFN_UNROLL % 2 == 0
    n_k = (n_c + 1) // FFN_UNROLL
    has_attn = attn is not None
    final_norm = g_final is not None
    tok = lambda b, s, k: (b, s, 0)
    layer = lambda b, s, k: (i, 0, 0)
    vec = pl.BlockSpec((None, 1, D), layer)

    def resident(shape, index_map):
        return pl.BlockSpec((None,) + shape, index_map, pipeline_mode=pl.Buffered(1))

    args, specs = [x], [pl.BlockSpec((None, TS, D), tok)]
    if has_attn:
        args += [attn, w_out]
        specs += [pl.BlockSpec((None, TS, D), tok), resident((D, D), lambda b, s, k: (j, 0, 0))]
    else:
        args += [g_mix, pool_w, pool_scale]
        specs += [vec, resident((len(POOL_WINDOWS), POOL_GROUP, POOL_GROUP), lambda b, s, k: (i, 0, 0, 0)), vec]
    args += [g_ffn, w_up, conv_w, conv_b, w_down, g_ple, ple_gate, p, ple_proj]
    specs += [
        vec,
        resident((D, 2 * D_FF), layer),
        resident((CONV_W, 2 * D_FF), layer),
        resident((1, 2 * D_FF), layer),
        resident((D_FF, D), layer),
        vec,
        resident((D, D), layer),
        pl.BlockSpec((None, None, TS, PLE_DIM), lambda b, s, k: (i, b, s, 0)),
        resident((PLE_DIM, D), layer),
    ]
    if final_norm:
        args.append(g_final.reshape(1, D))
        specs.append(pl.BlockSpec((1, D), lambda b, s, k: (0, 0)))
    return pl.pallas_call(
        functools.partial(_ffn_kernel, has_attn=has_attn, final_norm=final_norm),
        grid=(B, S // TS, n_k),
        in_specs=specs,
        out_specs=pl.BlockSpec((None, TS, D), tok),
        out_shape=jax.ShapeDtypeStruct(x.shape, F32),
        scratch_shapes=[
            pltpu.VMEM((TS, D), F32),
            pltpu.VMEM((TS, D), BF16),
            pltpu.VMEM((TS, D), F32),
            pltpu.VMEM((n_c, 2, SUBLANES, FC), F32),
            pltpu.VMEM((2, SUBLANES + TS, FC), F32),
            pltpu.VMEM((2, SUBLANES + TS, FC), F32),
        ] + ([] if has_attn else [pltpu.VMEM((POOL_HALO, D), F32)]),
        compiler_params=_params(3),
        name="ffn_embed",
    )(*args)


def _kv_kernel(x_ref, g_ref, w_ref, rc_ref, ra_ref, rb_ref,
               kc_ref, vc_ref, ks_ref, vts_ref, kw_ref, vtw_ref):
    s = pl.program_id(1)
    lane = lax.broadcasted_iota(jnp.int32, (TS, LANES), 1)

    @pl.when(s == 0)
    def _():
        flag = jnp.where(lane == HEAD_DIM, 1.0, 0.0).astype(BF16)
        for g in range(N_KV_GROUPS):
            kw_ref[0, g] = flag
            vtw_ref[0, g] = jnp.zeros((LANES, TS), BF16)

    @pl.when(s > 0)
    def _():
        h = _rmsnorm(x_ref[0], g_ref[...]).astype(BF16)
        kv = _dot(h, w_ref[...])
        gw = N_KV_GROUPS * HEAD_DIM
        rc, ra, rb = rc_ref[0], ra_ref[0], rb_ref[0]
        tok = (s - 1) * TS + lax.broadcasted_iota(jnp.int32, (TS, LANES), 0)
        onehot = jnp.where(lane - HEAD_DIM == tok // SLC_BLOCK, 1.0, 0.0)
        row = lax.broadcasted_iota(jnp.int32, (LANES - HEAD_DIM, TS), 0)
        ones_row = jnp.where(row == 0, 1.0, 0.0).astype(BF16)

        def slabs(i):
            return [kv[:, i * gw + j * LANES:i * gw + (j + 1) * LANES] for j in range(gw // LANES)]

        for g in range(N_KV_GROUPS):
            kc_ref[0, g] = kv[:, 0 * gw + g * HEAD_DIM:0 * gw + (g + 1) * HEAD_DIM]
            vc_ref[0, g] = kv[:, 1 * gw + g * HEAD_DIM:1 * gw + (g + 1) * HEAD_DIM]

        for i, k_ref, vt_ref in ((2, ks_ref, vts_ref), (4, kw_ref, vtw_ref)):
            for j, (kslab, vslab) in enumerate(zip(slabs(i), slabs(i + 1))):
                kr = _rope128(kslab, rc, ra, rb)
                vt = vslab.T
                for half in range(2):
                    g = 2 * j + half
                    kh = kr if half == 0 else pltpu.roll(kr, HEAD_DIM, 1)
                    upper = onehot if k_ref is ks_ref else 0.0
                    k_ref[0, g] = jnp.where(lane < HEAD_DIM, kh, upper).astype(BF16)
                    vt_ref[0, g, 0:HEAD_DIM, :] = vt[half * HEAD_DIM:(half + 1) * HEAD_DIM, :].astype(BF16)
                    vt_ref[0, g, HEAD_DIM:, :] = ones_row


def _kv_project(x, g, w_kv, rc, ra, rb):
    B, S, D = x.shape
    G = N_KV_GROUPS
    assert WINDOW == TS
    tile = lambda s: jnp.maximum(s - 1, 0)
    tok = lambda b, s: (b, tile(s), 0)
    rows = lambda b, s: (b, 0, tile(s), 0)
    cols = lambda b, s: (b, 0, 0, tile(s))
    return pl.pallas_call(
        _kv_kernel,
        grid=(B, S // TS + 1),
        in_specs=[
            pl.BlockSpec((1, TS, D), tok),
            pl.BlockSpec((1, D), lambda b, s: (0, 0)),
            pl.BlockSpec(w_kv.shape, lambda b, s: (0, 0)),
            pl.BlockSpec((1, TS, LANES), tok),
            pl.BlockSpec((1, TS, LANES), tok),
            pl.BlockSpec((1, TS, LANES), tok),
        ],
        out_specs=[
            pl.BlockSpec((1, G, TS, HEAD_DIM), rows),
            pl.BlockSpec((1, G, TS, HEAD_DIM), rows),
            pl.BlockSpec((1, G, TS, LANES), rows),
            pl.BlockSpec((1, G, LANES, TS), cols),
            pl.BlockSpec((1, G, TS, LANES), lambda b, s: (b, 0, s, 0)),
            pl.BlockSpec((1, G, LANES, TS), lambda b, s: (b, 0, 0, s)),
        ],
        out_shape=[
            jax.ShapeDtypeStruct((B, G, S, HEAD_DIM), F32),
            jax.ShapeDtypeStruct((B, G, S, HEAD_DIM), F32),
            jax.ShapeDtypeStruct((B, G, S, LANES), BF16),
            jax.ShapeDtypeStruct((B, G, LANES, S), BF16),
            jax.ShapeDtypeStruct((B, G, WINDOW + S, LANES), BF16),
            jax.ShapeDtypeStruct((B, G, LANES, WINDOW + S), BF16),
        ],
        compiler_params=_params(2),
        name="kv_project",
    )(x, g.reshape(1, D), w_kv.astype(BF16), rc, ra, rb)


def _gelu_tanh(x):
    return 0.5 * x * (1.0 + jnp.tanh(math.sqrt(2.0 / math.pi) * (x + 0.044715 * (x * x * x))))


def _compress_kernel(k_ref, v_ref, pk_ref, w1k_ref, w2k_ref, pv_ref, w1v_ref, w2v_ref,
                     rc_ref, ra_ref, rb_ref, kc_ref, vct_ref):
    n_chunk = kc_ref.shape[2]

    def hidden(x_ref, pos_ref, w1_ref):
        first = jnp.zeros((n_chunk, CMP_HIDDEN), F32)
        second = jnp.zeros((n_chunk, CMP_HIDDEN), F32)
        for r in range(CMP_STRIDE):
            tok = x_ref[0, 0, pl.ds(r, n_chunk, stride=CMP_STRIDE), :]
            lead = (tok + pos_ref[r:r + 1, :]).astype(BF16)
            trail = (tok + pos_ref[CMP_STRIDE + r:CMP_STRIDE + r + 1, :]).astype(BF16)
            first = first + _dot(lead, w1_ref[r * HEAD_DIM:(r + 1) * HEAD_DIM, :])
            second = second + _dot(trail, w1_ref[(CMP_STRIDE + r) * HEAD_DIM:(CMP_STRIDE + r + 1) * HEAD_DIM, :])
        hid = first + pltpu.roll(second, n_chunk - 1, 0)
        return _gelu_tanh(hid).astype(BF16)

    kc = _dot(hidden(k_ref, pk_ref, w1k_ref), w2k_ref[...])
    kc_ref[0, 0] = _rope128(kc, rc_ref[0], ra_ref[0], rb_ref[0])[:, :HEAD_DIM].astype(BF16)
    vct = lax.dot_general(w2v_ref[...], hidden(v_ref, pv_ref, w1v_ref), (((1,), (1,)), ((), ())),
                          preferred_element_type=F32)
    vct_ref[0, 0] = vct[:HEAD_DIM, :].astype(BF16)


def _compress(k_raw, v_raw, pos_k, w1_k, w2_k, pos_v, w1_v, w2_v, rc, ra, rb):
    B, G, S, _ = k_raw.shape
    n_chunk = S // CMP_STRIDE
    blk = lambda b, g: (b, g, 0, 0)
    const2 = lambda b, g: (0, 0)
    pad2 = lambda w: jnp.pad(w, ((0, 0), (0, LANES - HEAD_DIM))).astype(BF16)
    return pl.pallas_call(
        _compress_kernel,
        grid=(B, G),
        in_specs=[
            pl.BlockSpec((1, 1, S, HEAD_DIM), blk),
            pl.BlockSpec((1, 1, S, HEAD_DIM), blk),
            pl.BlockSpec((CMP_LEN, HEAD_DIM), const2),
            pl.BlockSpec((CMP_LEN * HEAD_DIM, CMP_HIDDEN), const2),
            pl.BlockSpec((CMP_HIDDEN, LANES), const2),
            pl.BlockSpec((CMP_LEN, HEAD_DIM), const2),
            pl.BlockSpec((CMP_LEN * HEAD_DIM, CMP_HIDDEN), const2),
            pl.BlockSpec((CMP_HIDDEN, LANES), const2),
            pl.BlockSpec((1, n_chunk, LANES), lambda b, g: (b, 0, 0)),
            pl.BlockSpec((1, n_chunk, LANES), lambda b, g: (b, 0, 0)),
            pl.BlockSpec((1, n_chunk, LANES), lambda b, g: (b, 0, 0)),
        ],
        out_specs=[
            pl.BlockSpec((1, 1, n_chunk, HEAD_DIM), blk),
            pl.BlockSpec((1, 1, HEAD_DIM, n_chunk), blk),
        ],
        out_shape=[
            jax.ShapeDtypeStruct((B, G, n_chunk, HEAD_DIM), BF16),
            jax.ShapeDtypeStruct((B, G, HEAD_DIM, n_chunk), BF16),
        ],
        compiler_params=_params(2),
        name="compress",
    )(k_raw, v_raw, pos_k, w1_k.astype(BF16), pad2(w2_k), pos_v, w1_v.astype(BF16), pad2(w2_v).T, rc, ra, rb)


def _qproj_kernel(x_ref, g_ref, w_ref, rc_ref, ra_ref, rb_ref, q_ref, gate_ref):
    h = _rmsnorm(x_ref[0], g_ref[...]).astype(BF16)
    proj = _dot(h, w_ref[...])
    rc, ra, rb = rc_ref[0], ra_ref[0], rb_ref[0]
    scale = HEAD_DIM ** -0.5 * math.log2(math.e)
    for j in range(N_HEADS // 2):
        qt = (_rope128(proj[:, j * LANES:(j + 1) * LANES], rc, ra, rb) * scale).T
        q_ref[0, 2 * j] = qt[:HEAD_DIM].astype(BF16)
        q_ref[0, 2 * j + 1] = qt[HEAD_DIM:].astype(BF16)
    n_q = N_HEADS * HEAD_DIM
    for g in range(N_KV_GROUPS):
        gt = _sigmoid(proj[:, n_q + g * LANES:n_q + (g + 1) * LANES]).T
        gate_ref[0, g] = gt[:gate_ref.shape[2]]


def _q_weights(w_in_b):
    L, D, _ = w_in_b.shape
    n_q = N_HEADS * HEAD_DIM
    per_group = HEADS_PER_GROUP * N_BRANCH
    gate_w = w_in_b[:, :, n_q:].reshape(L, D, N_KV_GROUPS, per_group)
    gate_w = jnp.pad(gate_w, ((0, 0), (0, 0), (0, 0), (0, LANES - per_group))).reshape(L, D, N_KV_GROUPS * LANES)
    return jnp.concatenate([w_in_b[:, :, :n_q], gate_w], axis=2).astype(BF16)


def _q_project(x, i, g, j, w, rc, ra, rb):
    B, S, D = x.shape
    tok = lambda b, s: (b, s, 0)
    return pl.pallas_call(
        _qproj_kernel,
        grid=(B, S // TS),
        in_specs=[
            pl.BlockSpec((1, TS, D), tok),
            pl.BlockSpec((None, 1, D), lambda b, s: (i, 0, 0)),
            pl.BlockSpec((None,) + w.shape[1:], lambda b, s: (j, 0, 0)),
            pl.BlockSpec((1, TS, LANES), tok),
            pl.BlockSpec((1, TS, LANES), tok),
            pl.BlockSpec((1, TS, LANES), tok),
        ],
        out_specs=[
            pl.BlockSpec((1, N_HEADS, HEAD_DIM, TS), lambda b, s: (b, 0, 0, s)),
            pl.BlockSpec((1, N_KV_GROUPS, GATE_ROWS, TS), lambda b, s: (b, 0, 0, s)),
        ],
        out_shape=[
            jax.ShapeDtypeStruct((B, N_HEADS, HEAD_DIM, S), BF16),
            jax.ShapeDtypeStruct((B, N_KV_GROUPS, GATE_ROWS, S), F32),
        ],
        compiler_params=_params(2),
        name="q_project",
    )(x, g, w, rc, ra, rb)


def _attn_kernel(q_ref, gate_ref, kc_ref, vct_ref, ovl_ref, cend_ref, edge_ref, ks_ref, vts_ref, kw_ref, vtw_ref,
                 o_ref, m_ref, acc_ref, sa_ref, sb_ref, sw_ref):
    qi = pl.program_id(2)
    s0 = qi * TQ
    hpg = HEADS_PER_GROUP
    n_slc = ovl_ref.shape[0]
    span = WINDOW + TQ
    w0 = pl.multiple_of(s0, TQ)
    groups = range(ATTN_GROUPS)

    flag_row = lax.broadcasted_iota(jnp.int32, (LANES - HEAD_DIM, hpg * TQ), 0) == 0
    flag_rows = jnp.where(flag_row, NEG, 0.0).astype(BF16)
    qt, sc = [], []
    for g in groups:
        qt.append(jnp.concatenate([q_ref[0, g * hpg + hh] for hh in range(hpg)], axis=1))
        sc.append(_dot(kc_ref[0, g], qt[g]))
        sw_ref[g] = _dot(kw_ref[0, g, pl.ds(w0, span), :], jnp.concatenate([qt[g], flag_rows], axis=0))

    mask_c = cend_ref[...] <= s0
    oc_t, imp_t = [], []
    for g in groups:
        s_c = jnp.where(mask_c, sc[g], NEG)
        e = jnp.where(mask_c, jnp.exp2(s_c - jnp.max(s_c, axis=0, keepdims=True)), 0.0)
        p_c = e * (1.0 / jnp.maximum(jnp.sum(e, axis=0, keepdims=True), 1e-30))
        oc_t.append(_dot(vct_ref[0, g], p_c.astype(BF16)))
        p_sum = p_c[:, 0:TQ]
        for hh in range(1, hpg):
            p_sum = p_sum + p_c[:, hh * TQ:(hh + 1) * TQ]
        imp_t.append(jnp.dot(ovl_ref[...], p_sum, precision=lax.Precision.HIGHEST,
                             preferred_element_type=F32))

    blk = lax.broadcasted_iota(jnp.int32, (n_slc, TQ), 0)
    t_col = s0 + lax.broadcasted_iota(jnp.int32, (n_slc, TQ), 1)
    cur = t_col // SLC_BLOCK
    forced = (blk == 0) | (blk == cur) | (blk == cur - 1)
    valid = blk * SLC_BLOCK <= t_col
    blk_in = lax.broadcasted_iota(jnp.int32, (SUBLANES, TQ), 0)
    qt_sel = []
    for g in groups:
        score = jnp.where(valid, jnp.where(forced, FORCE_SCORE, imp_t[g]), NEG)
        rows = [score[r:r + SUBLANES] for r in range(0, n_slc, SUBLANES)]
        ranks = [jnp.zeros((SUBLANES, TQ), F32) for _ in rows]
        for m in range(n_slc):
            other = score[m:m + 1, :]
            for gi, grp in enumerate(rows):
                lo = gi * SUBLANES
                if lo > m:
                    ahead = other >= grp
                elif lo + SUBLANES - 1 <= m:
                    ahead = other > grp
                else:
                    ahead = (other > grp) | ((other == grp) & (blk_in + lo > m))
                ranks[gi] = ranks[gi] + jnp.where(ahead, 1.0, 0.0)
        rank = jnp.concatenate(ranks, axis=0)
        bias_t = jnp.where(valid & (rank < min(N_SELECT, n_slc)), 0.0, NEG).astype(BF16)
        qt_sel.append(jnp.concatenate([qt[g], jnp.concatenate([bias_t] * hpg, axis=1)], axis=0))

    def scores(j, out_ref):
        for g in groups:
            out_ref[g] = _dot(ks_ref[0, g, pl.ds(pl.multiple_of(j * KC, KC), KC), :], qt_sel[g])

    def absorb(in_ref, j, diagonal=False):
        for g in groups:
            s = in_ref[g]
            if diagonal:
                s = s + edge_ref[0]
            m_old = m_ref[g]
            m_new = jnp.maximum(m_old, jnp.max(s, axis=0, keepdims=True))
            p = jnp.exp2(s - m_new).astype(BF16)
            v_t = vts_ref[0, g, :, pl.ds(pl.multiple_of(j * KC, KC), KC)]
            acc_ref[g] = jnp.exp2(m_old - m_new) * acc_ref[g] + _dot(v_t, p)
            m_ref[g] = m_new

    m_ref[...] = jnp.full_like(m_ref, NEG)
    acc_ref[...] = jnp.zeros_like(acc_ref)
    scores(0, sa_ref)

    ow_t = []
    for g in groups:
        lo_edge = sw_ref[g, 0:TQ] + edge_ref[1]
        middle = sw_ref[g, TQ:WINDOW]
        hi_edge = sw_ref[g, WINDOW:] + edge_ref[0]
        m_w = jnp.maximum(jnp.maximum(jnp.max(lo_edge, axis=0, keepdims=True),
                                      jnp.max(middle, axis=0, keepdims=True)),
                          jnp.max(hi_edge, axis=0, keepdims=True))
        rw = (_dot(vtw_ref[0, g, :, pl.ds(w0, TQ)], jnp.exp2(lo_edge - m_w).astype(BF16))
              + _dot(vtw_ref[0, g, :, pl.ds(w0 + TQ, WINDOW - TQ)], jnp.exp2(middle - m_w).astype(BF16))
              + _dot(vtw_ref[0, g, :, pl.ds(w0 + WINDOW, TQ)], jnp.exp2(hi_edge - m_w).astype(BF16)))
        ow_t.append(rw[:HEAD_DIM] * (1.0 / jnp.maximum(rw[HEAD_DIM:HEAD_DIM + 1], 1e-30)))

    def pair(i, carry):
        scores(2 * i + 1, sb_ref)
        absorb(sa_ref, 2 * i)
        scores(2 * i + 2, sa_ref)
        absorb(sb_ref, 2 * i + 1)
        return carry

    lax.fori_loop(0, qi // 2, pair, 0)

    @pl.when(qi % 2 == 0)
    def _():
        absorb(sa_ref, qi, diagonal=True)

    @pl.when(qi % 2 == 1)
    def _():
        scores(qi, sb_ref)
        absorb(sa_ref, qi - 1)
        absorb(sb_ref, qi, diagonal=True)

    outs = []
    for g in groups:
        acc = acc_ref[g]
        os_t = acc[:HEAD_DIM] * (1.0 / acc[HEAD_DIM:HEAD_DIM + 1])
        gates = gate_ref[0, g]
        for hh in range(hpg):
            c = slice(hh * TQ, (hh + 1) * TQ)
            r = hh * N_BRANCH
            outs.append(gates[r:r + 1] * oc_t[g][:, c] + gates[r + 1:r + 2] * os_t[:, c]
                        + gates[r + 2:r + 3] * ow_t[g][:, c])
    o_ref[0] = jnp.concatenate(outs, axis=0).T.astype(o_ref.dtype)


def _attention(q_t, gates_t, kc, vct, ovl_t, ks, vts, kw, vtw):
    B, _, _, S = q_t.shape
    G = N_KV_GROUPS
    hpg = HEADS_PER_GROUP
    ng = ATTN_GROUPS
    assert KC == TQ and WINDOW % TQ == 0 and G % ng == 0
    cols = hpg * TQ
    key = np.arange(TQ)[:, None]
    t_local = (np.arange(cols) % TQ)[None, :]
    cend = jnp.asarray(CMP_STRIDE * np.arange(kc.shape[2])[:, None] + CMP_LEN - 1 - t_local, jnp.int32)
    edges = jnp.asarray(np.where(np.stack([key <= t_local, key > t_local]), 0.0, NEG), F32)
    per_bg = lambda b, g, qi: (b, g, 0, 0)
    return pl.pallas_call(
        _attn_kernel,
        grid=(B, G // ng, S // TQ),
        in_specs=[
            pl.BlockSpec((1, ng * hpg, HEAD_DIM, TQ), lambda b, g, qi: (b, g, 0, qi)),
            pl.BlockSpec((1, ng, GATE_ROWS, TQ), lambda b, g, qi: (b, g, 0, qi)),
            pl.BlockSpec((1, ng) + kc.shape[2:], per_bg),
            pl.BlockSpec((1, ng) + vct.shape[2:], per_bg),
            pl.BlockSpec(ovl_t.shape, lambda b, g, qi: (0, 0)),
            pl.BlockSpec(cend.shape, lambda b, g, qi: (0, 0)),
            pl.BlockSpec(edges.shape, lambda b, g, qi: (0, 0, 0)),
            pl.BlockSpec((1, ng) + ks.shape[2:], per_bg),
            pl.BlockSpec((1, ng) + vts.shape[2:], per_bg),
            pl.BlockSpec((1, ng) + kw.shape[2:], per_bg),
            pl.BlockSpec((1, ng) + vtw.shape[2:], per_bg),
        ],
        out_specs=pl.BlockSpec((1, TQ, ng * hpg * HEAD_DIM), lambda b, g, qi: (b, qi, g)),
        out_shape=jax.ShapeDtypeStruct((B, S, N_HEADS * HEAD_DIM), BF16),
        scratch_shapes=[
            pltpu.VMEM((ng, 1, cols), F32),
            pltpu.VMEM((ng, LANES, cols), F32),
            pltpu.VMEM((ng, KC, cols), F32),
            pltpu.VMEM((ng, KC, cols), F32),
            pltpu.VMEM((ng, WINDOW + TQ, cols), F32),
        ],
        compiler_params=_params(3),
        name="sparse_attention",
    )(q_t, gates_t, kc, vct, ovl_t, cend, edges, ks, vts, kw, vtw)


def _rope_tables(pos):
    inv_freq = ROPE_THETA ** (-jnp.arange(0, ROT_DIM, 2, dtype=F32) / ROT_DIM)
    ang = pos.astype(F32)[..., None] * inv_freq
    cs = jnp.concatenate([jnp.cos(ang), jnp.sin(ang)], axis=-1)
    half = ROT_DIM // 2
    place = np.zeros((3, ROT_DIM, LANES), np.float32)
    const = np.zeros((3, LANES), np.float32)
    for lane in range(LANES):
        d = lane % HEAD_DIM
        if d < ROT_DIM:
            place[0, d % half, lane] = 1.0
            if d < half:
                place[1, half + d, lane] = -1.0
            else:
                place[2, half + d - half, lane] = 1.0
        else:
            const[0, lane] = 1.0
    tabs = jnp.einsum('...k,tkn->t...n', cs, jnp.asarray(place), precision=lax.Precision.HIGHEST)
    return tuple(tabs[t] + const[t] for t in range(3))


def _overlap_t(n_chunk, n_slc):
    c_start = CMP_STRIDE * np.arange(n_chunk)
    blk = np.arange(n_slc)
    ovl = ((c_start[None] < (blk[:, None] + 1) * SLC_BLOCK)
           & (c_start[None] + CMP_LEN > blk[:, None] * SLC_BLOCK)
           & (np.arange(n_chunk)[None] < n_chunk - 1))
    return jnp.asarray(ovl.astype(np.float32))


def kernel(x, p, positions, norm_mix, pool_w, pool_scale, norm_kv, w_kv, cmp_pos_k, cmp_w1_k, cmp_w2_k, cmp_pos_v, cmp_w1_v, cmp_w2_v, w_in_b, w_out_b, norm_ffn, ffn_up, ffn_conv, ffn_conv_b, ffn_down, norm_ple, ple_gate, ple_proj, norm_final):
    B, S, D = x.shape
    n_chunk = S // CMP_STRIDE
    assert S // SLC_BLOCK == LANES - HEAD_DIM
    rc, ra, rb = _rope_tables(positions)
    end_idx = jnp.minimum(CMP_STRIDE * jnp.arange(n_chunk) + CMP_LEN - 1, S - 1)
    rc_c, ra_c, rb_c = _rope_tables(positions[:, end_idx])
    ovl_t = _overlap_t(n_chunk, S // SLC_BLOCK)

    stack = lambda g: g.reshape(g.shape[0], 1, g.shape[1])
    norm_mix3, norm_ffn3, norm_ple3, pool_scale3 = map(stack, (norm_mix, norm_ffn, norm_ple, pool_scale))
    conv_b3 = stack(ffn_conv_b)
    pool_w16, up16, down16, gate16, proj16, w_out16 = (
        w.astype(BF16) for w in (pool_w, ffn_up, ffn_down, ple_gate, ple_proj, w_out_b))
    w_q16 = _q_weights(w_in_b)

    kv = None
    for i in range(DEPTH):
        attn = None
        j = i - N_A
        if i >= N_A:
            if i == N_A:
                kc_raw, vc_raw, ks, vts, kw, vtw = _kv_project(x, norm_kv, w_kv, rc, ra, rb)
                kc, vct = _compress(kc_raw, vc_raw, cmp_pos_k, cmp_w1_k, cmp_w2_k,
                                    cmp_pos_v, cmp_w1_v, cmp_w2_v, rc_c, ra_c, rb_c)
                kv = (kc, vct, ovl_t, ks, vts, kw, vtw)
            q_t, gates_t = _q_project(x, i, norm_mix3, j, w_q16, rc, ra, rb)
            attn = _attention(q_t, gates_t, *kv)
        x = _ffn_layer(x, i, attn, w_out16, j, norm_mix3, pool_w16, pool_scale3, norm_ffn3, up16, ffn_conv, conv_b3,
                       down16, norm_ple3, gate16, p, proj16, norm_final if i == DEPTH - 1 else None)
    return x
```

```python
import functools
import math

import jax
import jax.numpy as jnp
import numpy as np
from jax import lax
from jax.experimental import pallas as pl
from jax.experimental.pallas import tpu as pltpu

D_MODEL = 1024
DEPTH = 4
N_A = DEPTH // 2
POOL_WINDOWS = (2, 4, 8, 16)
POOL_GROUP = D_MODEL // len(POOL_WINDOWS)
POOL_HALO = 16
N_HEADS = 16
HEAD_DIM = 64
N_KV_GROUPS = 4
HEADS_PER_GROUP = N_HEADS // N_KV_GROUPS
CMP_STRIDE = 16
CMP_LEN = 2 * CMP_STRIDE
CMP_HIDDEN = 128
SLC_BLOCK = 64
N_SELECT = 16
WINDOW = 512
N_BRANCH = 3
ROPE_THETA = 500000.0
ROT_DIM = HEAD_DIM // 4
D_FF = 2816
CONV_W = 3
PLE_DIM = 256
EPS = 1e-6
NEG = -1e30
FORCE_SCORE = 1e4

LANES = 128
SUBLANES = 8
VMEM_LIMIT = 48 * 1024 * 1024

TS = 512
FC = 256
FFN_UNROLL = 2
TQ = 256
KC = TQ
ATTN_GROUPS = 2
GATE_ROWS = 16

BF16 = jnp.bfloat16
F32 = jnp.float32


def _dot(a, b):
    return jnp.dot(a, b, preferred_element_type=F32)


def _rmsnorm(x, g):
    return x * lax.rsqrt(jnp.mean(x * x, axis=-1, keepdims=True) + EPS) * g


def _sigmoid(x):
    return 1.0 / (1.0 + jnp.exp(-x))


def _rope128(x, c, a, b):
    return x * c + pltpu.roll(x, LANES - ROT_DIM // 2, 1) * a + pltpu.roll(x, ROT_DIM // 2, 1) * b


def _params(n_grid):
    return pltpu.CompilerParams(dimension_semantics=("arbitrary",) * n_grid,
                                vmem_limit_bytes=VMEM_LIMIT)


def _pool_mix(h, history, s, w_ref, scale):
    ext = jnp.concatenate([history, h], axis=0)
    sums = {1: ext}
    w = 1
    while w < POOL_WINDOWS[-1]:
        prev = sums[w]
        sums[2 * w] = prev + pltpu.roll(prev, w, 0)
        w *= 2
    t = s * TS + lax.broadcasted_iota(jnp.int32, (TS, 1), 0)
    outs = []
    for gi, win in enumerate(POOL_WINDOWS):
        lo, hi = gi * POOL_GROUP, (gi + 1) * POOL_GROUP
        tot = sums[win][POOL_HALO:, lo:hi]
        cnt = jnp.minimum(t + 1, win).astype(F32)
        u = tot / cnt - h[:, lo:hi]
        outs.append(_dot(u.astype(BF16), w_ref[gi]))
    return jnp.concatenate(outs, axis=-1) * scale


def _ffn_kernel(*refs, has_attn, final_norm):
    it = iter(refs)
    x_ref = next(it)
    if has_attn:
        attn_ref, wout_ref = next(it), next(it)
    else:
        gmix_ref, poolw_ref, pscale_ref = next(it), next(it), next(it)
    gffn_ref, wup_ref, cw_ref, cb_ref, wdown_ref = (next(it) for _ in range(5))
    gple_ref, pgate_ref, p_ref, pproj_ref = (next(it) for _ in range(4))
    if final_norm:
        gfin_ref = next(it)
    o_ref = next(it)
    xres_ref, h_ref, acc_ref, carry_ref = (next(it) for _ in range(4))
    u_refs = (next(it), next(it))
    if not has_attn:
        hist_ref = next(it)

    s = pl.program_id(1)
    k = pl.program_id(2)
    n_k = pl.num_programs(2)

    @pl.when((s == 0) & (k == 0))
    def _():
        carry_ref[...] = jnp.zeros_like(carry_ref)
        if not has_attn:
            hist_ref[...] = jnp.zeros_like(hist_ref)

    @pl.when(k == 0)
    def _():
        xin = x_ref[...]
        if has_attn:
            xin = xin + _dot(attn_ref[...], wout_ref[...])
        else:
            hm = _rmsnorm(xin, gmix_ref[...])
            xin = xin + _pool_mix(hm, hist_ref[...], s, poolw_ref, pscale_ref[...])
            hist_ref[...] = hm[TS - POOL_HALO:, :]
        xres_ref[...] = xin
        h_ref[...] = _rmsnorm(xin, gffn_ref[...]).astype(BF16)
        acc_ref[...] = jnp.zeros_like(acc_ref)
        u_refs[1][...] = jnp.zeros_like(u_refs[1])

    def cols(chunk, gate):
        return pl.ds(pl.multiple_of(gate * D_FF + chunk * FC, FC), FC)

    def half_step(u_in_ref, u_out_ref, chunk, valid, chunk_up):
        for slot in range(2):
            u_out_ref[slot, SUBLANES:, :] = _dot(h_ref[...], wup_ref[:, cols(chunk_up, slot)])

        def conv_half(slot):
            u = u_in_ref[slot, SUBLANES:, :]
            last = carry_ref[chunk, slot]
            u_in_ref[slot, 0:SUBLANES, :] = last
            carry_ref[chunk, slot] = u[TS - SUBLANES:, :] if valid is None else jnp.where(
                valid, u[TS - SUBLANES:, :], last)
            cw = cw_ref[:, cols(chunk, slot)]
            out = cw[CONV_W - 1:CONV_W] * u + cb_ref[:, cols(chunk, slot)]
            for back in range(1, CONV_W):
                shifted = u_in_ref[slot, pl.ds(SUBLANES - back, TS), :]
                out = out + cw[CONV_W - 1 - back:CONV_W - back] * shifted
            return out

        a = conv_half(0)
        g = conv_half(1)
        act = g * _sigmoid(g) * a
        if valid is not None:
            act = jnp.where(valid, act, 0.0)
        acc_ref[...] += _dot(act.astype(BF16), wdown_ref[pl.ds(pl.multiple_of(chunk * FC, FC), FC), :])

    for r in range(FFN_UNROLL):
        half_step(u_refs[(r + 1) % 2], u_refs[r % 2], jnp.maximum(FFN_UNROLL * k + r - 1, 0),
                  (k > 0) if r == 0 else None, jnp.minimum(FFN_UNROLL * k + r, D_FF // FC - 1))

    @pl.when(k == n_k - 1)
    def _():
        x2 = xres_ref[...] + acc_ref[...]
        hp = _rmsnorm(x2, gple_ref[...]).astype(BF16)
        gate = _sigmoid(_dot(hp, pgate_ref[...]))
        emb = _dot(p_ref[...].astype(BF16), pproj_ref[...])
        x3 = x2 + gate * emb
        if final_norm:
            x3 = _rmsnorm(x3, gfin_ref[...])
        o_ref[...] = x3


def _ffn_layer(x, i, attn, w_out, j, g_mix, pool_w, pool_scale, g_ffn, w_up, conv_w, conv_b, w_down,
               g_ple, ple_gate, p, ple_proj, g_final):
    B, S, D = x.shape
    n_c = D_FF // FC
    assert (n_c + 1) % FFN_UNROLL == 0 and FFN_UNROLL % 2 == 0
    n_k = (n_c + 1) // FFN_UNROLL
    has_attn = attn is not None
    final_norm = g_final is not None
    tok = lambda b, s, k: (b, s, 0)
    layer = lambda b, s, k: (i, 0, 0)
    vec = pl.BlockSpec((None, 1, D), layer)

    def resident(shape, index_map):
        return pl.BlockSpec((None,) + shape, index_map, pipeline_mode=pl.Buffered(1))

    args, specs = [x], [pl.BlockSpec((None, TS, D), tok)]
    if has_attn:
        args += [attn, w_out]
        specs += [pl.BlockSpec((None, TS, D), tok), resident((D, D), lambda b, s, k: (j, 0, 0))]
    else:
        args += [g_mix, pool_w, pool_scale]
        specs += [vec, resident((len(POOL_WINDOWS), POOL_GROUP, POOL_GROUP), lambda b, s, k: (i, 0, 0, 0)), vec]
    args += [g_ffn, w_up, conv_w, conv_b, w_down, g_ple, ple_gate, p, ple_proj]
    specs += [
        vec,
        resident((D, 2 * D_FF), layer),
        resident((CONV_W, 2 * D_FF), layer),
        resident((1, 2 * D_FF), layer),
        resident((D_FF, D), layer),
        vec,
        resident((D, D), layer),
        pl.BlockSpec((None, None, TS, PLE_DIM), lambda b, s, k: (i, b, s, 0)),
        resident((PLE_DIM, D), layer),
    ]
    if final_norm:
        args.append(g_final.reshape(1, D))
        specs.append(pl.BlockSpec((1, D), lambda b, s, k: (0, 0)))
    return pl.pallas_call(
        functools.partial(_ffn_kernel, has_attn=has_attn, final_norm=final_norm),
        grid=(B, S // TS, n_k),
        in_specs=specs,
        out_specs=pl.BlockSpec((None, TS, D), tok),
        out_shape=jax.ShapeDtypeStruct(x.shape, F32),
        scratch_shapes=[
            pltpu.VMEM((TS, D), F32),
            pltpu.VMEM((TS, D), BF16),
            pltpu.VMEM((TS, D), F32),
            pltpu.VMEM((n_c, 2, SUBLANES, FC), F32),
            pltpu.VMEM((2, SUBLANES + TS, FC), F32),
            pltpu.VMEM((2, SUBLANES + TS, FC), F32),
        ] + ([] if has_attn else [pltpu.VMEM((POOL_HALO, D), F32)]),
        compiler_params=_params(3),
        name="ffn_embed",
    )(*args)


def _kv_kernel(x_ref, g_ref, w_ref, rc_ref, ra_ref, rb_ref,
               kc_ref, vc_ref, ks_ref, vts_ref, kw_ref, vtw_ref):
    s = pl.program_id(1)
    lane = lax.broadcasted_iota(jnp.int32, (TS, LANES), 1)

    @pl.when(s == 0)
    def _():
        flag = jnp.where(lane == HEAD_DIM, 1.0, 0.0).astype(BF16)
        for g in range(N_KV_GROUPS):
            kw_ref[0, g] = flag
            vtw_ref[0, g] = jnp.zeros((LANES, TS), BF16)

    @pl.when(s > 0)
    def _():
        h = _rmsnorm(x_ref[0], g_ref[...]).astype(BF16)
        kv = _dot(h, w_ref[...])
        gw = N_KV_GROUPS * HEAD_DIM
        rc, ra, rb = rc_ref[0], ra_ref[0], rb_ref[0]
        tok = (s - 1) * TS + lax.broadcasted_iota(jnp.int32, (TS, LANES), 0)
        onehot = jnp.where(lane - HEAD_DIM == tok // SLC_BLOCK, 1.0, 0.0)
        row = lax.broadcasted_iota(jnp.int32, (LANES - HEAD_DIM, TS), 0)
        ones_row = jnp.where(row == 0, 1.0, 0.0).astype(BF16)

        def slabs(i):
            return [kv[:, i * gw + j * LANES:i * gw + (j + 1) * LANES] for j in range(gw // LANES)]

        for g in range(N_KV_GROUPS):
            kc_ref[0, g] = kv[:, 0 * gw + g * HEAD_DIM:0 * gw + (g + 1) * HEAD_DIM]
            vc_ref[0, g] = kv[:, 1 * gw + g * HEAD_DIM:1 * gw + (g + 1) * HEAD_DIM]

        for i, k_ref, vt_ref in ((2, ks_ref, vts_ref), (4, kw_ref, vtw_ref)):
            for j, (kslab, vslab) in enumerate(zip(slabs(i), slabs(i + 1))):
                kr = _rope128(kslab, rc, ra, rb)
                vt = vslab.T
                for half in range(2):
                    g = 2 * j + half
                    kh = kr if half == 0 else pltpu.roll(kr, HEAD_DIM, 1)
                    upper = onehot if k_ref is ks_ref else 0.0
                    k_ref[0, g] = jnp.where(lane < HEAD_DIM, kh, upper).astype(BF16)
                    vt_ref[0, g, 0:HEAD_DIM, :] = vt[half * HEAD_DIM:(half + 1) * HEAD_DIM, :].astype(BF16)
                    vt_ref[0, g, HEAD_DIM:, :] = ones_row


def _kv_project(x, g, w_kv, rc, ra, rb):
    B, S, D = x.shape
    G = N_KV_GROUPS
    assert WINDOW == TS
    tile = lambda s: jnp.maximum(s - 1, 0)
    tok = lambda b, s: (b, tile(s), 0)
    rows = lambda b, s: (b, 0, tile(s), 0)
    cols = lambda b, s: (b, 0, 0, tile(s))
    return pl.pallas_call(
        _kv_kernel,
        grid=(B, S // TS + 1),
        in_specs=[
            pl.BlockSpec((1, TS, D), tok),
            pl.BlockSpec((1, D), lambda b, s: (0, 0)),
            pl.BlockSpec(w_kv.shape, lambda b, s: (0, 0)),
            pl.BlockSpec((1, TS, LANES), tok),
            pl.BlockSpec((1, TS, LANES), tok),
            pl.BlockSpec((1, TS, LANES), tok),
        ],
        out_specs=[
            pl.BlockSpec((1, G, TS, HEAD_DIM), rows),
            pl.BlockSpec((1, G, TS, HEAD_DIM), rows),
            pl.BlockSpec((1, G, TS, LANES), rows),
            pl.BlockSpec((1, G, LANES, TS), cols),
            pl.BlockSpec((1, G, TS, LANES), lambda b, s: (b, 0, s, 0)),
            pl.BlockSpec((1, G, LANES, TS), lambda b, s: (b, 0, 0, s)),
        ],
        out_shape=[
            jax.ShapeDtypeStruct((B, G, S, HEAD_DIM), F32),
            jax.ShapeDtypeStruct((B, G, S, HEAD_DIM), F32),
            jax.ShapeDtypeStruct((B, G, S, LANES), BF16),
            jax.ShapeDtypeStruct((B, G, LANES, S), BF16),
            jax.ShapeDtypeStruct((B, G, WINDOW + S, LANES), BF16),
            jax.ShapeDtypeStruct((B, G, LANES, WINDOW + S), BF16),
        ],
        compiler_params=_params(2),
        name="kv_project",
    )(x, g.reshape(1, D), w_kv.astype(BF16), rc, ra, rb)


def _gelu_tanh(x):
    return 0.5 * x * (1.0 + jnp.tanh(math.sqrt(2.0 / math.pi) * (x + 0.044715 * (x * x * x))))


def _compress_kernel(k_ref, v_ref, pk_ref, w1k_ref, w2k_ref, pv_ref, w1v_ref, w2v_ref,
                     rc_ref, ra_ref, rb_ref, kc_ref, vct_ref):
    n_chunk = kc_ref.shape[2]

    def hidden(x_ref, pos_ref, w1_ref):
        first = jnp.zeros((n_chunk, CMP_HIDDEN), F32)
        second = jnp.zeros((n_chunk, CMP_HIDDEN), F32)
        for r in range(CMP_STRIDE):
            tok = x_ref[0, 0, pl.ds(r, n_chunk, stride=CMP_STRIDE), :]
            lead = (tok + pos_ref[r:r + 1, :]).astype(BF16)
            trail = (tok + pos_ref[CMP_STRIDE + r:CMP_STRIDE + r + 1, :]).astype(BF16)
            first = first + _dot(lead, w1_ref[r * HEAD_DIM:(r + 1) * HEAD_DIM, :])
            second = second + _dot(trail, w1_ref[(CMP_STRIDE + r) * HEAD_DIM:(CMP_STRIDE + r + 1) * HEAD_DIM, :])
        hid = first + pltpu.roll(second, n_chunk - 1, 0)
        return _gelu_tanh(hid).astype(BF16)

    kc = _dot(hidden(k_ref, pk_ref, w1k_ref), w2k_ref[...])
    kc_ref[0, 0] = _rope128(kc, rc_ref[0], ra_ref[0], rb_ref[0])[:, :HEAD_DIM].astype(BF16)
    vct = lax.dot_general(w2v_ref[...], hidden(v_ref, pv_ref, w1v_ref), (((1,), (1,)), ((), ())),
                          preferred_element_type=F32)
    vct_ref[0, 0] = vct[:HEAD_DIM, :].astype(BF16)


def _compress(k_raw, v_raw, pos_k, w1_k, w2_k, pos_v, w1_v, w2_v, rc, ra, rb):
    B, G, S, _ = k_raw.shape
    n_chunk = S // CMP_STRIDE
    blk = lambda b, g: (b, g, 0, 0)
    const2 = lambda b, g: (0, 0)
    pad2 = lambda w: jnp.pad(w, ((0, 0), (0, LANES - HEAD_DIM))).astype(BF16)
    return pl.pallas_call(
        _compress_kernel,
        grid=(B, G),
        in_specs=[
            pl.BlockSpec((1, 1, S, HEAD_DIM), blk),
            pl.BlockSpec((1, 1, S, HEAD_DIM), blk),
            pl.BlockSpec((CMP_LEN, HEAD_DIM), const2),
            pl.BlockSpec((CMP_LEN * HEAD_DIM, CMP_HIDDEN), const2),
            pl.BlockSpec((CMP_HIDDEN, LANES), const2),
            pl.BlockSpec((CMP_LEN, HEAD_DIM), const2),
            pl.BlockSpec((CMP_LEN * HEAD_DIM, CMP_HIDDEN), const2),
            pl.BlockSpec((CMP_HIDDEN, LANES), const2),
            pl.BlockSpec((1, n_chunk, LANES), lambda b, g: (b, 0, 0)),
            pl.BlockSpec((1, n_chunk, LANES), lambda b, g: (b, 0, 0)),
            pl.BlockSpec((1, n_chunk, LANES), lambda b, g: (b, 0, 0)),
        ],
        out_specs=[
            pl.BlockSpec((1, 1, n_chunk, HEAD_DIM), blk),
            pl.BlockSpec((1, 1, HEAD_DIM, n_chunk), blk),
        ],
        out_shape=[
            jax.ShapeDtypeStruct((B, G, n_chunk, HEAD_DIM), BF16),
            jax.ShapeDtypeStruct((B, G, HEAD_DIM, n_chunk), BF16),
        ],
        compiler_params=_params(2),
        name="compress",
    )(k_raw, v_raw, pos_k, w1_k.astype(BF16), pad2(w2_k), pos_v, w1_v.astype(BF16), pad2(w2_v).T, rc, ra, rb)


def _qproj_kernel(x_ref, g_ref, w_ref, rc_ref, ra_ref, rb_ref, q_ref, gate_ref):
    h = _rmsnorm(x_ref[0], g_ref[...]).astype(BF16)
    proj = _dot(h, w_ref[...])
    rc, ra, rb = rc_ref[0], ra_ref[0], rb_ref[0]
    scale = HEAD_DIM ** -0.5 * math.log2(math.e)
    for j in range(N_HEADS // 2):
        qt = (_rope128(proj[:, j * LANES:(j + 1) * LANES], rc, ra, rb) * scale).T
        q_ref[0, 2 * j] = qt[:HEAD_DIM].astype(BF16)
        q_ref[0, 2 * j + 1] = qt[HEAD_DIM:].astype(BF16)
    n_q = N_HEADS * HEAD_DIM
    for g in range(N_KV_GROUPS):
        gt = _sigmoid(proj[:, n_q + g * LANES:n_q + (g + 1) * LANES]).T
        gate_ref[0, g] = gt[:gate_ref.shape[2]]


def _q_weights(w_in_b):
    L, D, _ = w_in_b.shape
    n_q = N_HEADS * HEAD_DIM
    per_group = HEADS_PER_GROUP * N_BRANCH
    gate_w = w_in_b[:, :, n_q:].reshape(L, D, N_KV_GROUPS, per_group)
    gate_w = jnp.pad(gate_w, ((0, 0), (0, 0), (0, 0), (0, LANES - per_group))).reshape(L, D, N_KV_GROUPS * LANES)
    return jnp.concatenate([w_in_b[:, :, :n_q], gate_w], axis=2).astype(BF16)


def _q_project(x, i, g, j, w, rc, ra, rb):
    B, S, D = x.shape
    tok = lambda b, s: (b, s, 0)
    return pl.pallas_call(
        _qproj_kernel,
        grid=(B, S // TS),
        in_specs=[
            pl.BlockSpec((1, TS, D), tok),
            pl.BlockSpec((None, 1, D), lambda b, s: (i, 0, 0)),
            pl.BlockSpec((None,) + w.shape[1:], lambda b, s: (j, 0, 0)),
            pl.BlockSpec((1, TS, LANES), tok),
            pl.BlockSpec((1, TS, LANES), tok),
            pl.BlockSpec((1, TS, LANES), tok),
        ],
        out_specs=[
            pl.BlockSpec((1, N_HEADS, HEAD_DIM, TS), lambda b, s: (b, 0, 0, s)),
            pl.BlockSpec((1, N_KV_GROUPS, GATE_ROWS, TS), lambda b, s: (b, 0, 0, s)),
        ],
        out_shape=[
            jax.ShapeDtypeStruct((B, N_HEADS, HEAD_DIM, S), BF16),
            jax.ShapeDtypeStruct((B, N_KV_GROUPS, GATE_ROWS, S), F32),
        ],
        compiler_params=_params(2),
        name="q_project",
    )(x, g, w, rc, ra, rb)


def _attn_kernel(q_ref, gate_ref, kc_ref, vct_ref, ovl_ref, cend_ref, edge_ref, ks_ref, vts_ref, kw_ref, vtw_ref,
                 o_ref, m_ref, acc_ref, sa_ref, sb_ref, sw_ref):
    qi = pl.program_id(2)
    s0 = qi * TQ
    hpg = HEADS_PER_GROUP
    n_slc = ovl_ref.shape[0]
    span = WINDOW + TQ
    w0 = pl.multiple_of(s0, TQ)
    groups = range(ATTN_GROUPS)

    flag_row = lax.broadcasted_iota(jnp.int32, (LANES - HEAD_DIM, hpg * TQ), 0) == 0
    flag_rows = jnp.where(flag_row, NEG, 0.0).astype(BF16)
    qt, sc = [], []
    for g in groups:
        qt.append(jnp.concatenate([q_ref[0, g * hpg + hh] for hh in range(hpg)], axis=1))
        sc.append(_dot(kc_ref[0, g], qt[g]))
        sw_ref[g] = _dot(kw_ref[0, g, pl.ds(w0, span), :], jnp.concatenate([qt[g], flag_rows], axis=0))

    mask_c = cend_ref[...] <= s0
    has_block = cend_ref[0:1, :] <= s0
    oc_t, imp_t = [], []
    for g in groups:
        s_c = jnp.where(mask_c, sc[g], NEG)
        e = jnp.exp2(s_c - jnp.max(s_c, axis=0, keepdims=True))
        p_c = e * jnp.where(has_block, 1.0 / jnp.maximum(jnp.sum(e, axis=0, keepdims=True), 1e-30), 0.0)
        oc_t.append(_dot(vct_ref[0, g], p_c.astype(BF16)))
        p_sum = p_c[:, 0:TQ]
        for hh in range(1, hpg):
            p_sum = p_sum + p_c[:, hh * TQ:(hh + 1) * TQ]
        imp_t.append(jnp.dot(ovl_ref[...], p_sum, precision=lax.Precision.HIGHEST,
                             preferred_element_type=F32))

    blk = lax.broadcasted_iota(jnp.int32, (n_slc, TQ), 0)
    t_col = s0 + lax.broadcasted_iota(jnp.int32, (n_slc, TQ), 1)
    cur = t_col // SLC_BLOCK
    forced = (blk == 0) | (blk == cur) | (blk == cur - 1)
    valid = blk * SLC_BLOCK <= t_col
    blk_in = lax.broadcasted_iota(jnp.int32, (SUBLANES, TQ), 0)
    qt_sel = []
    for g in groups:
        score = jnp.where(valid, jnp.where(forced, FORCE_SCORE, imp_t[g]), NEG)
        rows = [score[r:r + SUBLANES] for r in range(0, n_slc, SUBLANES)]
        ranks = [jnp.zeros((SUBLANES, TQ), F32) for _ in rows]
        for m in range(n_slc):
            other = score[m:m + 1, :]
            for gi, grp in enumerate(rows):
                lo = gi * SUBLANES
                if lo > m:
                    ahead = other >= grp
                elif lo + SUBLANES - 1 <= m:
                    ahead = other > grp
                else:
                    ahead = (other > grp) | ((other == grp) & (blk_in + lo > m))
                ranks[gi] = ranks[gi] + jnp.where(ahead, 1.0, 0.0)
        rank = jnp.concatenate(ranks, axis=0)
        bias_t = jnp.where(valid & (rank < min(N_SELECT, n_slc)), 0.0, NEG).astype(BF16)
        qt_sel.append(jnp.concatenate([qt[g], jnp.concatenate([bias_t] * hpg, axis=1)], axis=0))

    def scores(j, out_ref):
        for g in groups:
            out_ref[g] = _dot(ks_ref[0, g, pl.ds(pl.multiple_of(j * KC, KC), KC), :], qt_sel[g])

    def absorb(in_ref, j, diagonal=False):
        for g in groups:
            s = in_ref[g]
            if diagonal:
                s = s + edge_ref[0]
            m_old = m_ref[g]
            m_new = jnp.maximum(m_old, jnp.max(s, axis=0, keepdims=True))
            p = jnp.exp2(s - m_new).astype(BF16)
            v_t = vts_ref[0, g, :, pl.ds(pl.multiple_of(j * KC, KC), KC)]
            acc_ref[g] = jnp.exp2(m_old - m_new) * acc_ref[g] + _dot(v_t, p)
            m_ref[g] = m_new

    m_ref[...] = jnp.full_like(m_ref, NEG)
    acc_ref[...] = jnp.zeros_like(acc_ref)
    scores(0, sa_ref)

    ow_t = []
    for g in groups:
        lo_edge = sw_ref[g, 0:TQ] + edge_ref[1]
        middle = sw_ref[g, TQ:WINDOW]
        hi_edge = sw_ref[g, WINDOW:] + edge_ref[0]
        m_w = jnp.maximum(jnp.maximum(jnp.max(lo_edge, axis=0, keepdims=True),
                                      jnp.max(middle, axis=0, keepdims=True)),
                          jnp.max(hi_edge, axis=0, keepdims=True))
        rw = (_dot(vtw_ref[0, g, :, pl.ds(w0, TQ)], jnp.exp2(lo_edge - m_w).astype(BF16))
              + _dot(vtw_ref[0, g, :, pl.ds(w0 + TQ, WINDOW - TQ)], jnp.exp2(middle - m_w).astype(BF16))
              + _dot(vtw_ref[0, g, :, pl.ds(w0 + WINDOW, TQ)], jnp.exp2(hi_edge - m_w).astype(BF16)))
        ow_t.append(rw[:HEAD_DIM] * (1.0 / jnp.maximum(rw[HEAD_DIM:HEAD_DIM + 1], 1e-30)))

    def pair(i, carry):
        scores(2 * i + 1, sb_ref)
        absorb(sa_ref, 2 * i)
        scores(2 * i + 2, sa_ref)
        absorb(sb_ref, 2 * i + 1)
        return carry

    lax.fori_loop(0, qi // 2, pair, 0)

    @pl.when(qi % 2 == 0)
    def _():
        absorb(sa_ref, qi, diagonal=True)

    @pl.when(qi % 2 == 1)
    def _():
        scores(qi, sb_ref)
        absorb(sa_ref, qi - 1)
        absorb(sb_ref, qi, diagonal=True)

    outs = []
    for g in groups:
        acc = acc_ref[g]
        os_t = acc[:HEAD_DIM] * (1.0 / acc[HEAD_DIM:HEAD_DIM + 1])
        gates = gate_ref[0, g]
        for hh in range(hpg):
            c = slice(hh * TQ, (hh + 1) * TQ)
            r = hh * N_BRANCH
            outs.append(gates[r:r + 1] * oc_t[g][:, c] + gates[r + 1:r + 2] * os_t[:, c]
                        + gates[r + 2:r + 3] * ow_t[g][:, c])
    o_ref[0] = jnp.concatenate(outs, axis=0).T.astype(o_ref.dtype)


def _attention(q_t, gates_t, kc, vct, ovl_t, ks, vts, kw, vtw):
    B, _, _, S = q_t.shape
    G = N_KV_GROUPS
    hpg = HEADS_PER_GROUP
    ng = ATTN_GROUPS
    assert KC == TQ and WINDOW % TQ == 0 and G % ng == 0
    cols = hpg * TQ
    key = np.arange(TQ)[:, None]
    t_local = (np.arange(cols) % TQ)[None, :]
    cend = jnp.asarray(CMP_STRIDE * np.arange(kc.shape[2])[:, None] + CMP_LEN - 1 - t_local, jnp.int32)
    edges = jnp.asarray(np.where(np.stack([key <= t_local, key > t_local]), 0.0, NEG), F32)
    per_bg = lambda b, g, qi: (b, g, 0, 0)
    return pl.pallas_call(
        _attn_kernel,
        grid=(B, G // ng, S // TQ),
        in_specs=[
            pl.BlockSpec((1, ng * hpg, HEAD_DIM, TQ), lambda b, g, qi: (b, g, 0, qi)),
            pl.BlockSpec((1, ng, GATE_ROWS, TQ), lambda b, g, qi: (b, g, 0, qi)),
            pl.BlockSpec((1, ng) + kc.shape[2:], per_bg),
            pl.BlockSpec((1, ng) + vct.shape[2:], per_bg),
            pl.BlockSpec(ovl_t.shape, lambda b, g, qi: (0, 0)),
            pl.BlockSpec(cend.shape, lambda b, g, qi: (0, 0)),
            pl.BlockSpec(edges.shape, lambda b, g, qi: (0, 0, 0)),
            pl.BlockSpec((1, ng) + ks.shape[2:], per_bg),
            pl.BlockSpec((1, ng) + vts.shape[2:], per_bg),
            pl.BlockSpec((1, ng) + kw.shape[2:], per_bg),
            pl.BlockSpec((1, ng) + vtw.shape[2:], per_bg),
        ],
        out_specs=pl.BlockSpec((1, TQ, ng * hpg * HEAD_DIM), lambda b, g, qi: (b, qi, g)),
        out_shape=jax.ShapeDtypeStruct((B, S, N_HEADS * HEAD_DIM), BF16),
        scratch_shapes=[
            pltpu.VMEM((ng, 1, cols), F32),
            pltpu.VMEM((ng, LANES, cols), F32),
            pltpu.VMEM((ng, KC, cols), F32),
            pltpu.VMEM((ng, KC, cols), F32),
            pltpu.VMEM((ng, WINDOW + TQ, cols), F32),
        ],
        compiler_params=_params(3),
        name="sparse_attention",
    )(q_t, gates_t, kc, vct, ovl_t, cend, edges, ks, vts, kw, vtw)


def _rope_tables(pos):
    inv_freq = ROPE_THETA ** (-jnp.arange(0, ROT_DIM, 2, dtype=F32) / ROT_DIM)
    ang = pos.astype(F32)[..., None] * inv_freq
    cs = jnp.concatenate([jnp.cos(ang), jnp.sin(ang)], axis=-1)
    half = ROT_DIM // 2
    place = np.zeros((3, ROT_DIM, LANES), np.float32)
    const = np.zeros((3, LANES), np.float32)
    for lane in range(LANES):
        d = lane % HEAD_DIM
        if d < ROT_DIM:
            place[0, d % half, lane] = 1.0
            if d < half:
                place[1, half + d, lane] = -1.0
            else:
                place[2, half + d - half, lane] = 1.0
        else:
            const[0, lane] = 1.0
    tabs = jnp.einsum('...k,tkn->t...n', cs, jnp.asarray(place), precision=lax.Precision.HIGHEST)
    return tuple(tabs[t] + const[t] for t in range(3))


def _overlap_t(n_chunk, n_slc):
    c_start = CMP_STRIDE * np.arange(n_chunk)
    blk = np.arange(n_slc)
    ovl = ((c_start[None] < (blk[:, None] + 1) * SLC_BLOCK)
           & (c_start[None] + CMP_LEN > blk[:, None] * SLC_BLOCK)
           & (np.arange(n_chunk)[None] < n_chunk - 1))
    return jnp.asarray(ovl.astype(np.float32))


def kernel(x, p, positions, norm_mix, pool_w, pool_scale, norm_kv, w_kv, cmp_pos_k, cmp_w1_k, cmp_w2_k, cmp_pos_v, cmp_w1_v, cmp_w2_v, w_in_b, w_out_b, norm_ffn, ffn_up, ffn_conv, ffn_conv_b, ffn_down, norm_ple, ple_gate, ple_proj, norm_final):
    B, S, D = x.shape
    n_chunk = S // CMP_STRIDE
    assert S // SLC_BLOCK == LANES - HEAD_DIM
    rc, ra, rb = _rope_tables(positions)
    end_idx = jnp.minimum(CMP_STRIDE * jnp.arange(n_chunk) + CMP_LEN - 1, S - 1)
    rc_c, ra_c, rb_c = _rope_tables(positions[:, end_idx])
    ovl_t = _overlap_t(n_chunk, S // SLC_BLOCK)

    stack = lambda g: g.reshape(g.shape[0], 1, g.shape[1])
    norm_mix3, norm_ffn3, norm_ple3, pool_scale3 = map(stack, (norm_mix, norm_ffn, norm_ple, pool_scale))
    conv_b3 = stack(ffn_conv_b)
    pool_w16, up16, down16, gate16, proj16, w_out16 = (
        w.astype(BF16) for w in (pool_w, ffn_up, ffn_down, ple_gate, ple_proj, w_out_b))
    w_q16 = _q_weights(w_in_b)

    kv = None
    for i in range(DEPTH):
        attn = None
        j = i - N_A
        if i >= N_A:
            if i == N_A:
                kc_raw, vc_raw, ks, vts, kw, vtw = _kv_project(x, norm_kv, w_kv, rc, ra, rb)
                kc, vct = _compress(kc_raw, vc_raw, cmp_pos_k, cmp_w1_k, cmp_w2_k,
                                    cmp_pos_v, cmp_w1_v, cmp_w2_v, rc_c, ra_c, rb_c)
                kv = (kc, vct, ovl_t, ks, vts, kw, vtw)
            q_t, gates_t = _q_project(x, i, norm_mix3, j, w_q16, rc, ra, rb)
            attn = _attention(q_t, gates_t, *kv)
        x = _ffn_layer(x, i, attn, w_out16, j, norm_mix3, pool_w16, pool_scale3, norm_ffn3, up16, ffn_conv, conv_b3,
                       down16, norm_ple3, gate16, p, proj16, norm_final if i == DEPTH - 1 else None)
    return x
```

```python
import functools
import math

import jax
import jax.numpy as jnp
import numpy as np
from jax import lax
from jax.experimental import pallas as pl
from jax.experimental.pallas import tpu as pltpu

D_MODEL = 1024
DEPTH = 4
N_A = DEPTH // 2
POOL_WINDOWS = (2, 4, 8, 16)
POOL_GROUP = D_MODEL // len(POOL_WINDOWS)
POOL_HALO = 16
N_HEADS = 16
HEAD_DIM = 64
N_KV_GROUPS = 4
HEADS_PER_GROUP = N_HEADS // N_KV_GROUPS
CMP_STRIDE = 16
CMP_LEN = 2 * CMP_STRIDE
CMP_HIDDEN = 128
SLC_BLOCK = 64
N_SELECT = 16
WINDOW = 512
N_BRANCH = 3
ROPE_THETA = 500000.0
ROT_DIM = HEAD_DIM // 4
D_FF = 2816
CONV_W = 3
PLE_DIM = 256
EPS = 1e-6
NEG = -1e30
FORCE_SCORE = 1e4

LANES = 128
SUBLANES = 8
VMEM_LIMIT = 48 * 1024 * 1024

TS = 512
FC = 256
FFN_UNROLL = 4
TQ = 256
KC = TQ
ATTN_GROUPS = 2
GATE_ROWS = 16

BF16 = jnp.bfloat16
F32 = jnp.float32


def _dot(a, b):
    return jnp.dot(a, b, preferred_element_type=F32)


def _rmsnorm(x, g):
    return x * lax.rsqrt(jnp.mean(x * x, axis=-1, keepdims=True) + EPS) * g


def _sigmoid(x):
    return 1.0 / (1.0 + jnp.exp(-x))


def _rope128(x, c, a, b):
    return x * c + pltpu.roll(x, LANES - ROT_DIM // 2, 1) * a + pltpu.roll(x, ROT_DIM // 2, 1) * b


def _params(n_grid):
    return pltpu.CompilerParams(dimension_semantics=("arbitrary",) * n_grid,
                                vmem_limit_bytes=VMEM_LIMIT)


def _pool_mix(h, history, s, w_ref, scale):
    ext = jnp.concatenate([history, h], axis=0)
    sums = {1: ext}
    w = 1
    while w < POOL_WINDOWS[-1]:
        prev = sums[w]
        sums[2 * w] = prev + pltpu.roll(prev, w, 0)
        w *= 2
    t = s * TS + lax.broadcasted_iota(jnp.int32, (TS, 1), 0)
    outs = []
    for gi, win in enumerate(POOL_WINDOWS):
        lo, hi = gi * POOL_GROUP, (gi + 1) * POOL_GROUP
        tot = sums[win][POOL_HALO:, lo:hi]
        cnt = jnp.minimum(t + 1, win).astype(F32)
        u = tot / cnt - h[:, lo:hi]
        outs.append(_dot(u.astype(BF16), w_ref[gi]))
    return jnp.concatenate(outs, axis=-1) * scale


def _ffn_kernel(*refs, has_attn, final_norm):
    it = iter(refs)
    x_ref = next(it)
    if has_attn:
        attn_ref, wout_ref = next(it), next(it)
    else:
        gmix_ref, poolw_ref, pscale_ref = next(it), next(it), next(it)
    gffn_ref, wup_ref, cw_ref, cb_ref, wdown_ref = (next(it) for _ in range(5))
    gple_ref, pgate_ref, p_ref, pproj_ref = (next(it) for _ in range(4))
    if final_norm:
        gfin_ref = next(it)
    o_ref = next(it)
    xres_ref, h_ref, acc_ref, carry_ref = (next(it) for _ in range(4))
    u_refs = (next(it), next(it))
    emb_ref = next(it)
    if not has_attn:
        hist_ref = next(it)

    s = pl.program_id(1)
    k = pl.program_id(2)
    n_k = pl.num_programs(2)

    @pl.when((s == 0) & (k == 0))
    def _():
        carry_ref[...] = jnp.zeros_like(carry_ref)
        if not has_attn:
            hist_ref[...] = jnp.zeros_like(hist_ref)

    @pl.when(k == 0)
    def _():
        xin = x_ref[...]
        if has_attn:
            xin = xin + _dot(attn_ref[...], wout_ref[...])
        else:
            hm = _rmsnorm(xin, gmix_ref[...])
            xin = xin + _pool_mix(hm, hist_ref[...], s, poolw_ref, pscale_ref[...])
            hist_ref[...] = hm[TS - POOL_HALO:, :]
        xres_ref[...] = xin
        emb_ref[...] = _dot(p_ref[...].astype(BF16), pproj_ref[...])
        h_ref[...] = _rmsnorm(xin, gffn_ref[...]).astype(BF16)
        acc_ref[...] = jnp.zeros_like(acc_ref)
        u_refs[1][...] = jnp.zeros_like(u_refs[1])

    def cols(chunk, gate):
        return pl.ds(pl.multiple_of(gate * D_FF + chunk * FC, FC), FC)

    def half_step(u_in_ref, u_out_ref, chunk, valid, chunk_up):
        for slot in range(2):
            u_out_ref[slot, SUBLANES:, :] = _dot(h_ref[...], wup_ref[:, cols(chunk_up, slot)])

        def conv_half(slot):
            u = u_in_ref[slot, SUBLANES:, :]
            last = carry_ref[chunk, slot]
            u_in_ref[slot, 0:SUBLANES, :] = last
            carry_ref[chunk, slot] = u[TS - SUBLANES:, :] if valid is None else jnp.where(
                valid, u[TS - SUBLANES:, :], last)
            cw = cw_ref[:, cols(chunk, slot)]
            out = cw[CONV_W - 1:CONV_W] * u + cb_ref[:, cols(chunk, slot)]
            for back in range(1, CONV_W):
                shifted = u_in_ref[slot, pl.ds(SUBLANES - back, TS), :]
                out = out + cw[CONV_W - 1 - back:CONV_W - back] * shifted
            return out

        a = conv_half(0)
        g = conv_half(1)
        act = g * _sigmoid(g) * a
        if valid is not None:
            act = jnp.where(valid, act, 0.0)
        acc_ref[...] += _dot(act.astype(BF16), wdown_ref[pl.ds(pl.multiple_of(chunk * FC, FC), FC), :])

    for r in range(FFN_UNROLL):
        half_step(u_refs[(r + 1) % 2], u_refs[r % 2], jnp.maximum(FFN_UNROLL * k + r - 1, 0),
                  (k > 0) if r == 0 else None, jnp.minimum(FFN_UNROLL * k + r, D_FF // FC - 1))

    @pl.when(k == n_k - 1)
    def _():
        x2 = xres_ref[...] + acc_ref[...]
        hp = _rmsnorm(x2, gple_ref[...]).astype(BF16)
        gate = _sigmoid(_dot(hp, pgate_ref[...]))
        x3 = x2 + gate * emb_ref[...]
        if final_norm:
            x3 = _rmsnorm(x3, gfin_ref[...])
        o_ref[...] = x3


def _ffn_layer(x, i, attn, w_out, j, g_mix, pool_w, pool_scale, g_ffn, w_up, conv_w, conv_b, w_down,
               g_ple, ple_gate, p, ple_proj, g_final):
    B, S, D = x.shape
    n_c = D_FF // FC
    assert (n_c + 1) % FFN_UNROLL == 0 and FFN_UNROLL % 2 == 0
    n_k = (n_c + 1) // FFN_UNROLL
    has_attn = attn is not None
    final_norm = g_final is not None
    tok = lambda b, s, k: (b, s, 0)
    layer = lambda b, s, k: (i, 0, 0)
    vec = pl.BlockSpec((None, 1, D), layer)

    def resident(shape, index_map):
        return pl.BlockSpec((None,) + shape, index_map, pipeline_mode=pl.Buffered(1))

    args, specs = [x], [pl.BlockSpec((None, TS, D), tok)]
    if has_attn:
        args += [attn, w_out]
        specs += [pl.BlockSpec((None, TS, D), tok), resident((D, D), lambda b, s, k: (j, 0, 0))]
    else:
        args += [g_mix, pool_w, pool_scale]
        specs += [vec, resident((len(POOL_WINDOWS), POOL_GROUP, POOL_GROUP), lambda b, s, k: (i, 0, 0, 0)), vec]
    args += [g_ffn, w_up, conv_w, conv_b, w_down, g_ple, ple_gate, p, ple_proj]
    specs += [
        vec,
        resident((D, 2 * D_FF), layer),
        resident((CONV_W, 2 * D_FF), layer),
        resident((1, 2 * D_FF), layer),
        resident((D_FF, D), layer),
        vec,
        resident((D, D), layer),
        pl.BlockSpec((None, None, TS, PLE_DIM), lambda b, s, k: (i, b, s, 0)),
        resident((PLE_DIM, D), layer),
    ]
    if final_norm:
        args.append(g_final.reshape(1, D))
        specs.append(pl.BlockSpec((1, D), lambda b, s, k: (0, 0)))
    return pl.pallas_call(
        functools.partial(_ffn_kernel, has_attn=has_attn, final_norm=final_norm),
        grid=(B, S // TS, n_k),
        in_specs=specs,
        out_specs=pl.BlockSpec((None, TS, D), tok),
        out_shape=jax.ShapeDtypeStruct(x.shape, F32),
        scratch_shapes=[
            pltpu.VMEM((TS, D), F32),
            pltpu.VMEM((TS, D), BF16),
            pltpu.VMEM((TS, D), F32),
            pltpu.VMEM((n_c, 2, SUBLANES, FC), F32),
            pltpu.VMEM((2, SUBLANES + TS, FC), F32),
            pltpu.VMEM((2, SUBLANES + TS, FC), F32),
            pltpu.VMEM((TS, D), F32),
        ] + ([] if has_attn else [pltpu.VMEM((POOL_HALO, D), F32)]),
        compiler_params=_params(3),
        name="ffn_embed",
    )(*args)


def _kv_kernel(x_ref, g_ref, w_ref, rc_ref, ra_ref, rb_ref,
               kc_ref, vc_ref, ks_ref, vts_ref, kw_ref, vtw_ref):
    s = pl.program_id(1)
    lane = lax.broadcasted_iota(jnp.int32, (TS, LANES), 1)

    @pl.when(s == 0)
    def _():
        flag = jnp.where(lane == HEAD_DIM, 1.0, 0.0).astype(BF16)
        for g in range(N_KV_GROUPS):
            kw_ref[0, g] = flag
            vtw_ref[0, g] = jnp.zeros((LANES, TS), BF16)

    @pl.when(s > 0)
    def _():
        h = _rmsnorm(x_ref[0], g_ref[...]).astype(BF16)
        kv = _dot(h, w_ref[...])
        gw = N_KV_GROUPS * HEAD_DIM
        rc, ra, rb = rc_ref[0], ra_ref[0], rb_ref[0]
        tok = (s - 1) * TS + lax.broadcasted_iota(jnp.int32, (TS, LANES), 0)
        onehot = jnp.where(lane - HEAD_DIM == tok // SLC_BLOCK, 1.0, 0.0)
        row = lax.broadcasted_iota(jnp.int32, (LANES - HEAD_DIM, TS), 0)
        ones_row = jnp.where(row == 0, 1.0, 0.0).astype(BF16)

        def slabs(i):
            return [kv[:, i * gw + j * LANES:i * gw + (j + 1) * LANES] for j in range(gw // LANES)]

        for g in range(N_KV_GROUPS):
            kc_ref[0, g] = kv[:, 0 * gw + g * HEAD_DIM:0 * gw + (g + 1) * HEAD_DIM]
            vc_ref[0, g] = kv[:, 1 * gw + g * HEAD_DIM:1 * gw + (g + 1) * HEAD_DIM]

        for i, k_ref, vt_ref in ((2, ks_ref, vts_ref), (4, kw_ref, vtw_ref)):
            for j, (kslab, vslab) in enumerate(zip(slabs(i), slabs(i + 1))):
                kr = _rope128(kslab, rc, ra, rb)
                vt = vslab.T
                for half in range(2):
                    g = 2 * j + half
                    kh = kr if half == 0 else pltpu.roll(kr, HEAD_DIM, 1)
                    upper = onehot if k_ref is ks_ref else 0.0
                    k_ref[0, g] = jnp.where(lane < HEAD_DIM, kh, upper).astype(BF16)
                    vt_ref[0, g, 0:HEAD_DIM, :] = vt[half * HEAD_DIM:(half + 1) * HEAD_DIM, :].astype(BF16)
                    vt_ref[0, g, HEAD_DIM:, :] = ones_row


def _kv_project(x, g, w_kv, rc, ra, rb):
    B, S, D = x.shape
    G = N_KV_GROUPS
    assert WINDOW == TS
    tile = lambda s: jnp.maximum(s - 1, 0)
    tok = lambda b, s: (b, tile(s), 0)
    rows = lambda b, s: (b, 0, tile(s), 0)
    cols = lambda b, s: (b, 0, 0, tile(s))
    return pl.pallas_call(
        _kv_kernel,
        grid=(B, S // TS + 1),
        in_specs=[
            pl.BlockSpec((1, TS, D), tok),
            pl.BlockSpec((1, D), lambda b, s: (0, 0)),
            pl.BlockSpec(w_kv.shape, lambda b, s: (0, 0)),
            pl.BlockSpec((1, TS, LANES), tok),
            pl.BlockSpec((1, TS, LANES), tok),
            pl.BlockSpec((1, TS, LANES), tok),
        ],
        out_specs=[
            pl.BlockSpec((1, G, TS, HEAD_DIM), rows),
            pl.BlockSpec((1, G, TS, HEAD_DIM), rows),
            pl.BlockSpec((1, G, TS, LANES), rows),
            pl.BlockSpec((1, G, LANES, TS), cols),
            pl.BlockSpec((1, G, TS, LANES), lambda b, s: (b, 0, s, 0)),
            pl.BlockSpec((1, G, LANES, TS), lambda b, s: (b, 0, 0, s)),
        ],
        out_shape=[
            jax.ShapeDtypeStruct((B, G, S, HEAD_DIM), F32),
            jax.ShapeDtypeStruct((B, G, S, HEAD_DIM), F32),
            jax.ShapeDtypeStruct((B, G, S, LANES), BF16),
            jax.ShapeDtypeStruct((B, G, LANES, S), BF16),
            jax.ShapeDtypeStruct((B, G, WINDOW + S, LANES), BF16),
            jax.ShapeDtypeStruct((B, G, LANES, WINDOW + S), BF16),
        ],
        compiler_params=_params(2),
        name="kv_project",
    )(x, g.reshape(1, D), w_kv.astype(BF16), rc, ra, rb)


def _gelu_tanh(x):
    return 0.5 * x * (1.0 + jnp.tanh(math.sqrt(2.0 / math.pi) * (x + 0.044715 * (x * x * x))))


def _compress_kernel(k_ref, v_ref, pk_ref, w1k_ref, w2k_ref, pv_ref, w1v_ref, w2v_ref,
                     rc_ref, ra_ref, rb_ref, kc_ref, vct_ref):
    n_chunk = kc_ref.shape[2]

    def hidden(x_ref, pos_ref, w1_ref):
        first = jnp.zeros((n_chunk, CMP_HIDDEN), F32)
        second = jnp.zeros((n_chunk, CMP_HIDDEN), F32)
        for r in range(CMP_STRIDE):
            tok = x_ref[0, 0, pl.ds(r, n_chunk, stride=CMP_STRIDE), :]
            lead = (tok + pos_ref[r:r + 1, :]).astype(BF16)
            trail = (tok + pos_ref[CMP_STRIDE + r:CMP_STRIDE + r + 1, :]).astype(BF16)
            first = first + _dot(lead, w1_ref[r * HEAD_DIM:(r + 1) * HEAD_DIM, :])
            second = second + _dot(trail, w1_ref[(CMP_STRIDE + r) * HEAD_DIM:(CMP_STRIDE + r + 1) * HEAD_DIM, :])
        hid = first + pltpu.roll(second, n_chunk - 1, 0)
        return _gelu_tanh(hid).astype(BF16)

    kc = _dot(hidden(k_ref, pk_ref, w1k_ref), w2k_ref[...])
    kc_ref[0, 0] = _rope128(kc, rc_ref[0], ra_ref[0], rb_ref[0])[:, :HEAD_DIM].astype(BF16)
    vct = lax.dot_general(w2v_ref[...], hidden(v_ref, pv_ref, w1v_ref), (((1,), (1,)), ((), ())),
                          preferred_element_type=F32)
    vct_ref[0, 0] = vct[:HEAD_DIM, :].astype(BF16)


def _compress(k_raw, v_raw, pos_k, w1_k, w2_k, pos_v, w1_v, w2_v, rc, ra, rb):
    B, G, S, _ = k_raw.shape
    n_chunk = S // CMP_STRIDE
    blk = lambda b, g: (b, g, 0, 0)
    const2 = lambda b, g: (0, 0)
    pad2 = lambda w: jnp.pad(w, ((0, 0), (0, LANES - HEAD_DIM))).astype(BF16)
    return pl.pallas_call(
        _compress_kernel,
        grid=(B, G),
        in_specs=[
            pl.BlockSpec((1, 1, S, HEAD_DIM), blk),
            pl.BlockSpec((1, 1, S, HEAD_DIM), blk),
            pl.BlockSpec((CMP_LEN, HEAD_DIM), const2),
            pl.BlockSpec((CMP_LEN * HEAD_DIM, CMP_HIDDEN), const2),
            pl.BlockSpec((CMP_HIDDEN, LANES), const2),
            pl.BlockSpec((CMP_LEN, HEAD_DIM), const2),
            pl.BlockSpec((CMP_LEN * HEAD_DIM, CMP_HIDDEN), const2),
            pl.BlockSpec((CMP_HIDDEN, LANES), const2),
            pl.BlockSpec((1, n_chunk, LANES), lambda b, g: (b, 0, 0)),
            pl.BlockSpec((1, n_chunk, LANES), lambda b, g: (b, 0, 0)),
            pl.BlockSpec((1, n_chunk, LANES), lambda b, g: (b, 0, 0)),
        ],
        out_specs=[
            pl.BlockSpec((1, 1, n_chunk, HEAD_DIM), blk),
            pl.BlockSpec((1, 1, HEAD_DIM, n_chunk), blk),
        ],
        out_shape=[
            jax.ShapeDtypeStruct((B, G, n_chunk, HEAD_DIM), BF16),
            jax.ShapeDtypeStruct((B, G, HEAD_DIM, n_chunk), BF16),
        ],
        compiler_params=_params(2),
        name="compress",
    )(k_raw, v_raw, pos_k, w1_k.astype(BF16), pad2(w2_k), pos_v, w1_v.astype(BF16), pad2(w2_v).T, rc, ra, rb)


def _qproj_kernel(x_ref, g_ref, w_ref, rc_ref, ra_ref, rb_ref, q_ref, gate_ref):
    h = _rmsnorm(x_ref[0], g_ref[...]).astype(BF16)
    proj = _dot(h, w_ref[...])
    rc, ra, rb = rc_ref[0], ra_ref[0], rb_ref[0]
    scale = HEAD_DIM ** -0.5 * math.log2(math.e)
    for j in range(N_HEADS // 2):
        qt = (_rope128(proj[:, j * LANES:(j + 1) * LANES], rc, ra, rb) * scale).T
        q_ref[0, 2 * j] = qt[:HEAD_DIM].astype(BF16)
        q_ref[0, 2 * j + 1] = qt[HEAD_DIM:].astype(BF16)
    n_q = N_HEADS * HEAD_DIM
    for g in range(N_KV_GROUPS):
        gt = _sigmoid(proj[:, n_q + g * LANES:n_q + (g + 1) * LANES]).T
        gate_ref[0, g] = gt[:gate_ref.shape[2]]


def _q_weights(w_in_b):
    L, D, _ = w_in_b.shape
    n_q = N_HEADS * HEAD_DIM
    per_group = HEADS_PER_GROUP * N_BRANCH
    gate_w = w_in_b[:, :, n_q:].reshape(L, D, N_KV_GROUPS, per_group)
    gate_w = jnp.pad(gate_w, ((0, 0), (0, 0), (0, 0), (0, LANES - per_group))).reshape(L, D, N_KV_GROUPS * LANES)
    return jnp.concatenate([w_in_b[:, :, :n_q], gate_w], axis=2).astype(BF16)


def _q_project(x, i, g, j, w, rc, ra, rb):
    B, S, D = x.shape
    tok = lambda b, s: (b, s, 0)
    return pl.pallas_call(
        _qproj_kernel,
        grid=(B, S // TS),
        in_specs=[
            pl.BlockSpec((1, TS, D), tok),
            pl.BlockSpec((None, 1, D), lambda b, s: (i, 0, 0)),
            pl.BlockSpec((None,) + w.shape[1:], lambda b, s: (j, 0, 0)),
            pl.BlockSpec((1, TS, LANES), tok),
            pl.BlockSpec((1, TS, LANES), tok),
            pl.BlockSpec((1, TS, LANES), tok),
        ],
        out_specs=[
            pl.BlockSpec((1, N_HEADS, HEAD_DIM, TS), lambda b, s: (b, 0, 0, s)),
            pl.BlockSpec((1, N_KV_GROUPS, GATE_ROWS, TS), lambda b, s: (b, 0, 0, s)),
        ],
        out_shape=[
            jax.ShapeDtypeStruct((B, N_HEADS, HEAD_DIM, S), BF16),
            jax.ShapeDtypeStruct((B, N_KV_GROUPS, GATE_ROWS, S), F32),
        ],
        compiler_params=_params(2),
        name="q_project",
    )(x, g, w, rc, ra, rb)


def _attn_kernel(q_ref, gate_ref, kc_ref, vct_ref, ovl_ref, cend_ref, edge_ref, ks_ref, vts_ref, kw_ref, vtw_ref,
                 o_ref, m_ref, acc_ref, sa_ref, sb_ref, sw_ref):
    qi = pl.program_id(2)
    s0 = qi * TQ
    hpg = HEADS_PER_GROUP
    n_slc = ovl_ref.shape[0]
    span = WINDOW + TQ
    w0 = pl.multiple_of(s0, TQ)
    groups = range(ATTN_GROUPS)

    flag_row = lax.broadcasted_iota(jnp.int32, (LANES - HEAD_DIM, hpg * TQ), 0) == 0
    flag_rows = jnp.where(flag_row, NEG, 0.0).astype(BF16)
    qt, sc = [], []
    for g in groups:
        qt.append(jnp.concatenate([q_ref[0, g * hpg + hh] for hh in range(hpg)], axis=1))
        sc.append(_dot(kc_ref[0, g], qt[g]))
        sw_ref[g] = _dot(kw_ref[0, g, pl.ds(w0, span), :], jnp.concatenate([qt[g], flag_rows], axis=0))

    mask_c = cend_ref[...] <= s0
    has_block = cend_ref[0:1, :] <= s0
    oc_t, imp_t = [], []
    for g in groups:
        s_c = jnp.where(mask_c, sc[g], NEG)
        e = jnp.exp2(s_c - jnp.max(s_c, axis=0, keepdims=True))
        p_c = e * jnp.where(has_block, 1.0 / jnp.maximum(jnp.sum(e, axis=0, keepdims=True), 1e-30), 0.0)
        oc_t.append(_dot(vct_ref[0, g], p_c.astype(BF16)))
        p_sum = p_c[:, 0:TQ]
        for hh in range(1, hpg):
            p_sum = p_sum + p_c[:, hh * TQ:(hh + 1) * TQ]
        imp_t.append(jnp.dot(ovl_ref[...], p_sum, precision=lax.Precision.HIGHEST,
                             preferred_element_type=F32))

    blk = lax.broadcasted_iota(jnp.int32, (n_slc, TQ), 0)
    t_col = s0 + lax.broadcasted_iota(jnp.int32, (n_slc, TQ), 1)
    cur = t_col // SLC_BLOCK
    forced = (blk == 0) | (blk == cur) | (blk == cur - 1)
    valid = blk * SLC_BLOCK <= t_col
    blk_in = lax.broadcasted_iota(jnp.int32, (SUBLANES, TQ), 0)
    qt_sel = []
    for g in groups:
        score = jnp.where(valid, jnp.where(forced, FORCE_SCORE, imp_t[g]), NEG)
        rows = [score[r:r + SUBLANES] for r in range(0, n_slc, SUBLANES)]
        ranks = [jnp.zeros((SUBLANES, TQ), F32) for _ in rows]
        for m in range(n_slc):
            other = score[m:m + 1, :]
            for gi, grp in enumerate(rows):
                lo = gi * SUBLANES
                if lo > m:
                    ahead = other >= grp
                elif lo + SUBLANES - 1 <= m:
                    ahead = other > grp
                else:
                    ahead = (other > grp) | ((other == grp) & (blk_in + lo > m))
                ranks[gi] = ranks[gi] + jnp.where(ahead, 1.0, 0.0)
        rank = jnp.concatenate(ranks, axis=0)
        bias_t = jnp.where(valid & (rank < min(N_SELECT, n_slc)), 0.0, NEG).astype(BF16)
        qt_sel.append(jnp.concatenate([qt[g], jnp.concatenate([bias_t] * hpg, axis=1)], axis=0))

    def scores(j, out_ref):
        for g in groups:
            out_ref[g] = _dot(ks_ref[0, g, pl.ds(pl.multiple_of(j * KC, KC), KC), :], qt_sel[g])

    def absorb(in_ref, j, diagonal=False):
        for g in groups:
            s = in_ref[g]
            if diagonal:
                s = s + edge_ref[0]
            m_old = m_ref[g]
            m_new = jnp.maximum(m_old, jnp.max(s, axis=0, keepdims=True))
            p = jnp.exp2(s - m_new).astype(BF16)
            v_t = vts_ref[0, g, :, pl.ds(pl.multiple_of(j * KC, KC), KC)]
            acc_ref[g] = jnp.exp2(m_old - m_new) * acc_ref[g] + _dot(v_t, p)
            m_ref[g] = m_new

    m_ref[...] = jnp.full_like(m_ref, NEG)
    acc_ref[...] = jnp.zeros_like(acc_ref)
    scores(0, sa_ref)

    ow_t = []
    for g in groups:
        lo_edge = sw_ref[g, 0:TQ] + edge_ref[1]
        middle = sw_ref[g, TQ:WINDOW]
        hi_edge = sw_ref[g, WINDOW:] + edge_ref[0]
        m_w = jnp.maximum(jnp.maximum(jnp.max(lo_edge, axis=0, keepdims=True),
                                      jnp.max(middle, axis=0, keepdims=True)),
                          jnp.max(hi_edge, axis=0, keepdims=True))
        rw = (_dot(vtw_ref[0, g, :, pl.ds(w0, TQ)], jnp.exp2(lo_edge - m_w).astype(BF16))
              + _dot(vtw_ref[0, g, :, pl.ds(w0 + TQ, WINDOW - TQ)], jnp.exp2(middle - m_w).astype(BF16))
              + _dot(vtw_ref[0, g, :, pl.ds(w0 + WINDOW, TQ)], jnp.exp2(hi_edge - m_w).astype(BF16)))
        ow_t.append(rw[:HEAD_DIM] * (1.0 / jnp.maximum(rw[HEAD_DIM:HEAD_DIM + 1], 1e-30)))

    def pair(i, carry):
        scores(2 * i + 1, sb_ref)
        absorb(sa_ref, 2 * i)
        scores(2 * i + 2, sa_ref)
        absorb(sb_ref, 2 * i + 1)
        return carry

    lax.fori_loop(0, qi // 2, pair, 0)

    @pl.when(qi % 2 == 0)
    def _():
        absorb(sa_ref, qi, diagonal=True)

    @pl.when(qi % 2 == 1)
    def _():
        scores(qi, sb_ref)
        absorb(sa_ref, qi - 1)
        absorb(sb_ref, qi, diagonal=True)

    outs = []
    for g in groups:
        acc = acc_ref[g]
        os_t = acc[:HEAD_DIM] * (1.0 / acc[HEAD_DIM:HEAD_DIM + 1])
        gates = gate_ref[0, g]
        for hh in range(hpg):
            c = slice(hh * TQ, (hh + 1) * TQ)
            r = hh * N_BRANCH
            outs.append(gates[r:r + 1] * oc_t[g][:, c] + gates[r + 1:r + 2] * os_t[:, c]
                        + gates[r + 2:r + 3] * ow_t[g][:, c])
    o_ref[0] = jnp.concatenate(outs, axis=0).T.astype(o_ref.dtype)


def _attention(q_t, gates_t, kc, vct, ovl_t, ks, vts, kw, vtw):
    B, _, _, S = q_t.shape
    G = N_KV_GROUPS
    hpg = HEADS_PER_GROUP
    ng = ATTN_GROUPS
    assert KC == TQ and WINDOW % TQ == 0 and G % ng == 0
    cols = hpg * TQ
    key = np.arange(TQ)[:, None]
    t_local = (np.arange(cols) % TQ)[None, :]
    cend = jnp.asarray(CMP_STRIDE * np.arange(kc.shape[2])[:, None] + CMP_LEN - 1 - t_local, jnp.int32)
    edges = jnp.asarray(np.where(np.stack([key <= t_local, key > t_local]), 0.0, NEG), F32)
    per_bg = lambda b, g, qi: (b, g, 0, 0)
    return pl.pallas_call(
        _attn_kernel,
        grid=(B, G // ng, S // TQ),
        in_specs=[
            pl.BlockSpec((1, ng * hpg, HEAD_DIM, TQ), lambda b, g, qi: (b, g, 0, qi)),
            pl.BlockSpec((1, ng, GATE_ROWS, TQ), lambda b, g, qi: (b, g, 0, qi)),
            pl.BlockSpec((1, ng) + kc.shape[2:], per_bg),
            pl.BlockSpec((1, ng) + vct.shape[2:], per_bg),
            pl.BlockSpec(ovl_t.shape, lambda b, g, qi: (0, 0)),
            pl.BlockSpec(cend.shape, lambda b, g, qi: (0, 0)),
            pl.BlockSpec(edges.shape, lambda b, g, qi: (0, 0, 0)),
            pl.BlockSpec((1, ng) + ks.shape[2:], per_bg),
            pl.BlockSpec((1, ng) + vts.shape[2:], per_bg),
            pl.BlockSpec((1, ng) + kw.shape[2:], per_bg),
            pl.BlockSpec((1, ng) + vtw.shape[2:], per_bg),
        ],
        out_specs=pl.BlockSpec((1, TQ, ng * hpg * HEAD_DIM), lambda b, g, qi: (b, qi, g)),
        out_shape=jax.ShapeDtypeStruct((B, S, N_HEADS * HEAD_DIM), BF16),
        scratch_shapes=[
            pltpu.VMEM((ng, 1, cols), F32),
            pltpu.VMEM((ng, LANES, cols), F32),
            pltpu.VMEM((ng, KC, cols), F32),
            pltpu.VMEM((ng, KC, cols), F32),
            pltpu.VMEM((ng, WINDOW + TQ, cols), F32),
        ],
        compiler_params=_params(3),
        name="sparse_attention",
    )(q_t, gates_t, kc, vct, ovl_t, cend, edges, ks, vts, kw, vtw)


def _rope_tables(pos):
    inv_freq = ROPE_THETA ** (-jnp.arange(0, ROT_DIM, 2, dtype=F32) / ROT_DIM)
    ang = pos.astype(F32)[..., None] * inv_freq
    cs = jnp.concatenate([jnp.cos(ang), jnp.sin(ang)], axis=-1)
    half = ROT_DIM // 2
    place = np.zeros((3, ROT_DIM, LANES), np.float32)
    const = np.zeros((3, LANES), np.float32)
    for lane in range(LANES):
        d = lane % HEAD_DIM
        if d < ROT_DIM:
            place[0, d % half, lane] = 1.0
            if d < half:
                place[1, half + d, lane] = -1.0
            else:
                place[2, half + d - half, lane] = 1.0
        else:
            const[0, lane] = 1.0
    tabs = jnp.einsum('...k,tkn->t...n', cs, jnp.asarray(place), precision=lax.Precision.HIGHEST)
    return tuple(tabs[t] + const[t] for t in range(3))


def _overlap_t(n_chunk, n_slc):
    c_start = CMP_STRIDE * np.arange(n_chunk)
    blk = np.arange(n_slc)
    ovl = ((c_start[None] < (blk[:, None] + 1) * SLC_BLOCK)
           & (c_start[None] + CMP_LEN > blk[:, None] * SLC_BLOCK)
           & (np.arange(n_chunk)[None] < n_chunk - 1))
    return jnp.asarray(ovl.astype(np.float32))


def kernel(x, p, positions, norm_mix, pool_w, pool_scale, norm_kv, w_kv, cmp_pos_k, cmp_w1_k, cmp_w2_k, cmp_pos_v, cmp_w1_v, cmp_w2_v, w_in_b, w_out_b, norm_ffn, ffn_up, ffn_conv, ffn_conv_b, ffn_down, norm_ple, ple_gate, ple_proj, norm_final):
    B, S, D = x.shape
    n_chunk = S // CMP_STRIDE
    assert S // SLC_BLOCK == LANES - HEAD_DIM
    rc, ra, rb = _rope_tables(positions)
    end_idx = jnp.minimum(CMP_STRIDE * jnp.arange(n_chunk) + CMP_LEN - 1, S - 1)
    rc_c, ra_c, rb_c = _rope_tables(positions[:, end_idx])
    ovl_t = _overlap_t(n_chunk, S // SLC_BLOCK)

    stack = lambda g: g.reshape(g.shape[0], 1, g.shape[1])
    norm_mix3, norm_ffn3, norm_ple3, pool_scale3 = map(stack, (norm_mix, norm_ffn, norm_ple, pool_scale))
    conv_b3 = stack(ffn_conv_b)
    pool_w16, up16, down16, gate16, proj16, w_out16 = (
        w.astype(BF16) for w in (pool_w, ffn_up, ffn_down, ple_gate, ple_proj, w_out_b))
    w_q16 = _q_weights(w_in_b)

    kv = None
    for i in range(DEPTH):
        attn = None
        j = i - N_A
        if i >= N_A:
            if i == N_A:
                kc_raw, vc_raw, ks, vts, kw, vtw = _kv_project(x, norm_kv, w_kv, rc, ra, rb)
                kc, vct = _compress(kc_raw, vc_raw, cmp_pos_k, cmp_w1_k, cmp_w2_k,
                                    cmp_pos_v, cmp_w1_v, cmp_w2_v, rc_c, ra_c, rb_c)
                kv = (kc, vct, ovl_t, ks, vts, kw, vtw)
            q_t, gates_t = _q_project(x, i, norm_mix3, j, w_q16, rc, ra, rb)
            attn = _attention(q_t, gates_t, *kv)
        x = _ffn_layer(x, i, attn, w_out16, j, norm_mix3, pool_w16, pool_scale3, norm_ffn3, up16, ffn_conv, conv_b3,
                       down16, norm_ple3, gate16, p, proj16, norm_final if i == DEPTH - 1 else None)
    return x
```

```python
import functools
import math

import jax
import jax.numpy as jnp
import numpy as np
from jax import lax
from jax.experimental import pallas as pl
from jax.experimental.pallas import tpu as pltpu

D_MODEL = 1024
DEPTH = 4
N_A = DEPTH // 2
POOL_WINDOWS = (2, 4, 8, 16)
POOL_GROUP = D_MODEL // len(POOL_WINDOWS)
POOL_HALO = 16
N_HEADS = 16
HEAD_DIM = 64
N_KV_GROUPS = 4
HEADS_PER_GROUP = N_HEADS // N_KV_GROUPS
CMP_STRIDE = 16
CMP_LEN = 2 * CMP_STRIDE
CMP_HIDDEN = 128
SLC_BLOCK = 64
N_SELECT = 16
WINDOW = 512
N_BRANCH = 3
ROPE_THETA = 500000.0
ROT_DIM = HEAD_DIM // 4
D_FF = 2816
CONV_W = 3
PLE_DIM = 256
EPS = 1e-6
NEG = -1e30
FORCE_SCORE = 1e4

LANES = 128
SUBLANES = 8
VMEM_LIMIT = 48 * 1024 * 1024

TS = 512
FC = 256
FFN_UNROLL = 12
TQ = 256
KC = TQ
ATTN_GROUPS = 2
GATE_ROWS = 16

BF16 = jnp.bfloat16
F32 = jnp.float32


def _dot(a, b):
    return jnp.dot(a, b, preferred_element_type=F32)


def _rmsnorm(x, g):
    return x * lax.rsqrt(jnp.mean(x * x, axis=-1, keepdims=True) + EPS) * g


def _sigmoid(x):
    return 1.0 / (1.0 + jnp.exp(-x))


def _rope128(x, c, a, b):
    return x * c + pltpu.roll(x, LANES - ROT_DIM // 2, 1) * a + pltpu.roll(x, ROT_DIM // 2, 1) * b


def _params(n_grid):
    return pltpu.CompilerParams(dimension_semantics=("arbitrary",) * n_grid,
                                vmem_limit_bytes=VMEM_LIMIT)


def _pool_mix(h, history, s, w_ref, scale):
    ext = jnp.concatenate([history, h], axis=0)
    sums = {1: ext}
    w = 1
    while w < POOL_WINDOWS[-1]:
        prev = sums[w]
        sums[2 * w] = prev + pltpu.roll(prev, w, 0)
        w *= 2
    t = s * TS + lax.broadcasted_iota(jnp.int32, (TS, 1), 0)
    outs = []
    for gi, win in enumerate(POOL_WINDOWS):
        lo, hi = gi * POOL_GROUP, (gi + 1) * POOL_GROUP
        tot = sums[win][POOL_HALO:, lo:hi]
        cnt = jnp.minimum(t + 1, win).astype(F32)
        u = tot / cnt - h[:, lo:hi]
        outs.append(_dot(u.astype(BF16), w_ref[gi]))
    return jnp.concatenate(outs, axis=-1) * scale


def _ffn_kernel(*refs, has_attn, final_norm):
    it = iter(refs)
    x_ref = next(it)
    if has_attn:
        attn_ref, wout_ref = next(it), next(it)
    else:
        gmix_ref, poolw_ref, pscale_ref = next(it), next(it), next(it)
    gffn_ref, wup_ref, cw_ref, cb_ref, wdown_ref = (next(it) for _ in range(5))
    gple_ref, pgate_ref, p_ref, pproj_ref = (next(it) for _ in range(4))
    if final_norm:
        gfin_ref = next(it)
    o_ref = next(it)
    xres_ref, h_ref, acc_ref, carry_ref = (next(it) for _ in range(4))
    u_refs = (next(it), next(it))
    emb_ref = next(it)
    if not has_attn:
        hist_ref = next(it)

    s = pl.program_id(1)
    k = pl.program_id(2)
    n_k = (D_FF // FC + 1) // FFN_UNROLL
    run = lambda f: f()
    first_step = run if n_k == 1 else pl.when(k == 0)
    last_step = run if n_k == 1 else pl.when(k == n_k - 1)

    @pl.when(s == 0 if n_k == 1 else (s == 0) & (k == 0))
    def _():
        carry_ref[...] = jnp.zeros_like(carry_ref)
        if not has_attn:
            hist_ref[...] = jnp.zeros_like(hist_ref)

    @first_step
    def _():
        xin = x_ref[...]
        if has_attn:
            xin = xin + _dot(attn_ref[...], wout_ref[...])
        else:
            hm = _rmsnorm(xin, gmix_ref[...])
            xin = xin + _pool_mix(hm, hist_ref[...], s, poolw_ref, pscale_ref[...])
            hist_ref[...] = hm[TS - POOL_HALO:, :]
        xres_ref[...] = xin
        emb_ref[...] = _dot(p_ref[...].astype(BF16), pproj_ref[...])
        h_ref[...] = _rmsnorm(xin, gffn_ref[...]).astype(BF16)
        acc_ref[...] = jnp.zeros_like(acc_ref)
        u_refs[1][...] = jnp.zeros_like(u_refs[1])

    def aligned(start):
        return start if isinstance(start, int) else pl.multiple_of(start, FC)

    def cols(chunk, gate):
        return pl.ds(aligned(gate * D_FF + chunk * FC), FC)

    def half_step(u_in_ref, u_out_ref, chunk, valid, chunk_up):
        for slot in range(2 if chunk_up is not None else 0):
            u_out_ref[slot, SUBLANES:, :] = _dot(h_ref[...], wup_ref[:, cols(chunk_up, slot)])

        if valid is False:
            return

        def conv_half(slot):
            u = u_in_ref[slot, SUBLANES:, :]
            last = carry_ref[chunk, slot]
            u_in_ref[slot, 0:SUBLANES, :] = last
            carry_ref[chunk, slot] = u[TS - SUBLANES:, :] if valid is None else jnp.where(
                valid, u[TS - SUBLANES:, :], last)
            cw = cw_ref[:, cols(chunk, slot)]
            out = cw[CONV_W - 1:CONV_W] * u + cb_ref[:, cols(chunk, slot)]
            for back in range(1, CONV_W):
                shifted = u_in_ref[slot, pl.ds(SUBLANES - back, TS), :]
                out = out + cw[CONV_W - 1 - back:CONV_W - back] * shifted
            return out

        a = conv_half(0)
        g = conv_half(1)
        act = g * _sigmoid(g) * a
        if valid is not None:
            act = jnp.where(valid, act, 0.0)
        acc_ref[...] += _dot(act.astype(BF16), wdown_ref[pl.ds(aligned(chunk * FC), FC), :])

    n_c = D_FF // FC
    for r in range(FFN_UNROLL):
        if n_k == 1:
            chunk, chunk_up, valid = max(r - 1, 0), (r if r < n_c else None), (False if r == 0 else None)
        else:
            chunk, chunk_up = jnp.maximum(FFN_UNROLL * k + r - 1, 0), jnp.minimum(FFN_UNROLL * k + r, n_c - 1)
            valid = (k > 0) if r == 0 else None
        half_step(u_refs[(r + 1) % 2], u_refs[r % 2], chunk, valid, chunk_up)

    @last_step
    def _():
        x2 = xres_ref[...] + acc_ref[...]
        hp = _rmsnorm(x2, gple_ref[...]).astype(BF16)
        gate = _sigmoid(_dot(hp, pgate_ref[...]))
        x3 = x2 + gate * emb_ref[...]
        if final_norm:
            x3 = _rmsnorm(x3, gfin_ref[...])
        o_ref[...] = x3


def _ffn_layer(x, i, attn, w_out, j, g_mix, pool_w, pool_scale, g_ffn, w_up, conv_w, conv_b, w_down,
               g_ple, ple_gate, p, ple_proj, g_final):
    B, S, D = x.shape
    n_c = D_FF // FC
    assert (n_c + 1) % FFN_UNROLL == 0 and FFN_UNROLL % 2 == 0
    n_k = (n_c + 1) // FFN_UNROLL
    has_attn = attn is not None
    final_norm = g_final is not None
    tok = lambda b, s, k: (b, s, 0)
    layer = lambda b, s, k: (i, 0, 0)
    vec = pl.BlockSpec((None, 1, D), layer)

    def resident(shape, index_map):
        return pl.BlockSpec((None,) + shape, index_map, pipeline_mode=pl.Buffered(1))

    args, specs = [x], [pl.BlockSpec((None, TS, D), tok)]
    if has_attn:
        args += [attn, w_out]
        specs += [pl.BlockSpec((None, TS, D), tok), resident((D, D), lambda b, s, k: (j, 0, 0))]
    else:
        args += [g_mix, pool_w, pool_scale]
        specs += [vec, resident((len(POOL_WINDOWS), POOL_GROUP, POOL_GROUP), lambda b, s, k: (i, 0, 0, 0)), vec]
    args += [g_ffn, w_up, conv_w, conv_b, w_down, g_ple, ple_gate, p, ple_proj]
    specs += [
        vec,
        resident((D, 2 * D_FF), layer),
        resident((CONV_W, 2 * D_FF), layer),
        resident((1, 2 * D_FF), layer),
        resident((D_FF, D), layer),
        vec,
        resident((D, D), layer),
        pl.BlockSpec((None, None, TS, PLE_DIM), lambda b, s, k: (i, b, s, 0)),
        resident((PLE_DIM, D), layer),
    ]
    if final_norm:
        args.append(g_final.reshape(1, D))
        specs.append(pl.BlockSpec((1, D), lambda b, s, k: (0, 0)))
    return pl.pallas_call(
        functools.partial(_ffn_kernel, has_attn=has_attn, final_norm=final_norm),
        grid=(B, S // TS, n_k),
        in_specs=specs,
        out_specs=pl.BlockSpec((None, TS, D), tok),
        out_shape=jax.ShapeDtypeStruct(x.shape, F32),
        scratch_shapes=[
            pltpu.VMEM((TS, D), F32),
            pltpu.VMEM((TS, D), BF16),
            pltpu.VMEM((TS, D), F32),
            pltpu.VMEM((n_c, 2, SUBLANES, FC), F32),
            pltpu.VMEM((2, SUBLANES + TS, FC), F32),
            pltpu.VMEM((2, SUBLANES + TS, FC), F32),
            pltpu.VMEM((TS, D), F32),
        ] + ([] if has_attn else [pltpu.VMEM((POOL_HALO, D), F32)]),
        compiler_params=_params(3),
        name="ffn_embed",
    )(*args)


def _kv_kernel(x_ref, g_ref, w_ref, rc_ref, ra_ref, rb_ref,
               kc_ref, vc_ref, ks_ref, vts_ref, kw_ref, vtw_ref):
    s = pl.program_id(1)
    lane = lax.broadcasted_iota(jnp.int32, (TS, LANES), 1)

    @pl.when(s == 0)
    def _():
        flag = jnp.where(lane == HEAD_DIM, 1.0, 0.0).astype(BF16)
        for g in range(N_KV_GROUPS):
            kw_ref[0, g] = flag
            vtw_ref[0, g] = jnp.zeros((LANES, TS), BF16)

    @pl.when(s > 0)
    def _():
        h = _rmsnorm(x_ref[0], g_ref[...]).astype(BF16)
        kv = _dot(h, w_ref[...])
        gw = N_KV_GROUPS * HEAD_DIM
        rc, ra, rb = rc_ref[0], ra_ref[0], rb_ref[0]
        tok = (s - 1) * TS + lax.broadcasted_iota(jnp.int32, (TS, LANES), 0)
        onehot = jnp.where(lane - HEAD_DIM == tok // SLC_BLOCK, 1.0, 0.0)
        row = lax.broadcasted_iota(jnp.int32, (LANES - HEAD_DIM, TS), 0)
        ones_row = jnp.where(row == 0, 1.0, 0.0).astype(BF16)

        def slabs(i):
            return [kv[:, i * gw + j * LANES:i * gw + (j + 1) * LANES] for j in range(gw // LANES)]

        for g in range(N_KV_GROUPS):
            kc_ref[0, g] = kv[:, 0 * gw + g * HEAD_DIM:0 * gw + (g + 1) * HEAD_DIM]
            vc_ref[0, g] = kv[:, 1 * gw + g * HEAD_DIM:1 * gw + (g + 1) * HEAD_DIM]

        for i, k_ref, vt_ref in ((2, ks_ref, vts_ref), (4, kw_ref, vtw_ref)):
            for j, (kslab, vslab) in enumerate(zip(slabs(i), slabs(i + 1))):
                kr = _rope128(kslab, rc, ra, rb)
                vt = vslab.T
                for half in range(2):
                    g = 2 * j + half
                    kh = kr if half == 0 else pltpu.roll(kr, HEAD_DIM, 1)
                    upper = onehot if k_ref is ks_ref else 0.0
                    k_ref[0, g] = jnp.where(lane < HEAD_DIM, kh, upper).astype(BF16)
                    vt_ref[0, g, 0:HEAD_DIM, :] = vt[half * HEAD_DIM:(half + 1) * HEAD_DIM, :].astype(BF16)
                    vt_ref[0, g, HEAD_DIM:, :] = ones_row


def _kv_project(x, g, w_kv, rc, ra, rb):
    B, S, D = x.shape
    G = N_KV_GROUPS
    assert WINDOW == TS
    tile = lambda s: jnp.maximum(s - 1, 0)
    tok = lambda b, s: (b, tile(s), 0)
    rows = lambda b, s: (b, 0, tile(s), 0)
    cols = lambda b, s: (b, 0, 0, tile(s))
    return pl.pallas_call(
        _kv_kernel,
        grid=(B, S // TS + 1),
        in_specs=[
            pl.BlockSpec((1, TS, D), tok),
            pl.BlockSpec((1, D), lambda b, s: (0, 0)),
            pl.BlockSpec(w_kv.shape, lambda b, s: (0, 0)),
            pl.BlockSpec((1, TS, LANES), tok),
            pl.BlockSpec((1, TS, LANES), tok),
            pl.BlockSpec((1, TS, LANES), tok),
        ],
        out_specs=[
            pl.BlockSpec((1, G, TS, HEAD_DIM), rows),
            pl.BlockSpec((1, G, TS, HEAD_DIM), rows),
            pl.BlockSpec((1, G, TS, LANES), rows),
            pl.BlockSpec((1, G, LANES, TS), cols),
            pl.BlockSpec((1, G, TS, LANES), lambda b, s: (b, 0, s, 0)),
            pl.BlockSpec((1, G, LANES, TS), lambda b, s: (b, 0, 0, s)),
        ],
        out_shape=[
            jax.ShapeDtypeStruct((B, G, S, HEAD_DIM), F32),
            jax.ShapeDtypeStruct((B, G, S, HEAD_DIM), F32),
            jax.ShapeDtypeStruct((B, G, S, LANES), BF16),
            jax.ShapeDtypeStruct((B, G, LANES, S), BF16),
            jax.ShapeDtypeStruct((B, G, WINDOW + S, LANES), BF16),
            jax.ShapeDtypeStruct((B, G, LANES, WINDOW + S), BF16),
        ],
        compiler_params=_params(2),
        name="kv_project",
    )(x, g.reshape(1, D), w_kv.astype(BF16), rc, ra, rb)


def _gelu_tanh(x):
    return 0.5 * x * (1.0 + jnp.tanh(math.sqrt(2.0 / math.pi) * (x + 0.044715 * (x * x * x))))


def _compress_kernel(k_ref, v_ref, pk_ref, w1k_ref, w2k_ref, pv_ref, w1v_ref, w2v_ref,
                     rc_ref, ra_ref, rb_ref, kc_ref, vct_ref):
    n_chunk = kc_ref.shape[2]

    def hidden(x_ref, pos_ref, w1_ref):
        first = jnp.zeros((n_chunk, CMP_HIDDEN), F32)
        second = jnp.zeros((n_chunk, CMP_HIDDEN), F32)
        for r in range(CMP_STRIDE):
            tok = x_ref[0, 0, pl.ds(r, n_chunk, stride=CMP_STRIDE), :]
            lead = (tok + pos_ref[r:r + 1, :]).astype(BF16)
            trail = (tok + pos_ref[CMP_STRIDE + r:CMP_STRIDE + r + 1, :]).astype(BF16)
            first = first + _dot(lead, w1_ref[r * HEAD_DIM:(r + 1) * HEAD_DIM, :])
            second = second + _dot(trail, w1_ref[(CMP_STRIDE + r) * HEAD_DIM:(CMP_STRIDE + r + 1) * HEAD_DIM, :])
        hid = first + pltpu.roll(second, n_chunk - 1, 0)
        return _gelu_tanh(hid).astype(BF16)

    kc = _dot(hidden(k_ref, pk_ref, w1k_ref), w2k_ref[...])
    kc_ref[0, 0] = _rope128(kc, rc_ref[0], ra_ref[0], rb_ref[0])[:, :HEAD_DIM].astype(BF16)
    vct = lax.dot_general(w2v_ref[...], hidden(v_ref, pv_ref, w1v_ref), (((1,), (1,)), ((), ())),
                          preferred_element_type=F32)
    vct_ref[0, 0] = vct[:HEAD_DIM, :].astype(BF16)


def _compress(k_raw, v_raw, pos_k, w1_k, w2_k, pos_v, w1_v, w2_v, rc, ra, rb):
    B, G, S, _ = k_raw.shape
    n_chunk = S // CMP_STRIDE
    blk = lambda b, g: (b, g, 0, 0)
    const2 = lambda b, g: (0, 0)
    pad2 = lambda w: jnp.pad(w, ((0, 0), (0, LANES - HEAD_DIM))).astype(BF16)
    return pl.pallas_call(
        _compress_kernel,
        grid=(B, G),
        in_specs=[
            pl.BlockSpec((1, 1, S, HEAD_DIM), blk),
            pl.BlockSpec((1, 1, S, HEAD_DIM), blk),
            pl.BlockSpec((CMP_LEN, HEAD_DIM), const2),
            pl.BlockSpec((CMP_LEN * HEAD_DIM, CMP_HIDDEN), const2),
            pl.BlockSpec((CMP_HIDDEN, LANES), const2),
            pl.BlockSpec((CMP_LEN, HEAD_DIM), const2),
            pl.BlockSpec((CMP_LEN * HEAD_DIM, CMP_HIDDEN), const2),
            pl.BlockSpec((CMP_HIDDEN, LANES), const2),
            pl.BlockSpec((1, n_chunk, LANES), lambda b, g: (b, 0, 0)),
            pl.BlockSpec((1, n_chunk, LANES), lambda b, g: (b, 0, 0)),
            pl.BlockSpec((1, n_chunk, LANES), lambda b, g: (b, 0, 0)),
        ],
        out_specs=[
            pl.BlockSpec((1, 1, n_chunk, HEAD_DIM), blk),
            pl.BlockSpec((1, 1, HEAD_DIM, n_chunk), blk),
        ],
        out_shape=[
            jax.ShapeDtypeStruct((B, G, n_chunk, HEAD_DIM), BF16),
            jax.ShapeDtypeStruct((B, G, HEAD_DIM, n_chunk), BF16),
        ],
        compiler_params=_params(2),
        name="compress",
    )(k_raw, v_raw, pos_k, w1_k.astype(BF16), pad2(w2_k), pos_v, w1_v.astype(BF16), pad2(w2_v).T, rc, ra, rb)


def _qproj_kernel(x_ref, g_ref, w_ref, rc_ref, ra_ref, rb_ref, q_ref, gate_ref):
    h = _rmsnorm(x_ref[0], g_ref[...]).astype(BF16)
    proj = _dot(h, w_ref[...])
    rc, ra, rb = rc_ref[0], ra_ref[0], rb_ref[0]
    scale = HEAD_DIM ** -0.5 * math.log2(math.e)
    for j in range(N_HEADS // 2):
        qt = (_rope128(proj[:, j * LANES:(j + 1) * LANES], rc, ra, rb) * scale).T
        q_ref[0, 2 * j] = qt[:HEAD_DIM].astype(BF16)
        q_ref[0, 2 * j + 1] = qt[HEAD_DIM:].astype(BF16)
    n_q = N_HEADS * HEAD_DIM
    for g in range(N_KV_GROUPS):
        gt = _sigmoid(proj[:, n_q + g * LANES:n_q + (g + 1) * LANES]).T
        gate_ref[0, g] = gt[:gate_ref.shape[2]]


def _q_weights(w_in_b):
    L, D, _ = w_in_b.shape
    n_q = N_HEADS * HEAD_DIM
    per_group = HEADS_PER_GROUP * N_BRANCH
    gate_w = w_in_b[:, :, n_q:].reshape(L, D, N_KV_GROUPS, per_group)
    gate_w = jnp.pad(gate_w, ((0, 0), (0, 0), (0, 0), (0, LANES - per_group))).reshape(L, D, N_KV_GROUPS * LANES)
    return jnp.concatenate([w_in_b[:, :, :n_q], gate_w], axis=2).astype(BF16)


def _q_project(x, i, g, j, w, rc, ra, rb):
    B, S, D = x.shape
    tok = lambda b, s: (b, s, 0)
    return pl.pallas_call(
        _qproj_kernel,
        grid=(B, S // TS),
        in_specs=[
            pl.BlockSpec((1, TS, D), tok),
            pl.BlockSpec((None, 1, D), lambda b, s: (i, 0, 0)),
            pl.BlockSpec((None,) + w.shape[1:], lambda b, s: (j, 0, 0)),
            pl.BlockSpec((1, TS, LANES), tok),
            pl.BlockSpec((1, TS, LANES), tok),
            pl.BlockSpec((1, TS, LANES), tok),
        ],
        out_specs=[
            pl.BlockSpec((1, N_HEADS, HEAD_DIM, TS), lambda b, s: (b, 0, 0, s)),
            pl.BlockSpec((1, N_KV_GROUPS, GATE_ROWS, TS), lambda b, s: (b, 0, 0, s)),
        ],
        out_shape=[
            jax.ShapeDtypeStruct((B, N_HEADS, HEAD_DIM, S), BF16),
            jax.ShapeDtypeStruct((B, N_KV_GROUPS, GATE_ROWS, S), F32),
        ],
        compiler_params=_params(2),
        name="q_project",
    )(x, g, w, rc, ra, rb)


def _attn_kernel(q_ref, gate_ref, kc_ref, vct_ref, ovl_ref, cend_ref, edge_ref, ks_ref, vts_ref, kw_ref, vtw_ref,
                 o_ref, m_ref, acc_ref, sa_ref, sb_ref, sw_ref):
    qi = pl.program_id(2)
    s0 = qi * TQ
    hpg = HEADS_PER_GROUP
    n_slc = ovl_ref.shape[0]
    span = WINDOW + TQ
    w0 = pl.multiple_of(s0, TQ)
    groups = range(ATTN_GROUPS)

    flag_row = lax.broadcasted_iota(jnp.int32, (LANES - HEAD_DIM, hpg * TQ), 0) == 0
    flag_rows = jnp.where(flag_row, NEG, 0.0).astype(BF16)
    qt, sc = [], []
    for g in groups:
        qt.append(jnp.concatenate([q_ref[0, g * hpg + hh] for hh in range(hpg)], axis=1))
        sc.append(_dot(kc_ref[0, g], qt[g]))
        sw_ref[g] = _dot(kw_ref[0, g, pl.ds(w0, span), :], jnp.concatenate([qt[g], flag_rows], axis=0))

    mask_c = cend_ref[...] <= s0
    has_block = cend_ref[0:1, :] <= s0
    oc_t, imp_t = [], []
    for g in groups:
        s_c = jnp.where(mask_c, sc[g], NEG)
        e = jnp.exp2(s_c - jnp.max(s_c, axis=0, keepdims=True))
        p_c = e * jnp.where(has_block, 1.0 / jnp.maximum(jnp.sum(e, axis=0, keepdims=True), 1e-30), 0.0)
        oc_t.append(_dot(vct_ref[0, g], p_c.astype(BF16)))
        p_sum = p_c[:, 0:TQ]
        for hh in range(1, hpg):
            p_sum = p_sum + p_c[:, hh * TQ:(hh + 1) * TQ]
        imp_t.append(jnp.dot(ovl_ref[...], p_sum, precision=lax.Precision.HIGHEST,
                             preferred_element_type=F32))

    blk = lax.broadcasted_iota(jnp.int32, (n_slc, TQ), 0)
    t_col = s0 + lax.broadcasted_iota(jnp.int32, (n_slc, TQ), 1)
    cur = t_col // SLC_BLOCK
    forced = (blk == 0) | (blk == cur) | (blk == cur - 1)
    valid = blk * SLC_BLOCK <= t_col
    blk_in = lax.broadcasted_iota(jnp.int32, (SUBLANES, TQ), 0)
    qt_sel = []
    for g in groups:
        score = jnp.where(valid, jnp.where(forced, FORCE_SCORE, imp_t[g]), NEG)
        rows = [score[r:r + SUBLANES] for r in range(0, n_slc, SUBLANES)]
        ranks = [jnp.zeros((SUBLANES, TQ), F32) for _ in rows]
        for m in range(n_slc):
            other = score[m:m + 1, :]
            for gi, grp in enumerate(rows):
                lo = gi * SUBLANES
                if lo > m:
                    ahead = other >= grp
                elif lo + SUBLANES - 1 <= m:
                    ahead = other > grp
                else:
                    ahead = (other > grp) | ((other == grp) & (blk_in + lo > m))
                ranks[gi] = ranks[gi] + jnp.where(ahead, 1.0, 0.0)
        rank = jnp.concatenate(ranks, axis=0)
        bias_t = jnp.where(valid & (rank < min(N_SELECT, n_slc)), 0.0, NEG).astype(BF16)
        qt_sel.append(jnp.concatenate([qt[g], jnp.concatenate([bias_t] * hpg, axis=1)], axis=0))

    def scores(j, out_ref):
        for g in groups:
            out_ref[g] = _dot(ks_ref[0, g, pl.ds(pl.multiple_of(j * KC, KC), KC), :], qt_sel[g])

    def absorb(in_ref, j, diagonal=False):
        for g in groups:
            s = in_ref[g]
            if diagonal:
                s = s + edge_ref[0]
            m_old = m_ref[g]
            m_new = jnp.maximum(m_old, jnp.max(s, axis=0, keepdims=True))
            p = jnp.exp2(s - m_new).astype(BF16)
            v_t = vts_ref[0, g, :, pl.ds(pl.multiple_of(j * KC, KC), KC)]
            acc_ref[g] = jnp.exp2(m_old - m_new) * acc_ref[g] + _dot(v_t, p)
            m_ref[g] = m_new

    m_ref[...] = jnp.full_like(m_ref, NEG)
    acc_ref[...] = jnp.zeros_like(acc_ref)
    scores(0, sa_ref)

    ow_t = []
    for g in groups:
        lo_edge = sw_ref[g, 0:TQ] + edge_ref[1]
        middle = sw_ref[g, TQ:WINDOW]
        hi_edge = sw_ref[g, WINDOW:] + edge_ref[0]
        m_w = jnp.maximum(jnp.maximum(jnp.max(lo_edge, axis=0, keepdims=True),
                                      jnp.max(middle, axis=0, keepdims=True)),
                          jnp.max(hi_edge, axis=0, keepdims=True))
        rw = (_dot(vtw_ref[0, g, :, pl.ds(w0, TQ)], jnp.exp2(lo_edge - m_w).astype(BF16))
              + _dot(vtw_ref[0, g, :, pl.ds(w0 + TQ, WINDOW - TQ)], jnp.exp2(middle - m_w).astype(BF16))
              + _dot(vtw_ref[0, g, :, pl.ds(w0 + WINDOW, TQ)], jnp.exp2(hi_edge - m_w).astype(BF16)))
        ow_t.append(rw[:HEAD_DIM] * (1.0 / jnp.maximum(rw[HEAD_DIM:HEAD_DIM + 1], 1e-30)))

    def pair(i, carry):
        scores(2 * i + 1, sb_ref)
        absorb(sa_ref, 2 * i)
        scores(2 * i + 2, sa_ref)
        absorb(sb_ref, 2 * i + 1)
        return carry

    lax.fori_loop(0, qi // 2, pair, 0)

    @pl.when(qi % 2 == 0)
    def _():
        absorb(sa_ref, qi, diagonal=True)

    @pl.when(qi % 2 == 1)
    def _():
        scores(qi, sb_ref)
        absorb(sa_ref, qi - 1)
        absorb(sb_ref, qi, diagonal=True)

    outs = []
    for g in groups:
        acc = acc_ref[g]
        os_t = acc[:HEAD_DIM] * (1.0 / acc[HEAD_DIM:HEAD_DIM + 1])
        gates = gate_ref[0, g]
        for hh in range(hpg):
            c = slice(hh * TQ, (hh + 1) * TQ)
            r = hh * N_BRANCH
            outs.append(gates[r:r + 1] * oc_t[g][:, c] + gates[r + 1:r + 2] * os_t[:, c]
                        + gates[r + 2:r + 3] * ow_t[g][:, c])
    o_ref[0] = jnp.concatenate(outs, axis=0).T.astype(o_ref.dtype)


def _attention(q_t, gates_t, kc, vct, ovl_t, ks, vts, kw, vtw):
    B, _, _, S = q_t.shape
    G = N_KV_GROUPS
    hpg = HEADS_PER_GROUP
    ng = ATTN_GROUPS
    assert KC == TQ and WINDOW % TQ == 0 and G % ng == 0
    cols = hpg * TQ
    key = np.arange(TQ)[:, None]
    t_local = (np.arange(cols) % TQ)[None, :]
    cend = jnp.asarray(CMP_STRIDE * np.arange(kc.shape[2])[:, None] + CMP_LEN - 1 - t_local, jnp.int32)
    edges = jnp.asarray(np.where(np.stack([key <= t_local, key > t_local]), 0.0, NEG), F32)
    per_bg = lambda b, g, qi: (b, g, 0, 0)
    return pl.pallas_call(
        _attn_kernel,
        grid=(B, G // ng, S // TQ),
        in_specs=[
            pl.BlockSpec((1, ng * hpg, HEAD_DIM, TQ), lambda b, g, qi: (b, g, 0, qi)),
            pl.BlockSpec((1, ng, GATE_ROWS, TQ), lambda b, g, qi: (b, g, 0, qi)),
            pl.BlockSpec((1, ng) + kc.shape[2:], per_bg),
            pl.BlockSpec((1, ng) + vct.shape[2:], per_bg),
            pl.BlockSpec(ovl_t.shape, lambda b, g, qi: (0, 0)),
            pl.BlockSpec(cend.shape, lambda b, g, qi: (0, 0)),
            pl.BlockSpec(edges.shape, lambda b, g, qi: (0, 0, 0)),
            pl.BlockSpec((1, ng) + ks.shape[2:], per_bg),
            pl.BlockSpec((1, ng) + vts.shape[2:], per_bg),
            pl.BlockSpec((1, ng) + kw.shape[2:], per_bg),
            pl.BlockSpec((1, ng) + vtw.shape[2:], per_bg),
        ],
        out_specs=pl.BlockSpec((1, TQ, ng * hpg * HEAD_DIM), lambda b, g, qi: (b, qi, g)),
        out_shape=jax.ShapeDtypeStruct((B, S, N_HEADS * HEAD_DIM), BF16),
        scratch_shapes=[
            pltpu.VMEM((ng, 1, cols), F32),
            pltpu.VMEM((ng, LANES, cols), F32),
            pltpu.VMEM((ng, KC, cols), F32),
            pltpu.VMEM((ng, KC, cols), F32),
            pltpu.VMEM((ng, WINDOW + TQ, cols), F32),
        ],
        compiler_params=_params(3),
        name="sparse_attention",
    )(q_t, gates_t, kc, vct, ovl_t, cend, edges, ks, vts, kw, vtw)


def _rope_tables(pos):
    inv_freq = ROPE_THETA ** (-jnp.arange(0, ROT_DIM, 2, dtype=F32) / ROT_DIM)
    ang = pos.astype(F32)[..., None] * inv_freq
    cs = jnp.concatenate([jnp.cos(ang), jnp.sin(ang)], axis=-1)
    half = ROT_DIM // 2
    place = np.zeros((3, ROT_DIM, LANES), np.float32)
    const = np.zeros((3, LANES), np.float32)
    for lane in range(LANES):
        d = lane % HEAD_DIM
        if d < ROT_DIM:
            place[0, d % half, lane] = 1.0
            if d < half:
                place[1, half + d, lane] = -1.0
            else:
                place[2, half + d - half, lane] = 1.0
        else:
            const[0, lane] = 1.0
    tabs = jnp.einsum('...k,tkn->t...n', cs, jnp.asarray(place), precision=lax.Precision.HIGHEST)
    return tuple(tabs[t] + const[t] for t in range(3))


def _overlap_t(n_chunk, n_slc):
    c_start = CMP_STRIDE * np.arange(n_chunk)
    blk = np.arange(n_slc)
    ovl = ((c_start[None] < (blk[:, None] + 1) * SLC_BLOCK)
           & (c_start[None] + CMP_LEN > blk[:, None] * SLC_BLOCK)
           & (np.arange(n_chunk)[None] < n_chunk - 1))
    return jnp.asarray(ovl.astype(np.float32))


def kernel(x, p, positions, norm_mix, pool_w, pool_scale, norm_kv, w_kv, cmp_pos_k, cmp_w1_k, cmp_w2_k, cmp_pos_v, cmp_w1_v, cmp_w2_v, w_in_b, w_out_b, norm_ffn, ffn_up, ffn_conv, ffn_conv_b, ffn_down, norm_ple, ple_gate, ple_proj, norm_final):
    B, S, D = x.shape
    n_chunk = S // CMP_STRIDE
    assert S // SLC_BLOCK == LANES - HEAD_DIM
    rc, ra, rb = _rope_tables(positions)
    end_idx = jnp.minimum(CMP_STRIDE * jnp.arange(n_chunk) + CMP_LEN - 1, S - 1)
    rc_c, ra_c, rb_c = _rope_tables(positions[:, end_idx])
    ovl_t = _overlap_t(n_chunk, S // SLC_BLOCK)

    stack = lambda g: g.reshape(g.shape[0], 1, g.shape[1])
    norm_mix3, norm_ffn3, norm_ple3, pool_scale3 = map(stack, (norm_mix, norm_ffn, norm_ple, pool_scale))
    conv_b3 = stack(ffn_conv_b)
    pool_w16, up16, down16, gate16, proj16, w_out16 = (
        w.astype(BF16) for w in (pool_w, ffn_up, ffn_down, ple_gate, ple_proj, w_out_b))
    w_q16 = _q_weights(w_in_b)

    kv = None
    for i in range(DEPTH):
        attn = None
        j = i - N_A
        if i >= N_A:
            if i == N_A:
                kc_raw, vc_raw, ks, vts, kw, vtw = _kv_project(x, norm_kv, w_kv, rc, ra, rb)
                kc, vct = _compress(kc_raw, vc_raw, cmp_pos_k, cmp_w1_k, cmp_w2_k,
                                    cmp_pos_v, cmp_w1_v, cmp_w2_v, rc_c, ra_c, rb_c)
                kv = (kc, vct, ovl_t, ks, vts, kw, vtw)
            q_t, gates_t = _q_project(x, i, norm_mix3, j, w_q16, rc, ra, rb)
            attn = _attention(q_t, gates_t, *kv)
        x = _ffn_layer(x, i, attn, w_out16, j, norm_mix3, pool_w16, pool_scale3, norm_ffn3, up16, ffn_conv, conv_b3,
                       down16, norm_ple3, gate16, p, proj16, norm_final if i == DEPTH - 1 else None)
    return x
```

```python
import functools
import math

import jax
import jax.numpy as jnp
import numpy as np
from jax import lax
from jax.experimental import pallas as pl
from jax.experimental.pallas import tpu as pltpu

D_MODEL = 1024
DEPTH = 4
N_A = DEPTH // 2
POOL_WINDOWS = (2, 4, 8, 16)
POOL_GROUP = D_MODEL // len(POOL_WINDOWS)
POOL_HALO = 16
N_HEADS = 16
HEAD_DIM = 64
N_KV_GROUPS = 4
HEADS_PER_GROUP = N_HEADS // N_KV_GROUPS
CMP_STRIDE = 16
CMP_LEN = 2 * CMP_STRIDE
CMP_HIDDEN = 128
SLC_BLOCK = 64
N_SELECT = 16
WINDOW = 512
N_BRANCH = 3
ROPE_THETA = 500000.0
ROT_DIM = HEAD_DIM // 4
D_FF = 2816
CONV_W = 3
PLE_DIM = 256
EPS = 1e-6
NEG = -1e30
FORCE_SCORE = 1e4

LANES = 128
SUBLANES = 8
VMEM_LIMIT = 48 * 1024 * 1024

TS = 512
FC = 256
FFN_UNROLL = 12
TQ = 256
KC = TQ
ATTN_GROUPS = 2
GATE_ROWS = 16

BF16 = jnp.bfloat16
F32 = jnp.float32


def _dot(a, b):
    return jnp.dot(a, b, preferred_element_type=F32)


def _rmsnorm(x, g):
    return x * lax.rsqrt(jnp.mean(x * x, axis=-1, keepdims=True) + EPS) * g


def _sigmoid(x):
    return 1.0 / (1.0 + jnp.exp(-x))


def _rope128(x, c, a, b):
    return x * c + pltpu.roll(x, LANES - ROT_DIM // 2, 1) * a + pltpu.roll(x, ROT_DIM // 2, 1) * b


def _params(n_grid):
    return pltpu.CompilerParams(dimension_semantics=("arbitrary",) * n_grid,
                                vmem_limit_bytes=VMEM_LIMIT)


def _pool_mix(h, history, s, w_ref, scale):
    ext = jnp.concatenate([history, h], axis=0)
    sums = {1: ext}
    w = 1
    while w < POOL_WINDOWS[-1]:
        prev = sums[w]
        sums[2 * w] = prev + pltpu.roll(prev, w, 0)
        w *= 2
    t = s * TS + lax.broadcasted_iota(jnp.int32, (TS, 1), 0)
    outs = []
    for gi, win in enumerate(POOL_WINDOWS):
        lo, hi = gi * POOL_GROUP, (gi + 1) * POOL_GROUP
        tot = sums[win][POOL_HALO:, lo:hi]
        cnt = jnp.minimum(t + 1, win).astype(F32)
        u = tot / cnt - h[:, lo:hi]
        outs.append(_dot(u.astype(BF16), w_ref[gi]))
    return jnp.concatenate(outs, axis=-1) * scale


def _ffn_kernel(*refs, has_attn, final_norm):
    it = iter(refs)
    x_ref = next(it)
    if has_attn:
        attn_ref, wout_ref = next(it), next(it)
    else:
        gmix_ref, poolw_ref, pscale_ref = next(it), next(it), next(it)
    gffn_ref, wup_ref, cw_ref, cb_ref, wdown_ref = (next(it) for _ in range(5))
    gple_ref, pgate_ref, p_ref, pproj_ref = (next(it) for _ in range(4))
    if final_norm:
        gfin_ref = next(it)
    o_ref = next(it)
    xres_ref, h_ref, acc_ref, carry_ref = (next(it) for _ in range(4))
    u_refs = (next(it), next(it))
    emb_ref = next(it)
    if not has_attn:
        hist_ref = next(it)

    s = pl.program_id(1)
    k = pl.program_id(2)
    n_k = (D_FF // FC + 1) // FFN_UNROLL
    run = lambda f: f()
    first_step = run if n_k == 1 else pl.when(k == 0)
    last_step = run if n_k == 1 else pl.when(k == n_k - 1)

    @pl.when(s == 0 if n_k == 1 else (s == 0) & (k == 0))
    def _():
        carry_ref[...] = jnp.zeros_like(carry_ref)
        if not has_attn:
            hist_ref[...] = jnp.zeros_like(hist_ref)

    @first_step
    def _():
        xin = x_ref[...]
        if has_attn:
            xin = xin + _dot(attn_ref[...], wout_ref[...])
        else:
            hm = _rmsnorm(xin, gmix_ref[...])
            xin = xin + _pool_mix(hm, hist_ref[...], s, poolw_ref, pscale_ref[...])
            hist_ref[...] = hm[TS - POOL_HALO:, :]
        xres_ref[...] = xin
        emb_ref[...] = _dot(p_ref[...].astype(BF16), pproj_ref[...])
        h_ref[...] = _rmsnorm(xin, gffn_ref[...]).astype(BF16)
        acc_ref[...] = jnp.zeros_like(acc_ref)
        u_refs[1][...] = jnp.zeros_like(u_refs[1])

    def aligned(start):
        return start if isinstance(start, int) else pl.multiple_of(start, FC)

    def cols(chunk, gate):
        return pl.ds(aligned(gate * D_FF + chunk * FC), FC)

    def half_step(u_in_ref, u_out_ref, chunk, valid, chunk_up):
        for slot in range(2 if chunk_up is not None else 0):
            u_out_ref[slot, SUBLANES:, :] = _dot(h_ref[...], wup_ref[:, cols(chunk_up, slot)])

        if valid is False:
            return

        def conv_half(slot):
            u = u_in_ref[slot, SUBLANES:, :]
            last = carry_ref[chunk, slot]
            u_in_ref[slot, 0:SUBLANES, :] = last
            carry_ref[chunk, slot] = u[TS - SUBLANES:, :] if valid is None else jnp.where(
                valid, u[TS - SUBLANES:, :], last)
            cw = cw_ref[:, cols(chunk, slot)]
            out = cw[CONV_W - 1:CONV_W] * u + cb_ref[:, cols(chunk, slot)]
            for back in range(1, CONV_W):
                shifted = u_in_ref[slot, pl.ds(SUBLANES - back, TS), :]
                out = out + cw[CONV_W - 1 - back:CONV_W - back] * shifted
            return out

        a = conv_half(0)
        g = conv_half(1)
        act = g * _sigmoid(g) * a
        if valid is not None:
            act = jnp.where(valid, act, 0.0)
        acc_ref[...] += _dot(act.astype(BF16), wdown_ref[pl.ds(aligned(chunk * FC), FC), :])

    n_c = D_FF // FC
    for r in range(FFN_UNROLL):
        if n_k == 1:
            chunk, chunk_up, valid = max(r - 1, 0), (r if r < n_c else None), (False if r == 0 else None)
        else:
            chunk, chunk_up = jnp.maximum(FFN_UNROLL * k + r - 1, 0), jnp.minimum(FFN_UNROLL * k + r, n_c - 1)
            valid = (k > 0) if r == 0 else None
        half_step(u_refs[(r + 1) % 2], u_refs[r % 2], chunk, valid, chunk_up)

    @last_step
    def _():
        x2 = xres_ref[...] + acc_ref[...]
        hp = _rmsnorm(x2, gple_ref[...]).astype(BF16)
        gate = _sigmoid(_dot(hp, pgate_ref[...]))
        x3 = x2 + gate * emb_ref[...]
        if final_norm:
            x3 = _rmsnorm(x3, gfin_ref[...])
        o_ref[...] = x3


def _ffn_layer(x, i, attn, w_out, j, g_mix, pool_w, pool_scale, g_ffn, w_up, conv_w, conv_b, w_down,
               g_ple, ple_gate, p, ple_proj, g_final):
    B, S, D = x.shape
    n_c = D_FF // FC
    assert (n_c + 1) % FFN_UNROLL == 0 and FFN_UNROLL % 2 == 0
    n_k = (n_c + 1) // FFN_UNROLL
    has_attn = attn is not None
    final_norm = g_final is not None
    tok = lambda b, s, k: (b, s, 0)
    layer = lambda b, s, k: (i, 0, 0)
    vec = pl.BlockSpec((None, 1, D), layer)

    def resident(shape, index_map):
        return pl.BlockSpec((None,) + shape, index_map, pipeline_mode=pl.Buffered(1))

    args, specs = [x], [pl.BlockSpec((None, TS, D), tok)]
    if has_attn:
        args += [attn, w_out]
        specs += [pl.BlockSpec((None, TS, D), tok), resident((D, D), lambda b, s, k: (j, 0, 0))]
    else:
        args += [g_mix, pool_w, pool_scale]
        specs += [vec, resident((len(POOL_WINDOWS), POOL_GROUP, POOL_GROUP), lambda b, s, k: (i, 0, 0, 0)), vec]
    args += [g_ffn, w_up, conv_w, conv_b, w_down, g_ple, ple_gate, p, ple_proj]
    specs += [
        vec,
        resident((D, 2 * D_FF), layer),
        resident((CONV_W, 2 * D_FF), layer),
        resident((1, 2 * D_FF), layer),
        resident((D_FF, D), layer),
        vec,
        resident((D, D), layer),
        pl.BlockSpec((None, None, TS, PLE_DIM), lambda b, s, k: (i, b, s, 0)),
        resident((PLE_DIM, D), layer),
    ]
    if final_norm:
        args.append(g_final.reshape(1, D))
        specs.append(pl.BlockSpec((1, D), lambda b, s, k: (0, 0)))
    return pl.pallas_call(
        functools.partial(_ffn_kernel, has_attn=has_attn, final_norm=final_norm),
        grid=(B, S // TS, n_k),
        in_specs=specs,
        out_specs=pl.BlockSpec((None, TS, D), tok),
        out_shape=jax.ShapeDtypeStruct(x.shape, F32),
        scratch_shapes=[
            pltpu.VMEM((TS, D), F32),
            pltpu.VMEM((TS, D), BF16),
            pltpu.VMEM((TS, D), F32),
            pltpu.VMEM((n_c, 2, SUBLANES, FC), F32),
            pltpu.VMEM((2, SUBLANES + TS, FC), F32),
            pltpu.VMEM((2, SUBLANES + TS, FC), F32),
            pltpu.VMEM((TS, D), F32),
        ] + ([] if has_attn else [pltpu.VMEM((POOL_HALO, D), F32)]),
        compiler_params=_params(3),
        name="ffn_embed",
    )(*args)


def _kv_kernel(x_ref, g_ref, w_ref, rc_ref, ra_ref, rb_ref,
               kc_ref, vc_ref, ks_ref, vts_ref, kw_ref, vtw_ref):
    s = pl.program_id(1)
    lane = lax.broadcasted_iota(jnp.int32, (TS, LANES), 1)

    @pl.when(s == 0)
    def _():
        flag = jnp.where(lane == HEAD_DIM, 1.0, 0.0).astype(BF16)
        for g in range(N_KV_GROUPS):
            kw_ref[0, g] = flag
            vtw_ref[0, g] = jnp.zeros((LANES, TS), BF16)

    @pl.when(s > 0)
    def _():
        h = _rmsnorm(x_ref[0], g_ref[...]).astype(BF16)
        kv = _dot(h, w_ref[...])
        gw = N_KV_GROUPS * HEAD_DIM
        rc, ra, rb = rc_ref[0], ra_ref[0], rb_ref[0]
        tok = (s - 1) * TS + lax.broadcasted_iota(jnp.int32, (TS, LANES), 0)
        onehot = jnp.where(lane - HEAD_DIM == tok // SLC_BLOCK, 1.0, 0.0)
        row = lax.broadcasted_iota(jnp.int32, (LANES - HEAD_DIM, TS), 0)
        ones_row = jnp.where(row == 0, 1.0, 0.0).astype(BF16)

        def slabs(i):
            return [kv[:, i * gw + j * LANES:i * gw + (j + 1) * LANES] for j in range(gw // LANES)]

        for g in range(N_KV_GROUPS):
            kc_ref[0, g] = kv[:, 0 * gw + g * HEAD_DIM:0 * gw + (g + 1) * HEAD_DIM]
            vc_ref[0, g] = kv[:, 1 * gw + g * HEAD_DIM:1 * gw + (g + 1) * HEAD_DIM]

        for i, k_ref, vt_ref in ((2, ks_ref, vts_ref), (4, kw_ref, vtw_ref)):
            for j, (kslab, vslab) in enumerate(zip(slabs(i), slabs(i + 1))):
                kr = _rope128(kslab, rc, ra, rb)
                vt = vslab.T
                for half in range(2):
                    g = 2 * j + half
                    kh = kr if half == 0 else pltpu.roll(kr, HEAD_DIM, 1)
                    upper = onehot if k_ref is ks_ref else 0.0
                    k_ref[0, g] = jnp.where(lane < HEAD_DIM, kh, upper).astype(BF16)
                    vt_ref[0, g, 0:HEAD_DIM, :] = vt[half * HEAD_DIM:(half + 1) * HEAD_DIM, :].astype(BF16)
                    vt_ref[0, g, HEAD_DIM:, :] = ones_row


def _kv_project(x, g, w_kv, rc, ra, rb):
    B, S, D = x.shape
    G = N_KV_GROUPS
    assert WINDOW == TS
    tile = lambda s: jnp.maximum(s - 1, 0)
    tok = lambda b, s: (b, tile(s), 0)
    rows = lambda b, s: (b, 0, tile(s), 0)
    cols = lambda b, s: (b, 0, 0, tile(s))
    return pl.pallas_call(
        _kv_kernel,
        grid=(B, S // TS + 1),
        in_specs=[
            pl.BlockSpec((1, TS, D), tok),
            pl.BlockSpec((1, D), lambda b, s: (0, 0)),
            pl.BlockSpec(w_kv.shape, lambda b, s: (0, 0)),
            pl.BlockSpec((1, TS, LANES), tok),
            pl.BlockSpec((1, TS, LANES), tok),
            pl.BlockSpec((1, TS, LANES), tok),
        ],
        out_specs=[
            pl.BlockSpec((1, G, TS, HEAD_DIM), rows),
            pl.BlockSpec((1, G, TS, HEAD_DIM), rows),
            pl.BlockSpec((1, G, TS, LANES), rows),
            pl.BlockSpec((1, G, LANES, TS), cols),
            pl.BlockSpec((1, G, TS, LANES), lambda b, s: (b, 0, s, 0)),
            pl.BlockSpec((1, G, LANES, TS), lambda b, s: (b, 0, 0, s)),
        ],
        out_shape=[
            jax.ShapeDtypeStruct((B, G, S, HEAD_DIM), F32),
            jax.ShapeDtypeStruct((B, G, S, HEAD_DIM), F32),
            jax.ShapeDtypeStruct((B, G, S, LANES), BF16),
            jax.ShapeDtypeStruct((B, G, LANES, S), BF16),
            jax.ShapeDtypeStruct((B, G, WINDOW + S, LANES), BF16),
            jax.ShapeDtypeStruct((B, G, LANES, WINDOW + S), BF16),
        ],
        compiler_params=_params(2),
        name="kv_project",
    )(x, g.reshape(1, D), w_kv.astype(BF16), rc, ra, rb)


def _gelu_tanh(x):
    return 0.5 * x * (1.0 + jnp.tanh(math.sqrt(2.0 / math.pi) * (x + 0.044715 * (x * x * x))))


def _compress_kernel(k_ref, v_ref, pk_ref, w1k_ref, w2k_ref, pv_ref, w1v_ref, w2v_ref,
                     rc_ref, ra_ref, rb_ref, kc_ref, vct_ref):
    n_chunk = kc_ref.shape[2]

    def hidden(x_ref, pos_ref, w1_ref):
        first = jnp.zeros((n_chunk, CMP_HIDDEN), F32)
        second = jnp.zeros((n_chunk, CMP_HIDDEN), F32)
        for r in range(CMP_STRIDE):
            tok = x_ref[0, 0, pl.ds(r, n_chunk, stride=CMP_STRIDE), :]
            lead = (tok + pos_ref[r:r + 1, :]).astype(BF16)
            trail = (tok + pos_ref[CMP_STRIDE + r:CMP_STRIDE + r + 1, :]).astype(BF16)
            first = first + _dot(lead, w1_ref[r * HEAD_DIM:(r + 1) * HEAD_DIM, :])
            second = second + _dot(trail, w1_ref[(CMP_STRIDE + r) * HEAD_DIM:(CMP_STRIDE + r + 1) * HEAD_DIM, :])
        hid = first + pltpu.roll(second, n_chunk - 1, 0)
        return _gelu_tanh(hid).astype(BF16)

    kc = _dot(hidden(k_ref, pk_ref, w1k_ref), w2k_ref[...])
    kc_ref[0, 0] = _rope128(kc, rc_ref[0], ra_ref[0], rb_ref[0])[:, :HEAD_DIM].astype(BF16)
    vct = lax.dot_general(w2v_ref[...], hidden(v_ref, pv_ref, w1v_ref), (((1,), (1,)), ((), ())),
                          preferred_element_type=F32)
    vct_ref[0, 0] = vct[:HEAD_DIM, :].astype(BF16)


def _compress(k_raw, v_raw, pos_k, w1_k, w2_k, pos_v, w1_v, w2_v, rc, ra, rb):
    B, G, S, _ = k_raw.shape
    n_chunk = S // CMP_STRIDE
    blk = lambda b, g: (b, g, 0, 0)
    const2 = lambda b, g: (0, 0)
    pad2 = lambda w: jnp.pad(w, ((0, 0), (0, LANES - HEAD_DIM))).astype(BF16)
    return pl.pallas_call(
        _compress_kernel,
        grid=(B, G),
        in_specs=[
            pl.BlockSpec((1, 1, S, HEAD_DIM), blk),
            pl.BlockSpec((1, 1, S, HEAD_DIM), blk),
            pl.BlockSpec((CMP_LEN, HEAD_DIM), const2),
            pl.BlockSpec((CMP_LEN * HEAD_DIM, CMP_HIDDEN), const2),
            pl.BlockSpec((CMP_HIDDEN, LANES), const2),
            pl.BlockSpec((CMP_LEN, HEAD_DIM), const2),
            pl.BlockSpec((CMP_LEN * HEAD_DIM, CMP_HIDDEN), const2),
            pl.BlockSpec((CMP_HIDDEN, LANES), const2),
            pl.BlockSpec((1, n_chunk, LANES), lambda b, g: (b, 0, 0)),
            pl.BlockSpec((1, n_chunk, LANES), lambda b, g: (b, 0, 0)),
            pl.BlockSpec((1, n_chunk, LANES), lambda b, g: (b, 0, 0)),
        ],
        out_specs=[
            pl.BlockSpec((1, 1, n_chunk, HEAD_DIM), blk),
            pl.BlockSpec((1, 1, HEAD_DIM, n_chunk), blk),
        ],
        out_shape=[
            jax.ShapeDtypeStruct((B, G, n_chunk, HEAD_DIM), BF16),
            jax.ShapeDtypeStruct((B, G, HEAD_DIM, n_chunk), BF16),
        ],
        compiler_params=_params(2),
        name="compress",
    )(k_raw, v_raw, pos_k, w1_k.astype(BF16), pad2(w2_k), pos_v, w1_v.astype(BF16), pad2(w2_v).T, rc, ra, rb)


def _qproj_kernel(x_ref, g_ref, wt_ref, rc_ref, ra_ref, rb_ref, q_ref, gate_ref):
    h = _rmsnorm(x_ref[0], g_ref[...]).astype(BF16)
    proj_t = lax.dot_general(wt_ref[...], h, (((1,), (1,)), ((), ())),
                             preferred_element_type=F32)
    rc, ra, rb = rc_ref[0], ra_ref[0], rb_ref[0]
    scale = HEAD_DIM ** -0.5 * math.log2(math.e)
    half = ROT_DIM // 2
    for j in range(N_HEADS // 2):
        x = proj_t[j * LANES:(j + 1) * LANES]
        qt = (x * rc + pltpu.roll(x, LANES - half, 0) * ra + pltpu.roll(x, half, 0) * rb) * scale
        q_ref[0, 2 * j] = qt[:HEAD_DIM].astype(BF16)
        q_ref[0, 2 * j + 1] = qt[HEAD_DIM:].astype(BF16)
    n_q = N_HEADS * HEAD_DIM
    for g in range(N_KV_GROUPS):
        gate_ref[0, g] = _sigmoid(proj_t[n_q + g * LANES:n_q + g * LANES + gate_ref.shape[2]])


def _q_weights(w_in_b):
    L, D, _ = w_in_b.shape
    n_q = N_HEADS * HEAD_DIM
    per_group = HEADS_PER_GROUP * N_BRANCH
    gate_w = w_in_b[:, :, n_q:].reshape(L, D, N_KV_GROUPS, per_group)
    gate_w = jnp.pad(gate_w, ((0, 0), (0, 0), (0, 0), (0, LANES - per_group))).reshape(L, D, N_KV_GROUPS * LANES)
    return jnp.concatenate([w_in_b[:, :, :n_q], gate_w], axis=2).astype(BF16).transpose(0, 2, 1)


def _q_project(x, i, g, j, w, rc, ra, rb):
    B, S, D = x.shape
    tok = lambda b, s: (b, s, 0)
    return pl.pallas_call(
        _qproj_kernel,
        grid=(B, S // TS),
        in_specs=[
            pl.BlockSpec((1, TS, D), tok),
            pl.BlockSpec((None, 1, D), lambda b, s: (i, 0, 0)),
            pl.BlockSpec((None,) + w.shape[1:], lambda b, s: (j, 0, 0)),
            pl.BlockSpec((1, LANES, TS), lambda b, s: (b, 0, s)),
            pl.BlockSpec((1, LANES, TS), lambda b, s: (b, 0, s)),
            pl.BlockSpec((1, LANES, TS), lambda b, s: (b, 0, s)),
        ],
        out_specs=[
            pl.BlockSpec((1, N_HEADS, HEAD_DIM, TS), lambda b, s: (b, 0, 0, s)),
            pl.BlockSpec((1, N_KV_GROUPS, GATE_ROWS, TS), lambda b, s: (b, 0, 0, s)),
        ],
        out_shape=[
            jax.ShapeDtypeStruct((B, N_HEADS, HEAD_DIM, S), BF16),
            jax.ShapeDtypeStruct((B, N_KV_GROUPS, GATE_ROWS, S), F32),
        ],
        compiler_params=_params(2),
        name="q_project",
    )(x, g, w, rc, ra, rb)


def _attn_kernel(q_ref, gate_ref, kc_ref, vct_ref, ovl_ref, cend_ref, edge_ref, ks_ref, vts_ref, kw_ref, vtw_ref,
                 o_ref, m_ref, acc_ref, sa_ref, sb_ref, sw_ref):
    qi = pl.program_id(2)
    s0 = qi * TQ
    hpg = HEADS_PER_GROUP
    n_slc = ovl_ref.shape[0]
    span = WINDOW + TQ
    w0 = pl.multiple_of(s0, TQ)
    groups = range(ATTN_GROUPS)

    flag_row = lax.broadcasted_iota(jnp.int32, (LANES - HEAD_DIM, hpg * TQ), 0) == 0
    flag_rows = jnp.where(flag_row, NEG, 0.0).astype(BF16)
    qt, sc = [], []
    for g in groups:
        qt.append(jnp.concatenate([q_ref[0, g * hpg + hh] for hh in range(hpg)], axis=1))
        sc.append(_dot(kc_ref[0, g], qt[g]))
        sw_ref[g] = _dot(kw_ref[0, g, pl.ds(w0, span), :], jnp.concatenate([qt[g], flag_rows], axis=0))

    mask_c = cend_ref[...] <= s0
    has_block = cend_ref[0:1, :] <= s0
    oc_t, imp_t = [], []
    for g in groups:
        s_c = jnp.where(mask_c, sc[g], NEG)
        e = jnp.exp2(s_c - jnp.max(s_c, axis=0, keepdims=True))
        p_c = e * jnp.where(has_block, 1.0 / jnp.maximum(jnp.sum(e, axis=0, keepdims=True), 1e-30), 0.0)
        oc_t.append(_dot(vct_ref[0, g], p_c.astype(BF16)))
        p_sum = p_c[:, 0:TQ]
        for hh in range(1, hpg):
            p_sum = p_sum + p_c[:, hh * TQ:(hh + 1) * TQ]
        imp_t.append(jnp.dot(ovl_ref[...], p_sum, precision=lax.Precision.HIGHEST,
                             preferred_element_type=F32))

    blk = lax.broadcasted_iota(jnp.int32, (n_slc, TQ), 0)
    t_col = s0 + lax.broadcasted_iota(jnp.int32, (n_slc, TQ), 1)
    cur = t_col // SLC_BLOCK
    forced = (blk == 0) | (blk == cur) | (blk == cur - 1)
    valid = blk * SLC_BLOCK <= t_col
    blk_in = lax.broadcasted_iota(jnp.int32, (SUBLANES, TQ), 0)
    qt_sel = []
    for g in groups:
        score = jnp.where(valid, jnp.where(forced, FORCE_SCORE, imp_t[g]), NEG)
        rows = [score[r:r + SUBLANES] for r in range(0, n_slc, SUBLANES)]
        ranks = [jnp.zeros((SUBLANES, TQ), F32) for _ in rows]
        for m in range(n_slc):
            other = score[m:m + 1, :]
            for gi, grp in enumerate(rows):
                lo = gi * SUBLANES
                if lo > m:
                    ahead = other >= grp
                elif lo + SUBLANES - 1 <= m:
                    ahead = other > grp
                else:
                    ahead = (other > grp) | ((other == grp) & (blk_in + lo > m))
                ranks[gi] = ranks[gi] + jnp.where(ahead, 1.0, 0.0)
        rank = jnp.concatenate(ranks, axis=0)
        bias_t = jnp.where(valid & (rank < min(N_SELECT, n_slc)), 0.0, NEG).astype(BF16)
        qt_sel.append(jnp.concatenate([qt[g], jnp.concatenate([bias_t] * hpg, axis=1)], axis=0))

    def scores(j, out_ref):
        for g in groups:
            out_ref[g] = _dot(ks_ref[0, g, pl.ds(pl.multiple_of(j * KC, KC), KC), :], qt_sel[g])

    def absorb(in_ref, j, diagonal=False):
        for g in groups:
            s = in_ref[g]
            if diagonal:
                s = s + edge_ref[0]
            m_old = m_ref[g]
            m_new = jnp.maximum(m_old, jnp.max(s, axis=0, keepdims=True))
            p = jnp.exp2(s - m_new).astype(BF16)
            v_t = vts_ref[0, g, :, pl.ds(pl.multiple_of(j * KC, KC), KC)]
            acc_ref[g] = jnp.exp2(m_old - m_new) * acc_ref[g] + _dot(v_t, p)
            m_ref[g] = m_new

    m_ref[...] = jnp.full_like(m_ref, NEG)
    acc_ref[...] = jnp.zeros_like(acc_ref)
    scores(0, sa_ref)

    ow_t = []
    for g in groups:
        lo_edge = sw_ref[g, 0:TQ] + edge_ref[1]
        middle = sw_ref[g, TQ:WINDOW]
        hi_edge = sw_ref[g, WINDOW:] + edge_ref[0]
        m_w = jnp.maximum(jnp.maximum(jnp.max(lo_edge, axis=0, keepdims=True),
                                      jnp.max(middle, axis=0, keepdims=True)),
                          jnp.max(hi_edge, axis=0, keepdims=True))
        rw = (_dot(vtw_ref[0, g, :, pl.ds(w0, TQ)], jnp.exp2(lo_edge - m_w).astype(BF16))
              + _dot(vtw_ref[0, g, :, pl.ds(w0 + TQ, WINDOW - TQ)], jnp.exp2(middle - m_w).astype(BF16))
              + _dot(vtw_ref[0, g, :, pl.ds(w0 + WINDOW, TQ)], jnp.exp2(hi_edge - m_w).astype(BF16)))
        ow_t.append(rw[:HEAD_DIM] * (1.0 / jnp.maximum(rw[HEAD_DIM:HEAD_DIM + 1], 1e-30)))

    def pair(i, carry):
        scores(2 * i + 1, sb_ref)
        absorb(sa_ref, 2 * i)
        scores(2 * i + 2, sa_ref)
        absorb(sb_ref, 2 * i + 1)
        return carry

    lax.fori_loop(0, qi // 2, pair, 0)

    @pl.when(qi % 2 == 0)
    def _():
        absorb(sa_ref, qi, diagonal=True)

    @pl.when(qi % 2 == 1)
    def _():
        scores(qi, sb_ref)
        absorb(sa_ref, qi - 1)
        absorb(sb_ref, qi, diagonal=True)

    outs = []
    for g in groups:
        acc = acc_ref[g]
        os_t = acc[:HEAD_DIM] * (1.0 / acc[HEAD_DIM:HEAD_DIM + 1])
        gates = gate_ref[0, g]
        for hh in range(hpg):
            c = slice(hh * TQ, (hh + 1) * TQ)
            r = hh * N_BRANCH
            outs.append(gates[r:r + 1] * oc_t[g][:, c] + gates[r + 1:r + 2] * os_t[:, c]
                        + gates[r + 2:r + 3] * ow_t[g][:, c])
    o_ref[0] = jnp.concatenate(outs, axis=0).T.astype(o_ref.dtype)


def _attention(q_t, gates_t, kc, vct, ovl_t, ks, vts, kw, vtw):
    B, _, _, S = q_t.shape
    G = N_KV_GROUPS
    hpg = HEADS_PER_GROUP
    ng = ATTN_GROUPS
    assert KC == TQ and WINDOW % TQ == 0 and G % ng == 0
    cols = hpg * TQ
    key = np.arange(TQ)[:, None]
    t_local = (np.arange(cols) % TQ)[None, :]
    cend = jnp.asarray(CMP_STRIDE * np.arange(kc.shape[2])[:, None] + CMP_LEN - 1 - t_local, jnp.int32)
    edges = jnp.asarray(np.where(np.stack([key <= t_local, key > t_local]), 0.0, NEG), F32)
    per_bg = lambda b, g, qi: (b, g, 0, 0)
    return pl.pallas_call(
        _attn_kernel,
        grid=(B, G // ng, S // TQ),
        in_specs=[
            pl.BlockSpec((1, ng * hpg, HEAD_DIM, TQ), lambda b, g, qi: (b, g, 0, qi)),
            pl.BlockSpec((1, ng, GATE_ROWS, TQ), lambda b, g, qi: (b, g, 0, qi)),
            pl.BlockSpec((1, ng) + kc.shape[2:], per_bg),
            pl.BlockSpec((1, ng) + vct.shape[2:], per_bg),
            pl.BlockSpec(ovl_t.shape, lambda b, g, qi: (0, 0)),
            pl.BlockSpec(cend.shape, lambda b, g, qi: (0, 0)),
            pl.BlockSpec(edges.shape, lambda b, g, qi: (0, 0, 0)),
            pl.BlockSpec((1, ng) + ks.shape[2:], per_bg),
            pl.BlockSpec((1, ng) + vts.shape[2:], per_bg),
            pl.BlockSpec((1, ng) + kw.shape[2:], per_bg),
            pl.BlockSpec((1, ng) + vtw.shape[2:], per_bg),
        ],
        out_specs=pl.BlockSpec((1, TQ, ng * hpg * HEAD_DIM), lambda b, g, qi: (b, qi, g)),
        out_shape=jax.ShapeDtypeStruct((B, S, N_HEADS * HEAD_DIM), BF16),
        scratch_shapes=[
            pltpu.VMEM((ng, 1, cols), F32),
            pltpu.VMEM((ng, LANES, cols), F32),
            pltpu.VMEM((ng, KC, cols), F32),
            pltpu.VMEM((ng, KC, cols), F32),
            pltpu.VMEM((ng, WINDOW + TQ, cols), F32),
        ],
        compiler_params=_params(3),
        name="sparse_attention",
    )(q_t, gates_t, kc, vct, ovl_t, cend, edges, ks, vts, kw, vtw)


def _rope_tables(pos):
    inv_freq = ROPE_THETA ** (-jnp.arange(0, ROT_DIM, 2, dtype=F32) / ROT_DIM)
    ang = pos.astype(F32)[..., None] * inv_freq
    cs = jnp.concatenate([jnp.cos(ang), jnp.sin(ang)], axis=-1)
    half = ROT_DIM // 2
    place = np.zeros((3, ROT_DIM, LANES), np.float32)
    const = np.zeros((3, LANES), np.float32)
    for lane in range(LANES):
        d = lane % HEAD_DIM
        if d < ROT_DIM:
            place[0, d % half, lane] = 1.0
            if d < half:
                place[1, half + d, lane] = -1.0
            else:
                place[2, half + d - half, lane] = 1.0
        else:
            const[0, lane] = 1.0
    tabs = jnp.einsum('...k,tkn->t...n', cs, jnp.asarray(place), precision=lax.Precision.HIGHEST)
    return tuple(tabs[t] + const[t] for t in range(3))


def _overlap_t(n_chunk, n_slc):
    c_start = CMP_STRIDE * np.arange(n_chunk)
    blk = np.arange(n_slc)
    ovl = ((c_start[None] < (blk[:, None] + 1) * SLC_BLOCK)
           & (c_start[None] + CMP_LEN > blk[:, None] * SLC_BLOCK)
           & (np.arange(n_chunk)[None] < n_chunk - 1))
    return jnp.asarray(ovl.astype(np.float32))


def kernel(x, p, positions, norm_mix, pool_w, pool_scale, norm_kv, w_kv, cmp_pos_k, cmp_w1_k, cmp_w2_k, cmp_pos_v, cmp_w1_v, cmp_w2_v, w_in_b, w_out_b, norm_ffn, ffn_up, ffn_conv, ffn_conv_b, ffn_down, norm_ple, ple_gate, ple_proj, norm_final):
    B, S, D = x.shape
    n_chunk = S // CMP_STRIDE
    assert S // SLC_BLOCK == LANES - HEAD_DIM
    rc, ra, rb = _rope_tables(positions)
    end_idx = jnp.minimum(CMP_STRIDE * jnp.arange(n_chunk) + CMP_LEN - 1, S - 1)
    rc_c, ra_c, rb_c = _rope_tables(positions[:, end_idx])
    ovl_t = _overlap_t(n_chunk, S // SLC_BLOCK)

    stack = lambda g: g.reshape(g.shape[0], 1, g.shape[1])
    norm_mix3, norm_ffn3, norm_ple3, pool_scale3 = map(stack, (norm_mix, norm_ffn, norm_ple, pool_scale))
    conv_b3 = stack(ffn_conv_b)
    pool_w16, up16, down16, gate16, proj16, w_out16 = (
        w.astype(BF16) for w in (pool_w, ffn_up, ffn_down, ple_gate, ple_proj, w_out_b))
    w_q16 = _q_weights(w_in_b)

    kv = None
    for i in range(DEPTH):
        attn = None
        j = i - N_A
        if i >= N_A:
            if i == N_A:
                kc_raw, vc_raw, ks, vts, kw, vtw = _kv_project(x, norm_kv, w_kv, rc, ra, rb)
                kc, vct = _compress(kc_raw, vc_raw, cmp_pos_k, cmp_w1_k, cmp_w2_k,
                                    cmp_pos_v, cmp_w1_v, cmp_w2_v, rc_c, ra_c, rb_c)
                kv = (kc, vct, ovl_t, ks, vts, kw, vtw)
            q_t, gates_t = _q_project(x, i, norm_mix3, j, w_q16, *(jnp.swapaxes(t, 1, 2) for t in (rc, ra, rb)))
            attn = _attention(q_t, gates_t, *kv)
        x = _ffn_layer(x, i, attn, w_out16, j, norm_mix3, pool_w16, pool_scale3, norm_ffn3, up16, ffn_conv, conv_b3,
                       down16, norm_ple3, gate16, p, proj16, norm_final if i == DEPTH - 1 else None)
    return x
```
